```python
import math
import jax
import jax.numpy as jnp
from jax import lax
import numpy as np

D_MODEL = 2048
BATCH = 4
SEQ = 4096
DEPTH = 4

GRID_W = 64
CTX_LEN = 256
N_BRANCH = 4
BRANCH_W = D_MODEL // 4
FN_GROUPS = 4
FN_GW = BRANCH_W // FN_GROUPS
DN_HEADS = 4
DN_DK = BRANCH_W // DN_HEADS
DN_DV = BRANCH_W // DN_HEADS
DN_CONV = 3
DN_CHUNK = 64
HG_HEADS = 4
HG_DK = BRANCH_W // HG_HEADS
HG_DV = BRANCH_W // HG_HEADS
HG_CHUNK = 64
DA_HEADS = 4
DA_DH = BRANCH_W // (2 * DA_HEADS)
DA_DV = 2 * DA_DH
Q_BLOCK = 128
ROPE_THETA = 10000.0
EPS = 1e-6

IN_LAYOUT = (
    ('fn_u', BRANCH_W), ('fn_z', BRANCH_W),
    ('dn_q', BRANCH_W), ('dn_k', BRANCH_W), ('dn_v', BRANCH_W), ('dn_z', BRANCH_W),
    ('dn_a', 2 * DN_HEADS), ('dn_b', 2 * DN_HEADS),
    ('hg_q', BRANCH_W), ('hg_f', 2 * BRANCH_W), ('hg_i', BRANCH_W), ('hg_z', BRANCH_W),
    ('da_q', BRANCH_W), ('da_k', BRANCH_W), ('da_v', BRANCH_W), ('da_z', BRANCH_W),
    ('gate', N_BRANCH * D_MODEL),
)
IN_W = sum(w for _, w in IN_LAYOUT)

kernel_name = 'hybrid_parallel_gated_diffusion_trunk'


def _rms(x, g):
    xf = x.astype(jnp.float32)
    y = xf * lax.rsqrt(jnp.mean(xf * xf, axis=-1, keepdims=True) + EPS)
    return (y * g.astype(jnp.float32)).astype(x.dtype)


def _head_rms(o, g, dtype):
    y = o * lax.rsqrt(jnp.mean(o * o, axis=-1, keepdims=True) + EPS) * g.astype(jnp.float32)
    return y.reshape(o.shape[:2] + (-1,)).astype(dtype)


def _l2n(x):
    return x * lax.rsqrt(jnp.sum(x * x, axis=-1, keepdims=True) + EPS)


def _split_proj(p):
    out, off = {}, 0
    for name, w in IN_LAYOUT:
        out[name] = p[..., off:off + w]
        off += w
    return out


def _stream_in(x, mod, norm_g, w_in):
    shift, scale, gate = jnp.split(mod, 3, axis=-1)
    h = _rms(x, norm_g) * (1.0 + scale) + shift
    return _split_proj(h @ w_in), gate


def _to_chunks(a, c):
    b, t = a.shape[:2]
    a = a.reshape((b, t // c, c) + a.shape[2:])
    return jnp.moveaxis(a, (1, 3), (0, 2))


def _from_chunks(a):
    a = jnp.moveaxis(a, (0, 2), (1, 3))
    return a.reshape((a.shape[0], -1) + a.shape[3:])


def _fourier(u, w, b):
    bsz, t, _ = u.shape
    ug = u.astype(jnp.float32).reshape(bsz, t, FN_GROUPS, FN_GW)
    f = jnp.fft.fft2(ug, axes=(1, 3), norm='ortho').real
    return f.reshape(bsz, t, BRANCH_W).astype(u.dtype) @ w + b


def _short_conv(u, w):
    pad = DN_CONV // 2
    t = u.shape[1]
    up = jnp.pad(u, ((0, 0), (pad, pad), (0, 0)))
    y = sum(up[:, k:k + t] * w[k] for k in range(DN_CONV))
    return jax.nn.silu(y)


def _dn_prep(p, conv_w, a_log, dt_bias):
    bsz, t, _ = p['dn_q'].shape
    qkv = _short_conv(jnp.concatenate([p['dn_q'], p['dn_k'], p['dn_v']], axis=-1), conv_w)
    q, k, v = jnp.split(qkv.astype(jnp.float32), 3, axis=-1)
    q = _l2n(q.reshape(bsz, t, DN_HEADS, DN_DK)) * DN_DK ** -0.5
    k = _l2n(k.reshape(bsz, t, DN_HEADS, DN_DK))
    v = v.reshape(bsz, t, DN_HEADS, DN_DV)
    a = p['dn_a'].astype(jnp.float32).reshape(bsz, t, 2, DN_HEADS)
    g = -jnp.exp(a_log.astype(jnp.float32)) * jax.nn.softplus(a + dt_bias.astype(jnp.float32))
    beta = jax.nn.sigmoid(p['dn_b'].astype(jnp.float32).reshape(bsz, t, 2, DN_HEADS))
    return q, k, v, g, beta


def _delta_chunked(q, k, v, g, beta, s0):
    c = DN_CHUNK
    q, k, v = _to_chunks(q, c), _to_chunks(k, c), _to_chunks(v, c)
    g, beta = _to_chunks(g, c), _to_chunks(beta, c)
    gc = jnp.cumsum(g, axis=-1)
    idx = jnp.arange(c)
    incl = idx[:, None] >= idx[None, :]
    strict = idx[:, None] > idx[None, :]
    decay = jnp.exp(jnp.where(incl, gc[..., :, None] - gc[..., None, :], -jnp.inf))
    kb = k * beta[..., None]
    a_kk = jnp.where(strict, jnp.einsum('nbhid,nbhjd->nbhij', kb, k) * decay, 0.0)
    eye = jnp.eye(c, dtype=a_kk.dtype)
    t_inv = lax.linalg.triangular_solve(a_kk + eye, jnp.broadcast_to(eye, a_kk.shape),
                                        left_side=True, lower=True, unit_diagonal=True)
    u = t_inv @ (v * beta[..., None])
    w = t_inv @ (kb * jnp.exp(gc)[..., None])
    a_qk = jnp.where(incl, jnp.einsum('nbhid,nbhjd->nbhij', q, k) * decay, 0.0)
    q_dec = q * jnp.exp(gc)[..., None]
    k_dec = k * jnp.exp(gc[..., -1:] - gc)[..., None]
    g_last = jnp.exp(gc[..., -1])[..., None, None]

    def step(s, xs):
        u_n, w_n, aqk_n, qd_n, kd_n, gl_n = xs
        v_new = u_n - w_n @ s
        o_n = qd_n @ s + aqk_n @ v_new
        s = s * gl_n + jnp.einsum('bhcd,bhce->bhde', kd_n, v_new)
        return s, o_n

    s, o = lax.scan(step, s0, (u, w, a_qk, q_dec, k_dec, g_last))
    return _from_chunks(o), s


def _dn_bidir(q, k, v, g, beta, s_f, s_b):
    flip = lambda a: jnp.flip(a, axis=1)
    o_f, s_f = _delta_chunked(q, k, v, g[:, :, 0], beta[:, :, 0], s_f)
    o_b, s_b = _delta_chunked(flip(q), flip(k), flip(v), flip(g[:, :, 1]), flip(beta[:, :, 1]), s_b)
    return o_f + flip(o_b), s_f, s_b


def _hg_prep(p, lb):
    bsz, t, _ = p['hg_q'].shape
    q = jax.nn.silu(p['hg_q'].astype(jnp.float32)).reshape(bsz, t, HG_HEADS, HG_DK)
    f = p['hg_f'].astype(jnp.float32).reshape(bsz, t, 2, HG_HEADS * HG_DK)
    log_f = jnp.logaddexp(jnp.log(lb), jnp.log1p(-lb) + jax.nn.log_sigmoid(f))
    k = (1.0 - lb) * jax.nn.sigmoid(-f)
    v = p['hg_i'].astype(jnp.float32).reshape(bsz, t, HG_HEADS, HG_DV)
    return (q, k.reshape(bsz, t, 2, HG_HEADS, HG_DK), v,
            log_f.reshape(bsz, t, 2, HG_HEADS, HG_DK))


def _hgrn2_chunked(q, k, v, log_f, s0):
    c = HG_CHUNK
    q, k, v, log_f = (_to_chunks(a, c) for a in (q, k, v, log_f))
    gc = jnp.cumsum(log_f, axis=-2)
    idx = jnp.arange(c)
    incl = idx[:, None] >= idx[None, :]

    def step(s, xs):
        q_n, k_n, v_n, g_n = xs
        diff = g_n[..., :, None, :] - g_n[..., None, :, :]
        dec = jnp.exp(jnp.where(incl[:, :, None], diff, -jnp.inf))
        a_qk = jnp.einsum('bhid,bhjd,bhijd->bhij', q_n, k_n, dec)
        o_n = (q_n * jnp.exp(g_n)) @ s + a_qk @ v_n
        g_last = g_n[..., -1:, :]
        s = (jnp.exp(g_last[..., 0, :])[..., None] * s
             + jnp.einsum('bhcd,bhce->bhde', k_n * jnp.exp(g_last - g_n), v_n))
        return s, o_n

    s, o = lax.scan(step, s0, (q, k, v, gc))
    return _from_chunks(o), s


def _hg_bidir(q, k, v, log_f, s_f, s_b):
    flip = lambda a: jnp.flip(a, axis=1)
    o_f, s_f = _hgrn2_chunked(q, k[:, :, 0], v, log_f[:, :, 0], s_f)
    o_b, s_b = _hgrn2_chunked(flip(q), flip(k[:, :, 1]), flip(v), flip(log_f[:, :, 1]), s_b)
    return o_f + flip(o_b), s_f, s_b


def _axial_angles(t):
    rows = t // GRID_W
    pos = jnp.arange(rows * GRID_W)
    row = (pos // GRID_W).astype(jnp.float32)
    col = (pos % GRID_W).astype(jnp.float32)
    n = DA_DH // 4
    inv = ROPE_THETA ** (-jnp.arange(n, dtype=jnp.float32) / n)
    return row[:, None] * inv, col[:, None] * inv


def _rope_half(x, ang):
    x1, x2 = jnp.split(x, 2, axis=-1)
    cos = jnp.cos(ang)[None, :, None, None, :].astype(x.dtype)
    sin = jnp.sin(ang)[None, :, None, None, :].astype(x.dtype)
    return jnp.concatenate([x1 * cos - x2 * sin, x2 * cos + x1 * sin], axis=-1)


def _axial_rope(x, ang_r, ang_c):
    half = DA_DH // 2
    return jnp.concatenate([_rope_half(x[..., :half], ang_r), _rope_half(x[..., half:], ang_c)], axis=-1)


def _da_prep(p):
    bsz, t, _ = p['da_q'].shape
    q = p['da_q'].reshape(bsz, t, DA_HEADS, 2, DA_DH)
    k = p['da_k'].reshape(bsz, t, DA_HEADS, 2, DA_DH)
    v = p['da_v'].reshape(bsz, t, DA_HEADS, DA_DV)
    return q, k, v


def _diff_block(qb, k, v, lam):
    s = jnp.einsum('bqhmd,bkhmd->bhmqk', qb, k).astype(jnp.float32) * DA_DH ** -0.5
    p = jax.nn.softmax(s, axis=-1)
    a = p[:, :, 0] - lam * p[:, :, 1]
    return jnp.einsum('bhqk,bkhd->bqhd', a.astype(v.dtype), v)


def _diff_latent(q, k_all, v_all, lam):
    bsz, t = q.shape[:2]
    qb = jnp.moveaxis(q.reshape((bsz, t // Q_BLOCK, Q_BLOCK) + q.shape[2:]), 1, 0)
    o = lax.map(lambda blk: _diff_block(blk, k_all, v_all, lam), qb)
    return jnp.moveaxis(o, 0, 1).reshape(bsz, t, DA_HEADS, DA_DV)


def _branches(p, o_dn, o_hg, o_da, fn_w, fn_b, dn_norm, hg_norm, da_norm, lam_init):
    dt = p['fn_u'].dtype
    y_fn = _fourier(p['fn_u'], fn_w, fn_b) * jax.nn.silu(p['fn_z'])
    y_dn = _head_rms(o_dn, dn_norm, dt) * jax.nn.silu(p['dn_z'])
    y_hg = _head_rms(o_hg, hg_norm, dt) * jax.nn.silu(p['hg_z'])
    y_da = _head_rms(o_da.astype(jnp.float32), da_norm, dt) * (1.0 - lam_init) * jax.nn.silu(p['da_z'])
    return (y_fn, y_dn, y_hg, y_da)


def _merge(gate_logits, ys, w_branch, w_out):
    bsz, t, _ = gate_logits.shape
    g = jax.nn.sigmoid(gate_logits).reshape(bsz, t, N_BRANCH, D_MODEL)
    y = sum(g[:, :, n] * (ys[n] @ w_branch[n]) for n in range(N_BRANCH))
    return y @ w_out


def setup_inputs(seed: int = 0) -> dict:
    key = jax.random.key(seed)
    ks = jax.random.split(key, 24)
    f32 = jnp.float32
    nrm = lambda k, shape, scale: jax.random.normal(k, shape, f32) * scale
    x = nrm(ks[0], (BATCH, SEQ, D_MODEL), 1.0)
    c = nrm(ks[1], (BATCH, D_MODEL), 1.0)
    ctx = nrm(ks[2], (BATCH, CTX_LEN, D_MODEL), 1.0)
    c_ctx = nrm(ks[3], (D_MODEL,), 1.0)
    norm_g = 1.0 + nrm(ks[4], (DEPTH, D_MODEL), 0.02)
    w_ada = nrm(ks[5], (DEPTH, D_MODEL, 3 * D_MODEL), 0.5 * D_MODEL ** -0.5)
    b_ada = nrm(ks[6], (DEPTH, 3 * D_MODEL), 0.02)
    w_in = nrm(ks[7], (DEPTH, D_MODEL, IN_W), D_MODEL ** -0.5)
    fn_w = nrm(ks[8], (DEPTH, BRANCH_W, BRANCH_W), BRANCH_W ** -0.5)
    fn_b = nrm(ks[9], (DEPTH, BRANCH_W), 0.02)
    dn_conv = nrm(ks[10], (DEPTH, DN_CONV, 3 * BRANCH_W), DN_CONV ** -0.5)
    dn_a_log = jnp.log(jax.random.uniform(ks[11], (DEPTH, 2, DN_HEADS), f32, 1.0, 16.0))
    dt0 = jnp.exp(jax.random.uniform(ks[12], (DEPTH, 2, DN_HEADS), f32, math.log(1e-3), math.log(1e-1)))
    dn_dt_bias = dt0 + jnp.log(-jnp.expm1(-dt0))
    dn_norm = 1.0 + nrm(ks[13], (DEPTH, DN_DV), 0.02)
    hg_lb_logits = nrm(ks[14], (2, DEPTH, HG_HEADS * HG_DK), 0.5)
    hg_norm = 1.0 + nrm(ks[15], (DEPTH, HG_DV), 0.02)
    da_lambda = nrm(ks[16], (DEPTH, 4, DA_DH), 0.1)
    da_norm = 1.0 + nrm(ks[17], (DEPTH, DA_DV), 0.02)
    w_branch = nrm(ks[18], (DEPTH, N_BRANCH, BRANCH_W, D_MODEL), BRANCH_W ** -0.5)
    w_out = nrm(ks[19], (DEPTH, D_MODEL, D_MODEL), D_MODEL ** -0.5)
    final_g = 1.0 + nrm(ks[20], (D_MODEL,), 0.02)
    return {'x': x, 'c': c, 'ctx': ctx, 'c_ctx': c_ctx, 'norm_g': norm_g, 'w_ada': w_ada,
            'b_ada': b_ada, 'w_in': w_in, 'fn_w': fn_w, 'fn_b': fn_b, 'dn_conv': dn_conv,
            'dn_a_log': dn_a_log, 'dn_dt_bias': dn_dt_bias, 'dn_norm': dn_norm,
            'hg_lb_logits': hg_lb_logits, 'hg_norm': hg_norm, 'da_lambda': da_lambda,
            'da_norm': da_norm, 'w_branch': w_branch, 'w_out': w_out, 'final_g': final_g}


def reference(x, c, ctx, c_ctx, norm_g, w_ada, b_ada, w_in, fn_w, fn_b, dn_conv, dn_a_log,
              dn_dt_bias, dn_norm, hg_lb_logits, hg_norm, da_lambda, da_norm, w_branch, w_out,
              final_g):
    bsz, t, _ = x.shape
    ang_r, ang_c = _axial_angles(t)
    lb_all = jnp.cumsum(jax.nn.softmax(hg_lb_logits.astype(jnp.float32), axis=1), axis=1)
    lb_all = lb_all - lb_all[:, :1]
    silu_c = jax.nn.silu(c)
    silu_cc = jax.nn.silu(c_ctx)
    xl, xc = x, ctx
    for l in range(DEPTH):
        last = l == DEPTH - 1
        mod_l = (silu_c @ w_ada[l] + b_ada[l])[:, None, :]
        mod_c = silu_cc @ w_ada[l] + b_ada[l]
        pl, gate_l = _stream_in(xl, mod_l, norm_g[l], w_in[l])
        pc, gate_c = _stream_in(xc, mod_c, norm_g[l], w_in[l])

        zs = jnp.zeros((bsz, DN_HEADS, DN_DK, DN_DV), jnp.float32)
        o_dn_c, s_f, s_b = _dn_bidir(*_dn_prep(pc, dn_conv[l], dn_a_log[l], dn_dt_bias[l]), zs, zs)
        o_dn_l, _, _ = _dn_bidir(*_dn_prep(pl, dn_conv[l], dn_a_log[l], dn_dt_bias[l]), s_f, s_b)

        zh = jnp.zeros((bsz, HG_HEADS, HG_DK, HG_DV), jnp.float32)
        o_hg_c, h_f, h_b = _hg_bidir(*_hg_prep(pc, lb_all[:, l]), zh, zh)
        o_hg_l, _, _ = _hg_bidir(*_hg_prep(pl, lb_all[:, l]), h_f, h_b)

        lam_init = 0.8 - 0.6 * math.exp(-0.3 * l)
        lp = da_lambda[l].astype(jnp.float32)
        lam = jnp.exp(jnp.sum(lp[0] * lp[1])) - jnp.exp(jnp.sum(lp[2] * lp[3])) + lam_init
        qc, kc, vc = _da_prep(pc)
        ql, kl, vl = _da_prep(pl)
        ql = _axial_rope(ql, ang_r, ang_c)
        kl = _axial_rope(kl, ang_r, ang_c)
        o_da_l = _diff_latent(ql, jnp.concatenate([kl, kc], axis=1),
                              jnp.concatenate([vl, vc], axis=1), lam)

        ys_l = _branches(pl, o_dn_l, o_hg_l, o_da_l, fn_w[l], fn_b[l], dn_norm[l], hg_norm[l],
                         da_norm[l], lam_init)
        new_xl = xl + gate_l * _merge(pl['gate'], ys_l, w_branch[l], w_out[l])
        if not last:
            o_da_c = _diff_block(qc, kc, vc, lam)
            ys_c = _branches(pc, o_dn_c, o_hg_c, o_da_c, fn_w[l], fn_b[l], dn_norm[l], hg_norm[l],
                             da_norm[l], lam_init)
            xc = xc + gate_c * _merge(pc['gate'], ys_c, w_branch[l], w_out[l])
        xl = new_xl
    return _rms(xl, final_g)
```

```python
import functools
import math

import jax
import jax.numpy as jnp
from jax import lax
from jax.experimental import pallas as pl
from jax.experimental.pallas import tpu as pltpu

F32 = jnp.float32
BF16 = jnp.bfloat16
HI = lax.Precision.HIGHEST

EPS = 1e-6
N_BRANCH = 4
HEADS = 4
HD = 128
BRANCH_W = HEADS * HD
FN_GW = 128
CHUNK = 64
SUB = 16
GRID_W = 64
ROPE_THETA = 10000.0
DA_DH = 64

COL = dict(fn_u=64, fn_z=68, dn_q=72, dn_k=76, dn_v=80, dn_z=84, hg_q=88, hg_f0=92, hg_f1=96,
           hg_i=100, hg_z=104, da_q=108, da_k=112, da_v=116, da_z=120, dn_ab=124)
PROJ_W = 126 * 128
GATE_W_OFF = 7696
AB_OFF = 3072

MIB = 1024 * 1024


def _cparams(sem, vmem_mib):
    return pltpu.CompilerParams(dimension_semantics=sem, vmem_limit_bytes=vmem_mib * MIB)


def _silu(x):
    return x * jax.nn.sigmoid(x)


def _dot(a, b, prec=None):
    return jnp.dot(a, b, preferred_element_type=F32, precision=prec)


def _dot_nt(a, b, prec=None):
    return lax.dot_general(a, b, (((1,), (1,)), ((), ())), preferred_element_type=F32, precision=prec)


def _dot_tn(a, b, prec=None):
    return lax.dot_general(a, b, (((0,), (0,)), ((), ())), preferred_element_type=F32, precision=prec)


def _pick(n, cands):
    for c in cands:
        if n % c == 0:
            return c
    return n


def _ada_kernel(c_ref, w_ref, b_ref, o_ref):
    sc = _silu(c_ref[...])
    o_ref[0] = _dot(sc.astype(BF16), w_ref[0].astype(BF16)) + b_ref[0]


def _ada(c8, w_ada, b_ada):
    depth, d, d3 = w_ada.shape
    tn = _pick(d3, (1536, 768, 512, 256, 128))
    return pl.pallas_call(
        _ada_kernel,
        grid=(depth, d3 // tn),
        in_specs=[pl.BlockSpec((8, d), lambda l, n: (0, 0)),
                  pl.BlockSpec((1, d, tn), lambda l, n: (l, 0, n)),
                  pl.BlockSpec((1, 1, tn), lambda l, n: (l, 0, n))],
        out_specs=pl.BlockSpec((1, 8, tn), lambda l, n: (l, 0, n)),
        out_shape=jax.ShapeDtypeStruct((depth, 8, d3), F32),
        compiler_params=_cparams(("parallel", "parallel"), 48),
        name="adaln_mod",
    )(c8, w_ada, b_ada.reshape(depth, 1, d3))


def _proj_kernel(x_ref, mod_ref, g_ref, w_ref, o_ref, h_ref):
    @pl.when(pl.program_id(1) == 0)
    def _():
        x = x_ref[...]
        y = x * lax.rsqrt(jnp.mean(x * x, axis=-1, keepdims=True) + EPS) * g_ref[...]
        h = y * (1.0 + mod_ref[0, 1:2, :]) + mod_ref[0, 0:1, :]
        h_ref[...] = h.astype(BF16)

    o_ref[...] = _dot(h_ref[...], w_ref[...])


def _proj(x2, mod, g, w, rows_per_seg):
    n, d = x2.shape
    tm = _pick(rows_per_seg, (1024, 512, 256, 128))
    tn = 768
    return pl.pallas_call(
        _proj_kernel,
        grid=(n // tm, PROJ_W // tn),
        in_specs=[pl.BlockSpec((tm, d), lambda i, j: (i, 0)),
                  pl.BlockSpec((1, 3, d), lambda i, j: (i * tm // rows_per_seg, 0, 0)),
                  pl.BlockSpec((1, d), lambda i, j: (0, 0)),
                  pl.BlockSpec((d, tn), lambda i, j: (0, j))],
        out_specs=pl.BlockSpec((tm, tn), lambda i, j: (i, j)),
        out_shape=jax.ShapeDtypeStruct((n, PROJ_W), F32),
        scratch_shapes=[pltpu.VMEM((tm, d), BF16)],
        compiler_params=_cparams(("parallel", "arbitrary"), 48),
        name="in_proj",
    )(x2, mod, g, w)


def _dft_cos_sin(n):
    j = jnp.arange(n, dtype=jnp.int32)
    sc = n ** -0.5
    if n <= 1024:
        ang = (2.0 * math.pi / n) * ((j[:, None] * j[None, :]) % n).astype(F32)
        return jnp.cos(ang) * sc, jnp.sin(ang) * sc
    m = n // 64
    k1 = jnp.arange(m, dtype=jnp.int32)
    k2 = jnp.arange(64, dtype=jnp.int32)
    a = (2.0 * math.pi / m) * ((j[:, None] * k1[None, :]) % m).astype(F32)
    b = (2.0 * math.pi / n) * ((j[:, None] * k2[None, :]) % n).astype(F32)
    ca, sa, cb, sb = jnp.cos(a), jnp.sin(a), jnp.cos(b), jnp.sin(b)
    c = ca[:, :, None] * cb[:, None, :] - sa[:, :, None] * sb[:, None, :]
    s = sa[:, :, None] * cb[:, None, :] + ca[:, :, None] * sb[:, None, :]
    return c.reshape(n, n) * sc, s.reshape(n, n) * sc


def _fn1_kernel(u_ref, cs_ref, o_ref):
    for g in range(BRANCH_W // FN_GW):
        r = _dot(u_ref[:, g * FN_GW:(g + 1) * FN_GW].astype(BF16), cs_ref[...])
        o_ref[:, g * FN_GW:(g + 1) * FN_GW] = r[:, :FN_GW].astype(BF16)
        o_ref[:, BRANCH_W + g * FN_GW:BRANCH_W + (g + 1) * FN_GW] = r[:, FN_GW:].astype(BF16)


def _fourier_channels(p, cs_ch):
    n = p.shape[0]
    tm = _pick(n, (1024, 512, 256, 128))
    return pl.pallas_call(
        _fn1_kernel,
        grid=(n // tm,),
        in_specs=[pl.BlockSpec((tm, BRANCH_W), lambda i: (i, COL["fn_u"] // 4)),
                  pl.BlockSpec((FN_GW, 2 * FN_GW), lambda i: (0, 0))],
        out_specs=pl.BlockSpec((tm, 2 * BRANCH_W), lambda i: (i, 0)),
        out_shape=jax.ShapeDtypeStruct((n, 2 * BRANCH_W), BF16),
        compiler_params=_cparams(("parallel",), 32),
        name="fourier_channels",
    )(p, cs_ch)


def _fn2_kernel(ct_ref, st_ref, r_ref, z_ref, w_ref, b_ref, y_ref):
    f = _dot(ct_ref[...], r_ref[:, :BRANCH_W]) + _dot(st_ref[...], r_ref[:, BRANCH_W:])
    y = _dot(f.astype(BF16), w_ref[...]) + b_ref[...]
    y_ref[...] = (y * _silu(z_ref[...])).astype(BF16)


def _fourier_positions(r, p, ct, nst, fn_w, fn_b, t):
    n = p.shape[0]
    bsz = n // t
    tm = _pick(t, (512, 256, 128))
    nt = t // tm
    return pl.pallas_call(
        _fn2_kernel,
        grid=(bsz, nt),
        in_specs=[pl.BlockSpec((tm, t), lambda b, i: (i, 0)),
                  pl.BlockSpec((tm, t), lambda b, i: (i, 0)),
                  pl.BlockSpec((t, 2 * BRANCH_W), lambda b, i: (b, 0)),
                  pl.BlockSpec((tm, BRANCH_W), lambda b, i: (b * nt + i, COL["fn_z"] // 4)),
                  pl.BlockSpec((BRANCH_W, BRANCH_W), lambda b, i: (0, 0)),
                  pl.BlockSpec((1, BRANCH_W), lambda b, i: (0, 0))],
        out_specs=pl.BlockSpec((tm, BRANCH_W), lambda b, i: (b * nt + i, 0)),
        out_shape=jax.ShapeDtypeStruct((n, BRANCH_W), BF16),
        compiler_params=_cparams(("parallel", "arbitrary"), 48),
        name="fourier_positions",
    )(ct, nst, r, p, fn_w, fn_b)


def _seq_spec(rows, index_map):
    return pl.BlockSpec((rows, HD), index_map, pipeline_mode=pl.Buffered(1))


def _chunk_masks():
    i = lax.broadcasted_iota(jnp.int32, (CHUNK, CHUNK), 0)
    j = lax.broadcasted_iota(jnp.int32, (CHUNK, CHUNK), 1)
    incl = (i >= j, i <= j)
    strict = (i > j, i < j)
    return incl, strict


def _softplus(x):
    return jnp.maximum(x, 0.0) + jnp.log1p(jnp.exp(-jnp.abs(x)))


def _head_rms_gate(o, g_ref, z_ref):
    y = o * lax.rsqrt(jnp.mean(o * o, axis=-1, keepdims=True) + EPS) * g_ref[...]
    return y * _silu(z_ref[...])


def _dn_kernel(alog_ref, dtb_ref, q_ref, k_ref, v_ref, z_ref, ab_ref, cq_ref, ck_ref, cv_ref,
               ng_ref, s0f_ref, s0b_ref, y_ref, sf_ref, sb_ref, qs, ks, vs, gs, bs, o_s):
    h = pl.program_id(1)
    t = q_ref.shape[0]
    nchunk = t // CHUNK
    row = lax.broadcasted_iota(jnp.int32, (t, HD), 0)

    def conv(x_ref, c_ref):
        x = x_ref[...]
        xm = jnp.where(row == 0, 0.0, pltpu.roll(x, 1, 0))
        xp = jnp.where(row == t - 1, 0.0, pltpu.roll(x, t - 1, 0))
        y = xm * c_ref[0:1, :] + x * c_ref[1:2, :] + xp * c_ref[2:3, :]
        return _silu(y)

    q = conv(q_ref, cq_ref)
    qs[...] = q * lax.rsqrt(jnp.sum(q * q, axis=-1, keepdims=True) + EPS) * HD ** -0.5
    k = conv(k_ref, ck_ref)
    ks[...] = k * lax.rsqrt(jnp.sum(k * k, axis=-1, keepdims=True) + EPS)
    vs[...] = conv(v_ref, cv_ref)

    sel_row = lax.broadcasted_iota(jnp.int32, (HD, HD), 0)
    ab = ab_ref[...]
    for d in range(2):
        a_b = _dot(ab, (sel_row == d * HEADS + h).astype(F32), HI)
        b_b = _dot(ab, (sel_row == 2 * HEADS + d * HEADS + h).astype(F32), HI)
        neg_a = -jnp.exp(jnp.full((1, HD), alog_ref[d, h], F32))
        gs[d] = neg_a * _softplus(a_b + dtb_ref[d, h])
        bs[d] = jax.nn.sigmoid(b_b)

    incl, strict = _chunk_masks()
    ones = jnp.ones((CHUNK, CHUNK), F32)
    eye = (lax.broadcasted_iota(jnp.int32, (CHUNK, CHUNK), 0)
           == lax.broadcasted_iota(jnp.int32, (CHUNK, CHUNK), 1)).astype(F32)

    def chunk(d, c0, s):
        qc = qs[pl.ds(c0, CHUNK), :]
        kc = ks[pl.ds(c0, CHUNK), :]
        vc = vs[pl.ds(c0, CHUNK), :]
        gb = gs[d, pl.ds(c0, CHUNK), :]
        bt = bs[d, pl.ds(c0, CHUNK), :]
        gc = _dot(incl[d].astype(F32), gb, HI)
        gr = _dot(ones, jnp.where(incl[1 - d], gb[:, :CHUNK], 0.0), HI)
        decay = jnp.where(incl[d], jnp.exp(jnp.where(incl[d], gc[:, :CHUNK] - gr, 0.0)), 0.0)
        eg = jnp.exp(gc)
        kb = kc * bt
        kcb = kc.astype(BF16)
        a_kk = jnp.where(strict[d], _dot_nt(kb.astype(BF16), kcb) * decay, 0.0)
        xp = -a_kk
        t_inv = eye + xp
        for _ in range(5):
            xp = _dot(xp, xp, HI)
            t_inv = t_inv + _dot(t_inv, xp, HI)
        t_inv_b = t_inv.astype(BF16)
        u = _dot(t_inv_b, (vc * bt).astype(BF16))
        w = _dot(t_inv_b, (kb * eg).astype(BF16))
        a_qk = jnp.where(incl[d], _dot_nt(qc.astype(BF16), kcb) * decay, 0.0)
        g_last = gc[CHUNK - 1:CHUNK, :] if d == 0 else gc[0:1, :]
        k_dec = kc * jnp.exp(g_last - gc)
        s_b16 = s.astype(BF16)
        v_new = u - _dot(w.astype(BF16), s_b16)
        o = _dot((qc * eg).astype(BF16), s_b16) + _dot(a_qk.astype(BF16), v_new.astype(BF16))
        s = s * jnp.exp(g_last) + _dot_tn(k_dec.astype(BF16), v_new.astype(BF16))
        return o, s

    def body(n, carry):
        s_f, s_b = carry
        cf = pl.multiple_of(n * CHUNK, CHUNK)
        cb = pl.multiple_of((nchunk - 1 - n) * CHUNK, CHUNK)
        o_f, s_f = chunk(0, cf, s_f)
        o_s[0, pl.ds(cf, CHUNK), :] = o_f
        o_b, s_b = chunk(1, cb, s_b)
        o_s[1, pl.ds(cb, CHUNK), :] = o_b
        return s_f, s_b

    s_f, s_b = lax.fori_loop(0, nchunk, body, (s0f_ref[0, 0], s0b_ref[0, 0]))
    sf_ref[0, 0] = s_f
    sb_ref[0, 0] = s_b
    y_ref[...] = _head_rms_gate(o_s[0] + o_s[1], ng_ref, z_ref).astype(BF16)


def _deltanet(p, conv_w, a_log, dt_bias, norm_g, s0f, s0b, t):
    n = p.shape[0]
    bsz = n // t
    col = lambda name: (lambda b, h: (b, COL[name] + h))
    state = pl.BlockSpec((1, 1, HD, HD), lambda b, h: (b, h, 0, 0))
    smem = pl.BlockSpec(memory_space=pltpu.SMEM)
    conv = lambda off: pl.BlockSpec((3, HD), lambda b, h: (0, off + h))
    st_shape = jax.ShapeDtypeStruct((bsz, HEADS, HD, HD), F32)
    return pl.pallas_call(
        _dn_kernel,
        grid=(bsz, HEADS),
        in_specs=[smem, smem,
                  _seq_spec(t, col("dn_q")), _seq_spec(t, col("dn_k")),
                  _seq_spec(t, col("dn_v")), _seq_spec(t, col("dn_z")),
                  _seq_spec(t, lambda b, h: (b, COL["dn_ab"])),
                  conv(0), conv(HEADS), conv(2 * HEADS),
                  pl.BlockSpec((1, HD), lambda b, h: (0, 0)), state, state],
        out_specs=[pl.BlockSpec((t, HD), lambda b, h: (b, h)), state, state],
        out_shape=[jax.ShapeDtypeStruct((n, BRANCH_W), BF16), st_shape, st_shape],
        scratch_shapes=[pltpu.VMEM((t, HD), F32), pltpu.VMEM((t, HD), F32), pltpu.VMEM((t, HD), F32),
                        pltpu.VMEM((2, t, HD), F32), pltpu.VMEM((2, t, HD), F32),
                        pltpu.VMEM((2, t, HD), F32)],
        compiler_params=_cparams(("parallel", "parallel"), 48),
        name="deltanet",
    )(a_log, dt_bias, p, p, p, p, p, conv_w, conv_w, conv_w, norm_g, s0f, s0b)


def _hg_kernel(q_ref, f0_ref, f1_ref, i_ref, z_ref, llb_ref, l1m_ref, oml_ref, ng_ref, s0f_ref,
               s0b_ref, y_ref, sf_ref, sb_ref, qs, lfs, ks, o_s):
    t = q_ref.shape[0]
    nchunk = t // CHUNK
    nsub = CHUNK // SUB
    qs[...] = _silu(q_ref[...])
    for d, f_ref in enumerate((f0_ref, f1_ref)):
        f = f_ref[...]
        lsig = jnp.minimum(f, 0.0) - jnp.log1p(jnp.exp(-jnp.abs(f)))
        a = llb_ref[d:d + 1, :]
        b = l1m_ref[d:d + 1, :] + lsig
        lfs[d] = jnp.maximum(a, b) + jnp.log1p(jnp.exp(-jnp.abs(a - b)))
        ks[d] = oml_ref[d:d + 1, :] * jax.nn.sigmoid(-f)

    incl, _ = _chunk_masks()
    sub_i = lax.broadcasted_iota(jnp.int32, (SUB, HD), 0)

    def chunk(d, c0, s_t):
        qc = qs[pl.ds(c0, CHUNK), :]
        kc = ks[d, pl.ds(c0, CHUNK), :]
        vc = i_ref[pl.ds(c0, CHUNK), :]
        gc = _dot(incl[d].astype(F32), lfs[d, pl.ds(c0, CHUNK), :], HI)
        g_last = gc[CHUNK - 1:CHUNK, :] if d == 0 else gc[0:1, :]
        o_inter = _dot_nt((qc * jnp.exp(gc)).astype(BF16), s_t.astype(BF16))
        k_dec = kc * jnp.exp(g_last - gc)
        s_t = s_t * jnp.exp(g_last) + _dot_tn(vc.astype(BF16), k_dec.astype(BF16))
        acc = [jnp.zeros((SUB, HD), F32) for _ in range(nsub)]
        blk = lambda a, m: a[m * SUB:(m + 1) * SUB, :]
        for jb in range(nsub):
            ibs = range(jb, nsub) if d == 0 else range(0, jb + 1)
            for jj in range(SUB):
                j = jb * SUB + jj
                kj, gj, vj = kc[j:j + 1, :], gc[j:j + 1, :], vc[j:j + 1, :]
                for ib in ibs:
                    term = blk(qc, ib) * kj * jnp.exp(jnp.minimum(blk(gc, ib) - gj, 0.0))
                    if ib == jb:
                        keep = (sub_i >= jj) if d == 0 else (sub_i <= jj)
                        term = jnp.where(keep, term, 0.0)
                    acc[ib] = acc[ib] + jnp.sum(term, axis=-1, keepdims=True) * vj
        return o_inter + jnp.concatenate(acc, axis=0), s_t

    def body(n, carry):
        s_f, s_b = carry
        cf = pl.multiple_of(n * CHUNK, CHUNK)
        cb = pl.multiple_of((nchunk - 1 - n) * CHUNK, CHUNK)
        o_f, s_f = chunk(0, cf, s_f)
        o_s[0, pl.ds(cf, CHUNK), :] = o_f
        o_b, s_b = chunk(1, cb, s_b)
        o_s[1, pl.ds(cb, CHUNK), :] = o_b
        return s_f, s_b

    s_f, s_b = lax.fori_loop(0, nchunk, body, (s0f_ref[0, 0], s0b_ref[0, 0]))
    sf_ref[0, 0] = s_f
    sb_ref[0, 0] = s_b
    y_ref[...] = _head_rms_gate(o_s[0] + o_s[1], ng_ref, z_ref).astype(BF16)


def _hgrn2(p, log_lb, log_1m_lb, one_m_lb, norm_g, s0f, s0b, t):
    n = p.shape[0]
    bsz = n // t
    col = lambda name: (lambda b, h: (b, COL[name] + h))
    state = pl.BlockSpec((1, 1, HD, HD), lambda b, h: (b, h, 0, 0))
    lbs = pl.BlockSpec((2, HD), lambda b, h: (0, h))
    st_shape = jax.ShapeDtypeStruct((bsz, HEADS, HD, HD), F32)
    return pl.pallas_call(
        _hg_kernel,
        grid=(bsz, HEADS),
        in_specs=[_seq_spec(t, col("hg_q")), _seq_spec(t, col("hg_f0")),
                  _seq_spec(t, col("hg_f1")), _seq_spec(t, col("hg_i")),
                  _seq_spec(t, col("hg_z")), lbs, lbs, lbs,
                  pl.BlockSpec((1, HD), lambda b, h: (0, 0)), state, state],
        out_specs=[pl.BlockSpec((t, HD), lambda b, h: (b, h)), state, state],
        out_shape=[jax.ShapeDtypeStruct((n, BRANCH_W), BF16), st_shape, st_shape],
        scratch_shapes=[pltpu.VMEM((t, HD), F32), pltpu.VMEM((2, t, HD), F32),
                        pltpu.VMEM((2, t, HD), F32), pltpu.VMEM((2, t, HD), F32)],
        compiler_params=_cparams(("parallel", "parallel"), 48),
        name="hgrn2",
    )(p, p, p, p, p, log_lb, log_1m_lb, one_m_lb, norm_g, s0f, s0b)


def _rope_tables(t):
    pos = jnp.arange(t)
    row = (pos // GRID_W).astype(F32)
    col = (pos % GRID_W).astype(F32)
    n = DA_DH // 4
    inv = ROPE_THETA ** (-jnp.arange(n, dtype=F32) / n)
    ar, ac = row[:, None] * inv, col[:, None] * inv
    zero = jnp.zeros_like(ar)
    cos = jnp.concatenate([jnp.cos(ar), jnp.cos(ar), jnp.cos(ac), jnp.cos(ac)], axis=-1)
    s_up = jnp.concatenate([zero, jnp.sin(ar), zero, jnp.sin(ac)], axis=-1)
    s_dn = jnp.concatenate([-jnp.sin(ar), zero, -jnp.sin(ac), zero], axis=-1)
    tile = lambda a: jnp.concatenate([a, a], axis=-1)
    return tile(cos), tile(s_up), tile(s_dn)


def _rope(x, cos, s_up, s_dn):
    half = DA_DH // 4
    return x * cos + pltpu.roll(x, half, 1) * s_up + pltpu.roll(x, HD - half, 1) * s_dn


def _da_kernel(lam_ref, q_ref, kl_ref, kc_ref, vl_ref, vc_ref, z_ref, cq_ref, uq_ref, dq_ref,
               ck_ref, uk_ref, dk_ref, ng_ref, y_ref, k_s, v_s, *, t_lat, out_scale):
    @pl.when(pl.program_id(2) == 0)
    def _():
        if t_lat:
            k_s[0:t_lat, :] = _rope(kl_ref[...], ck_ref[...], uk_ref[...], dk_ref[...]).astype(BF16)
            v_s[0:t_lat, :] = vl_ref[...].astype(BF16)
        k_s[t_lat:, :] = kc_ref[...].astype(BF16)
        v_s[t_lat:, :] = vc_ref[...].astype(BF16)

    q = q_ref[...]
    if t_lat:
        q = _rope(q, cq_ref[...], uq_ref[...], dq_ref[...])
    q = q * DA_DH ** -0.5
    lane = lax.broadcasted_iota(jnp.int32, q.shape, 1)
    k_all = k_s[...]

    def softmax_map(qm):
        s = _dot_nt(qm.astype(BF16), k_all)
        e = jnp.exp(s - jnp.max(s, axis=-1, keepdims=True))
        return e * (1.0 / jnp.sum(e, axis=-1, keepdims=True))

    p0 = softmax_map(jnp.where(lane < DA_DH, q, 0.0))
    p1 = softmax_map(jnp.where(lane >= DA_DH, q, 0.0))
    a = p0 - lam_ref[0] * p1
    o = _dot(a.astype(BF16), v_s[...])
    y = o * lax.rsqrt(jnp.mean(o * o, axis=-1, keepdims=True) + EPS) * ng_ref[...]
    y_ref[...] = (y * out_scale * _silu(z_ref[...])).astype(BF16)


def _diff_attention(lam, p_q, p_lat, p_ctx, rope, norm_g, t, t_ctx, out_scale, latent):
    n = p_q.shape[0]
    bsz = n // t
    tq = _pick(t, (256, 128))
    nq = t // tq
    t_lat = t if latent else 0
    t_kl = t if latent else t_ctx
    cos, s_up, s_dn = rope
    col = lambda name: (lambda b, h, i: (b, COL[name] + h))
    qcol = lambda name: (lambda b, h, i: (b * nq + i, COL[name] + h))
    tab_q = pl.BlockSpec((tq, HD), lambda b, h, i: (i, 0))
    tab_k = _seq_spec(t_kl, lambda b, h, i: (0, 0))
    return pl.pallas_call(
        functools.partial(_da_kernel, t_lat=t_lat, out_scale=out_scale),
        grid=(bsz, HEADS, nq),
        in_specs=[pl.BlockSpec(memory_space=pltpu.SMEM),
                  pl.BlockSpec((tq, HD), qcol("da_q")),
                  _seq_spec(t_kl, col("da_k")), _seq_spec(t_ctx, col("da_k")),
                  _seq_spec(t_kl, col("da_v")), _seq_spec(t_ctx, col("da_v")),
                  pl.BlockSpec((tq, HD), qcol("da_z")),
                  tab_q, tab_q, tab_q, tab_k, tab_k, tab_k,
                  pl.BlockSpec((1, HD), lambda b, h, i: (0, 0))],
        out_specs=pl.BlockSpec((tq, HD), lambda b, h, i: (b * nq + i, h)),
        out_shape=jax.ShapeDtypeStruct((n, BRANCH_W), BF16),
        scratch_shapes=[pltpu.VMEM((t_lat + t_ctx, HD), BF16), pltpu.VMEM((t_lat + t_ctx, HD), BF16)],
        compiler_params=_cparams(("parallel", "parallel", "arbitrary"), 56),
        name="diff_attention",
    )(lam, p_q, p_lat, p_ctx, p_lat, p_ctx, p_q, cos, s_up, s_dn, cos, s_up, s_dn, norm_g)


def _merge_kernel(y0_ref, y1_ref, y2_ref, y3_ref, gl_ref, wb_ref, wo_ref, x_ref, mod_ref, fg_ref,
                  o_ref, acc_ref, *, final):
    n = pl.program_id(1)
    for k, y_ref in enumerate((y0_ref, y1_ref, y2_ref, y3_ref)):
        @pl.when(n == k)
        def _(k=k, y_ref=y_ref):
            c = jax.nn.sigmoid(gl_ref[...]) * _dot(y_ref[...], wb_ref[0])
            if k == 0:
                acc_ref[...] = c
            else:
                acc_ref[...] += c

    @pl.when(n == N_BRANCH - 1)
    def _():
        x = x_ref[...] + mod_ref[0, 2:3, :] * _dot(acc_ref[...].astype(BF16), wo_ref[...])
        if final:
            x = x * lax.rsqrt(jnp.mean(x * x, axis=-1, keepdims=True) + EPS) * fg_ref[...]
        o_ref[...] = x


def _merge(ys, p, w_branch, w_out, x2, mod, final_g, rows_per_seg, final):
    n, d = x2.shape
    tm = _pick(rows_per_seg, (256, 128))
    ysp = pl.BlockSpec((tm, BRANCH_W), lambda i, k: (i, 0))
    return pl.pallas_call(
        functools.partial(_merge_kernel, final=final),
        grid=(n // tm, N_BRANCH),
        in_specs=[ysp, ysp, ysp, ysp,
                  pl.BlockSpec((tm, d), lambda i, k: (i, k)),
                  pl.BlockSpec((1, BRANCH_W, d), lambda i, k: (k, 0, 0)),
                  pl.BlockSpec((d, d), lambda i, k: (0, 0)),
                  pl.BlockSpec((tm, d), lambda i, k: (i, 0)),
                  pl.BlockSpec((1, 3, d), lambda i, k: (i * tm // rows_per_seg, 0, 0)),
                  pl.BlockSpec((1, d), lambda i, k: (0, 0))],
        out_specs=pl.BlockSpec((tm, d), lambda i, k: (i, 0)),
        out_shape=jax.ShapeDtypeStruct((n, d), F32),
        scratch_shapes=[pltpu.VMEM((tm, d), F32)],
        compiler_params=_cparams(("parallel", "arbitrary"), 56),
        name="merge",
    )(*ys, p, w_branch, w_out, x2, mod, final_g)


def kernel(x, c, ctx, c_ctx, norm_g, w_ada, b_ada, w_in, fn_w, fn_b, dn_conv, dn_a_log, dn_dt_bias,
           dn_norm, hg_lb_logits, hg_norm, da_lambda, da_norm, w_branch, w_out, final_g):
    bsz, t, d = x.shape
    t_ctx = ctx.shape[1]
    depth = w_in.shape[0]
    assert d == N_BRANCH * BRANCH_W and t % CHUNK == 0 and t_ctx % CHUNK == 0 and bsz + 1 <= 8

    w_in_r = jnp.concatenate(
        [w_in[:, :, GATE_W_OFF:], w_in[:, :, :AB_OFF], w_in[:, :, AB_OFF + 4 * HEADS:GATE_W_OFF],
         w_in[:, :, AB_OFF:AB_OFF + 4 * HEADS],
         jnp.zeros((depth, d, 2 * HD - 4 * HEADS), w_in.dtype)], axis=-1).astype(BF16)
    wb16, wo16, fnw16 = w_branch.astype(BF16), w_out.astype(BF16), fn_w.astype(BF16)

    c8 = jnp.concatenate([c, c_ctx[None, :], jnp.zeros((8 - bsz - 1, d), F32)], axis=0)
    mod = _ada(c8, w_ada, b_ada).reshape(depth, 8, 3, d)

    lb_all = jnp.cumsum(jax.nn.softmax(hg_lb_logits.astype(F32), axis=1), axis=1)
    lb_all = lb_all - lb_all[:, :1]
    log_lb, log_1m_lb, one_m_lb = jnp.log(lb_all), jnp.log1p(-lb_all), 1.0 - lb_all

    c_ch, s_ch = _dft_cos_sin(FN_GW)
    cs_ch = jnp.concatenate([c_ch, s_ch], axis=-1).astype(BF16)
    dft = {}
    for tt in (t, t_ctx):
        ct, st = _dft_cos_sin(tt)
        dft[tt] = (ct.astype(BF16), (-st).astype(BF16))
    rope_l = _rope_tables(t)
    rope_c = tuple(a[:t_ctx] for a in rope_l)

    xl = x.reshape(bsz * t, d)
    xc = ctx.reshape(bsz * t_ctx, d)
    zstate = jnp.zeros((bsz, HEADS, HD, HD), F32)
    out = None
    for l in range(depth):
        last = l == depth - 1
        g_l = norm_g[l][None, :]
        mod_l, mod_c = mod[l, :bsz], mod[l, bsz:bsz + 1]
        pl_ = _proj(xl, mod_l, g_l, w_in_r[l], t)
        pc_ = _proj(xc, mod_c, g_l, w_in_r[l], bsz * t_ctx)

        fn_b_l = fn_b[l][None, :]
        y_fn_l = _fourier_positions(_fourier_channels(pl_, cs_ch), pl_, *dft[t], fnw16[l], fn_b_l, t)

        dn_n, hg_n, da_n = dn_norm[l][None, :], hg_norm[l][None, :], da_norm[l][None, :]
        y_dn_c, s_f, s_b = _deltanet(pc_, dn_conv[l], dn_a_log[l], dn_dt_bias[l], dn_n,
                                     zstate, zstate, t_ctx)
        y_dn_l, _, _ = _deltanet(pl_, dn_conv[l], dn_a_log[l], dn_dt_bias[l], dn_n, s_f, s_b, t)

        lbs = (log_lb[:, l], log_1m_lb[:, l], one_m_lb[:, l])
        y_hg_c, h_f, h_b = _hgrn2(pc_, *lbs, hg_n, zstate, zstate, t_ctx)
        y_hg_l, _, _ = _hgrn2(pl_, *lbs, hg_n, h_f, h_b, t)

        lam_init = 0.8 - 0.6 * math.exp(-0.3 * l)
        lp = da_lambda[l].astype(F32)
        lam = (jnp.exp(jnp.sum(lp[0] * lp[1])) - jnp.exp(jnp.sum(lp[2] * lp[3])) + lam_init).reshape(1)
        y_da_l = _diff_attention(lam, pl_, pl_, pc_, rope_l, da_n, t, t_ctx, 1.0 - lam_init, True)

        fg = final_g[None, :]
        new_xl = _merge((y_fn_l, y_dn_l, y_hg_l, y_da_l), pl_, wb16[l], wo16[l], xl, mod_l, fg, t, last)
        if not last:
            y_fn_c = _fourier_positions(_fourier_channels(pc_, cs_ch), pc_, *dft[t_ctx], fnw16[l],
                                        fn_b_l, t_ctx)
            y_da_c = _diff_attention(lam, pc_, pc_, pc_, rope_c, da_n, t_ctx, t_ctx,
                                     1.0 - lam_init, False)
            xc = _merge((y_fn_c, y_dn_c, y_hg_c, y_da_c), pc_, wb16[l], wo16[l], xc, mod_c, fg,
                        bsz * t_ctx, False)
        xl = new_xl
    return xl.reshape(bsz, t, d)
```

```python
import functools
import math

import jax
import jax.numpy as jnp
from jax import lax
from jax.experimental import pallas as pl
from jax.experimental.pallas import tpu as pltpu

F32 = jnp.float32
BF16 = jnp.bfloat16

EPS = 1e-6
N_BRANCH = 4
HEADS = 4
HD = 128
BRANCH_W = HEADS * HD
FN_GW = 128
CHUNK = 64
SUB = 16
GRID_W = 64
ROPE_THETA = 10000.0
DA_DH = 64
LOG2E = 1.4426950408889634

COL = dict(fn_u=64, fn_z=68, dn_q=72, dn_k=76, dn_v=80, dn_z=84, hg_q=88, hg_f0=92, hg_f1=96,
           hg_i=100, hg_z=104, da_q=108, da_k=112, da_v=116, da_z=120, dn_ab=124)
PROJ_W = 126 * 128
GATE_W_OFF = 7696
AB_OFF = 3072

MIB = 1024 * 1024


def _cparams(sem, vmem_mib):
    return pltpu.CompilerParams(dimension_semantics=sem, vmem_limit_bytes=vmem_mib * MIB)


def _silu(x):
    return x * jax.nn.sigmoid(x)


def _dot(a, b):
    return jnp.dot(a, b, preferred_element_type=F32)


def _dot_nt(a, b):
    return lax.dot_general(a, b, (((1,), (1,)), ((), ())), preferred_element_type=F32)


def _dot_tn(a, b):
    return lax.dot_general(a, b, (((0,), (0,)), ((), ())), preferred_element_type=F32)


def _split2(x):
    hi = x.astype(BF16)
    return hi, (x - hi.astype(F32)).astype(BF16)


def _split3(x):
    hi = x.astype(BF16)
    r = x - hi.astype(F32)
    mid = r.astype(BF16)
    return hi, mid, (r - mid.astype(F32)).astype(BF16)


def _dot3(a, b):
    a_hi, a_lo = _split2(a)
    b_hi, b_lo = _split2(b)
    return _dot(a_hi, b_hi) + (_dot(a_hi, b_lo) + _dot(a_lo, b_hi))


def _mask_dot(mask, x):
    hi, mid, lo = _split3(x)
    return _dot(mask, hi) + (_dot(mask, mid) + _dot(mask, lo))


def _dot_mask(x, mask):
    hi, mid, lo = _split3(x)
    return _dot(hi, mask) + (_dot(mid, mask) + _dot(lo, mask))


def _pick(n, cands):
    for c in cands:
        if n % c == 0:
            return c
    return n


def _ada_kernel(c_ref, w_ref, b_ref, o_ref):
    sc = _silu(c_ref[...])
    o_ref[0] = _dot(sc.astype(BF16), w_ref[0].astype(BF16)) + b_ref[0]


def _ada(c8, w_ada, b_ada):
    depth, d, d3 = w_ada.shape
    tn = _pick(d3, (1536, 768, 512, 256, 128))
    return pl.pallas_call(
        _ada_kernel,
        grid=(depth, d3 // tn),
        in_specs=[pl.BlockSpec((8, d), lambda l, n: (0, 0)),
                  pl.BlockSpec((1, d, tn), lambda l, n: (l, 0, n)),
                  pl.BlockSpec((1, 1, tn), lambda l, n: (l, 0, n))],
        out_specs=pl.BlockSpec((1, 8, tn), lambda l, n: (l, 0, n)),
        out_shape=jax.ShapeDtypeStruct((depth, 8, d3), F32),
        compiler_params=_cparams(("parallel", "parallel"), 48),
        name="adaln_mod",
    )(c8, w_ada, b_ada.reshape(depth, 1, d3))


def _proj_kernel(x_ref, mod_ref, g_ref, w_ref, o_ref, h_ref):
    @pl.when(pl.program_id(1) == 0)
    def _():
        x = x_ref[...]
        y = x * lax.rsqrt(jnp.mean(x * x, axis=-1, keepdims=True) + EPS) * g_ref[...]
        h = y * (1.0 + mod_ref[0, 1:2, :]) + mod_ref[0, 0:1, :]
        h_ref[...] = h.astype(BF16)

    o_ref[...] = _dot(h_ref[...], w_ref[...])


def _proj(x2, mod, g, w_all, layer, rows_per_seg):
    n, d = x2.shape
    tm = _pick(rows_per_seg, (1024, 512, 256, 128))
    tn = 768
    return pl.pallas_call(
        _proj_kernel,
        grid=(n // tm, PROJ_W // tn),
        in_specs=[pl.BlockSpec((tm, d), lambda i, j: (i, 0)),
                  pl.BlockSpec((1, 3, d), lambda i, j: (i * tm // rows_per_seg, 0, 0)),
                  pl.BlockSpec((1, d), lambda i, j: (0, 0)),
                  pl.BlockSpec((None, d, tn), lambda i, j: (layer, 0, j))],
        out_specs=pl.BlockSpec((tm, tn), lambda i, j: (i, j)),
        out_shape=jax.ShapeDtypeStruct((n, PROJ_W), F32),
        scratch_shapes=[pltpu.VMEM((tm, d), BF16)],
        compiler_params=_cparams(("parallel", "arbitrary"), 48),
        name="in_proj",
    )(x2, mod, g, w_all)


def _dft_cos_sin(n):
    j = jnp.arange(n, dtype=jnp.int32)
    sc = n ** -0.5
    if n <= 1024:
        ang = (2.0 * math.pi / n) * ((j[:, None] * j[None, :]) % n).astype(F32)
        return jnp.cos(ang) * sc, jnp.sin(ang) * sc
    m = n // 64
    k1 = jnp.arange(m, dtype=jnp.int32)
    k2 = jnp.arange(64, dtype=jnp.int32)
    a = (2.0 * math.pi / m) * ((j[:, None] * k1[None, :]) % m).astype(F32)
    b = (2.0 * math.pi / n) * ((j[:, None] * k2[None, :]) % n).astype(F32)
    ca, sa, cb, sb = jnp.cos(a), jnp.sin(a), jnp.cos(b), jnp.sin(b)
    c = ca[:, :, None] * cb[:, None, :] - sa[:, :, None] * sb[:, None, :]
    s = sa[:, :, None] * cb[:, None, :] + ca[:, :, None] * sb[:, None, :]
    return c.reshape(n, n) * sc, s.reshape(n, n) * sc


def _fn1_kernel(u_ref, cs_ref, o_ref):
    for g in range(BRANCH_W // FN_GW):
        r = _dot(u_ref[:, g * FN_GW:(g + 1) * FN_GW].astype(BF16), cs_ref[...])
        o_ref[:, g * FN_GW:(g + 1) * FN_GW] = r[:, :FN_GW].astype(BF16)
        o_ref[:, BRANCH_W + g * FN_GW:BRANCH_W + (g + 1) * FN_GW] = r[:, FN_GW:].astype(BF16)


def _fourier_channels(p, cs_ch):
    n = p.shape[0]
    tm = _pick(n, (1024, 512, 256, 128))
    return pl.pallas_call(
        _fn1_kernel,
        grid=(n // tm,),
        in_specs=[pl.BlockSpec((tm, BRANCH_W), lambda i: (i, COL["fn_u"] // 4)),
                  pl.BlockSpec((FN_GW, 2 * FN_GW), lambda i: (0, 0))],
        out_specs=pl.BlockSpec((tm, 2 * BRANCH_W), lambda i: (i, 0)),
        out_shape=jax.ShapeDtypeStruct((n, 2 * BRANCH_W), BF16),
        compiler_params=_cparams(("parallel",), 32),
        name="fourier_channels",
    )(p, cs_ch)


def _fn2_kernel(ct_ref, st_ref, r_ref, z_ref, w_ref, b_ref, y_ref):
    f = _dot(ct_ref[...], r_ref[:, :BRANCH_W]) + _dot(st_ref[...], r_ref[:, BRANCH_W:])
    y = _dot(f.astype(BF16), w_ref[...]) + b_ref[...]
    y_ref[...] = (y * _silu(z_ref[...])).astype(BF16)


def _fourier_positions(r, p, ct, nst, fn_w_all, fn_b, layer, t):
    n = p.shape[0]
    bsz = n // t
    tm = _pick(t, (512, 256, 128))
    nt = t // tm
    return pl.pallas_call(
        _fn2_kernel,
        grid=(bsz, nt),
        in_specs=[pl.BlockSpec((tm, t), lambda b, i: (i, 0)),
                  pl.BlockSpec((tm, t), lambda b, i: (i, 0)),
                  pl.BlockSpec((t, 2 * BRANCH_W), lambda b, i: (b, 0)),
                  pl.BlockSpec((tm, BRANCH_W), lambda b, i: (b * nt + i, COL["fn_z"] // 4)),
                  pl.BlockSpec((None, BRANCH_W, BRANCH_W), lambda b, i: (layer, 0, 0)),
                  pl.BlockSpec((1, BRANCH_W), lambda b, i: (0, 0))],
        out_specs=pl.BlockSpec((tm, BRANCH_W), lambda b, i: (b * nt + i, 0)),
        out_shape=jax.ShapeDtypeStruct((n, BRANCH_W), BF16),
        compiler_params=_cparams(("parallel", "arbitrary"), 48),
        name="fourier_positions",
    )(ct, nst, r, p, fn_w_all, fn_b)


def _seq_spec(rows, index_map):
    return pl.BlockSpec((rows, HD), index_map, pipeline_mode=pl.Buffered(1))


def _chunk_masks():
    i = lax.broadcasted_iota(jnp.int32, (CHUNK, CHUNK), 0)
    j = lax.broadcasted_iota(jnp.int32, (CHUNK, CHUNK), 1)
    incl = (i >= j, i <= j)
    strict = (i > j, i < j)
    return incl, strict


def _as_bf16_mask(m):
    return jnp.where(m, 1.0, 0.0).astype(BF16)


def _softplus(x):
    return jnp.maximum(x, 0.0) + jnp.log1p(jnp.exp(-jnp.abs(x)))


def _head_rms_gate(o, g_ref, z_ref):
    y = o * lax.rsqrt(jnp.mean(o * o, axis=-1, keepdims=True) + EPS) * g_ref[...]
    return y * _silu(z_ref[...])


def _dn_kernel(alog_ref, dtb_ref, q_ref, k_ref, v_ref, z_ref, ab_ref, cq_ref, ck_ref, cv_ref,
               ng_ref, s0f_ref, s0b_ref, y_ref, sf_ref, sb_ref,
               qs, ks, vs, m_s, n_s, qp_s, gl_s, o_s, *, group):
    h = pl.program_id(1)
    t = q_ref.shape[0]
    nchunk = t // CHUNK
    row = lax.broadcasted_iota(jnp.int32, (t, HD), 0)

    def conv(x_ref, c_ref):
        x = x_ref[...]
        xm = jnp.where(row == 0, 0.0, pltpu.roll(x, 1, 0))
        xp = jnp.where(row == t - 1, 0.0, pltpu.roll(x, t - 1, 0))
        y = xm * c_ref[0:1, :] + x * c_ref[1:2, :] + xp * c_ref[2:3, :]
        return _silu(y)

    q = conv(q_ref, cq_ref)
    qs[...] = q * lax.rsqrt(jnp.sum(q * q, axis=-1, keepdims=True) + EPS) * HD ** -0.5
    k = conv(k_ref, ck_ref)
    ks[...] = k * lax.rsqrt(jnp.sum(k * k, axis=-1, keepdims=True) + EPS)
    vs[...] = conv(v_ref, cv_ref)

    incl, _ = _chunk_masks()
    incl_b = tuple(_as_bf16_mask(m) for m in incl)
    ri = lax.broadcasted_iota(jnp.int32, (CHUNK, HD), 0)
    li = lax.broadcasted_iota(jnp.int32, (CHUNK, HD), 1)
    cj = li % CHUNK
    incl2 = (ri >= cj, ri <= cj)
    strict2 = (ri > cj, ri < cj)
    right = li >= CHUNK
    eye_right = jnp.where(li == ri + CHUNK, 1.0, 0.0)
    sel_r = lax.broadcasted_iota(jnp.int32, (HD, 4 * HD), 0)
    sel_c = lax.broadcasted_iota(jnp.int32, (HD, 4 * HD), 1)
    sel = _as_bf16_mask(sel_r == (sel_c // HD) * HEADS + h)
    neg_a = [-jnp.exp(jnp.full((1, HD), alog_ref[d, h], F32)) for d in range(2)]
    dtb = [dtb_ref[d, h] for d in range(2)]

    def prep(g, carry):
        r0 = pl.multiple_of(g * (group * CHUNK), group * CHUNK)
        rows_g = pl.ds(r0, group * CHUNK)
        ab = _dot_mask(ab_ref[rows_g, :], sel)
        gb_all = [neg_a[d] * _softplus(ab[:, d * HD:(d + 1) * HD] + dtb[d]) for d in range(2)]
        bt_all = [jax.nn.sigmoid(ab[:, (2 + d) * HD:(3 + d) * HD]) for d in range(2)]
        q_g, k_g, v_g = qs[rows_g, :], ks[rows_g, :], vs[rows_g, :]
        chains = [(c, d) for c in range(group) for d in range(2)]
        sl = lambda c: slice(c * CHUNK, (c + 1) * CHUNK)
        kcbs = [k_g[sl(c)].astype(BF16) for c in range(group)]
        kk2 = [_dot_nt(kb_, jnp.concatenate([kb_, kb_], axis=0)) for kb_ in kcbs]
        qk = [_dot_nt(q_g[sl(c)].astype(BF16), kcbs[c]) for c in range(group)]
        gcs = [_mask_dot(incl_b[d], gb_all[d][sl(c)]) for c, d in chains]
        grs = [jnp.concatenate([gc, gc], axis=0).T[:CHUNK, :] for gc in gcs]
        decays = [jnp.where(incl2[d], jnp.exp(jnp.where(incl2[d], gc - gr, 0.0)), 0.0)
                  for (c, d), gc, gr in zip(chains, gcs, grs)]
        zs = [jnp.where(strict2[d] & ~right, -(kk2[c] * bt_all[d][sl(c)] * dec), 0.0) + eye_right
              for (c, d), dec in zip(chains, decays)]
        for _ in range(6):
            zs = [_dot3(z[:, :CHUNK], z) + jnp.where(right, z, 0.0) for z in zs]
        zero_rows = jnp.zeros((CHUNK, 2 * HD), BF16)
        for (c, d), gc, dec, z in zip(chains, gcs, decays, zs):
            rows = pl.ds(r0 + c * CHUNK, CHUNK)
            qc, kc, vc, bt = q_g[sl(c)], k_g[sl(c)], v_g[sl(c)], bt_all[d][sl(c)]
            eg = jnp.exp(gc)
            g_last = gc[CHUNK - 1:CHUNK, :] if d == 0 else gc[0:1, :]
            rhs = jnp.concatenate([(kc * bt * eg).astype(BF16), (vc * bt).astype(BF16)], axis=1)
            wu = _dot(z.astype(BF16), jnp.concatenate([zero_rows, rhs], axis=0)).astype(BF16)
            a_qk = jnp.where(incl[d], qk[c] * dec[:, :CHUNK], 0.0).astype(BF16)
            k_dec = (kc * jnp.exp(g_last - gc)).astype(BF16)
            mn = _dot_tn(k_dec, wu)
            qo = _dot(a_qk, wu)
            mrows = pl.ds(pl.multiple_of((g * group + c) * HD, HD), HD)
            m_s[d, mrows, :] = mn[:, :HD].astype(BF16)
            n_s[d, mrows, :] = mn[:, HD:]
            qp_s[d, rows, :] = (qc * eg - qo[:, :HD]).astype(BF16)
            o_s[d, rows, :] = qo[:, HD:]
            gl_rows = pl.ds(pl.multiple_of((g * group + c) * 8, 8), 8)
            gl_s[d, gl_rows, :] = jnp.broadcast_to(jnp.exp(g_last), (8, HD))
        return carry

    lax.fori_loop(0, nchunk // group, prep, 0)

    def step(d, ci, s):
        rows = pl.ds(pl.multiple_of(ci * CHUNK, CHUNK), CHUNK)
        mrows = pl.ds(pl.multiple_of(ci * HD, HD), HD)
        s_b16 = s.astype(BF16)
        o_s[d, rows, :] = o_s[d, rows, :] + _dot(qp_s[d, rows, :], s_b16)
        g_last = gl_s[d, pl.ds(pl.multiple_of(ci * 8, 8), 8), :][0:1, :]
        return s * g_last + (n_s[d, mrows, :] - _dot(m_s[d, mrows, :], s_b16))

    def body(n, carry):
        return step(0, n, carry[0]), step(1, nchunk - 1 - n, carry[1])

    s_f, s_b = lax.fori_loop(0, nchunk, body, (s0f_ref[0, 0], s0b_ref[0, 0]))
    sf_ref[0, 0] = s_f
    sb_ref[0, 0] = s_b
    y_ref[...] = _head_rms_gate(o_s[0] + o_s[1], ng_ref, z_ref).astype(BF16)


def _deltanet(p, conv_all, layer, a_log, dt_bias, norm_g, s0f, s0b, t):
    n = p.shape[0]
    bsz = n // t
    nchunk = t // CHUNK
    group = _pick(nchunk, (4, 2, 1))
    col = lambda name: (lambda b, h: (b, COL[name] + h))
    state = pl.BlockSpec((1, 1, HD, HD), lambda b, h: (b, h, 0, 0))
    smem = pl.BlockSpec(memory_space=pltpu.SMEM)
    conv = lambda off: pl.BlockSpec((None, 3, HD), lambda b, h: (layer, 0, off + h))
    st_shape = jax.ShapeDtypeStruct((bsz, HEADS, HD, HD), F32)
    seq_f32 = pltpu.VMEM((t, HD), F32)
    return pl.pallas_call(
        functools.partial(_dn_kernel, group=group),
        grid=(bsz, HEADS),
        in_specs=[smem, smem,
                  _seq_spec(t, col("dn_q")), _seq_spec(t, col("dn_k")),
                  _seq_spec(t, col("dn_v")), _seq_spec(t, col("dn_z")),
                  _seq_spec(t, lambda b, h: (b, COL["dn_ab"])),
                  conv(0), conv(HEADS), conv(2 * HEADS),
                  pl.BlockSpec((1, HD), lambda b, h: (0, 0)), state, state],
        out_specs=[pl.BlockSpec((t, HD), lambda b, h: (b, h)), state, state],
        out_shape=[jax.ShapeDtypeStruct((n, BRANCH_W), BF16), st_shape, st_shape],
        scratch_shapes=[seq_f32, seq_f32, seq_f32,
                        pltpu.VMEM((2, nchunk * HD, HD), BF16), pltpu.VMEM((2, nchunk * HD, HD), F32),
                        pltpu.VMEM((2, t, HD), BF16), pltpu.VMEM((2, nchunk * 8, HD), F32),
                        pltpu.VMEM((2, t, HD), F32)],
        compiler_params=_cparams(("parallel", "parallel"), 56),
        name="deltanet",
    )(a_log, dt_bias, p, p, p, p, p, conv_all, conv_all, conv_all, norm_g, s0f, s0b)


def _hg_kernel(q_ref, f0_ref, f1_ref, i_ref, z_ref, llb_ref, l1m_ref, oml_ref, ng_ref, s0f_ref,
               s0b_ref, y_ref, sf_ref, sb_ref, qs, lfs, ks, o_s):
    t = q_ref.shape[0]
    nchunk = t // CHUNK
    nsub = CHUNK // SUB
    qs[...] = _silu(q_ref[...])
    for d, f_ref in enumerate((f0_ref, f1_ref)):
        f = f_ref[...]
        lsig = jnp.minimum(f, 0.0) - jnp.log1p(jnp.exp(-jnp.abs(f)))
        a = llb_ref[d:d + 1, :]
        b = l1m_ref[d:d + 1, :] + lsig
        lfs[d] = jnp.maximum(a, b) + jnp.log1p(jnp.exp(-jnp.abs(a - b)))
        ks[d] = oml_ref[d:d + 1, :] * jax.nn.sigmoid(-f)

    incl, _ = _chunk_masks()
    incl_b = tuple(_as_bf16_mask(m) for m in incl)
    sub_i = lax.broadcasted_iota(jnp.int32, (SUB, HD), 0)

    def chunk(d, c0, s_t):
        qc = qs[pl.ds(c0, CHUNK), :]
        kc = ks[d, pl.ds(c0, CHUNK), :]
        vc = i_ref[pl.ds(c0, CHUNK), :]
        vb = vc.astype(BF16)
        gc = _mask_dot(incl_b[d], lfs[d, pl.ds(c0, CHUNK), :])
        g_last = gc[CHUNK - 1:CHUNK, :] if d == 0 else gc[0:1, :]
        o_inter = _dot_nt((qc * jnp.exp(gc)).astype(BF16), s_t.astype(BF16))
        k_dec = kc * jnp.exp(g_last - gc)
        s_t = s_t * jnp.exp(g_last) + _dot_tn(vb, k_dec.astype(BF16))
        outs = []
        for ib in range(nsub):
            q_i = qc[ib * SUB:(ib + 1) * SUB, :]
            g_i = gc[ib * SUB:(ib + 1) * SUB, :]
            acc = jnp.zeros((SUB, HD), F32)
            for jj in range(SUB):
                j = ib * SUB + jj
                term = q_i * kc[j:j + 1, :] * jnp.exp(jnp.minimum(g_i - gc[j:j + 1, :], 0.0))
                keep = (sub_i >= jj) if d == 0 else (sub_i <= jj)
                acc = acc + jnp.sum(jnp.where(keep, term, 0.0), axis=-1, keepdims=True) * vc[j:j + 1, :]
            if d == 0 and ib > 0:
                lo, hi = 0, ib * SUB
                ref = gc[hi - 1:hi, :]
            elif d == 1 and ib < nsub - 1:
                lo, hi = (ib + 1) * SUB, CHUNK
                ref = gc[lo:lo + 1, :]
            else:
                lo = hi = 0
            if hi > lo:
                q_t = (q_i * jnp.exp(jnp.minimum(g_i - ref, 0.0))).astype(BF16)
                k_t = (kc[lo:hi, :] * jnp.exp(jnp.minimum(ref - gc[lo:hi, :], 0.0))).astype(BF16)
                acc = acc + _dot(_dot_nt(q_t, k_t).astype(BF16), vb[lo:hi, :])
            outs.append(acc)
        return o_inter + jnp.concatenate(outs, axis=0), s_t

    def body(n, carry):
        s_f, s_b = carry
        cf = pl.multiple_of(n * CHUNK, CHUNK)
        cb = pl.multiple_of((nchunk - 1 - n) * CHUNK, CHUNK)
        o_f, s_f = chunk(0, cf, s_f)
        o_s[0, pl.ds(cf, CHUNK), :] = o_f
        o_b, s_b = chunk(1, cb, s_b)
        o_s[1, pl.ds(cb, CHUNK), :] = o_b
        return s_f, s_b

    s_f, s_b = lax.fori_loop(0, nchunk, body, (s0f_ref[0, 0], s0b_ref[0, 0]))
    sf_ref[0, 0] = s_f
    sb_ref[0, 0] = s_b
    y_ref[...] = _head_rms_gate(o_s[0] + o_s[1], ng_ref, z_ref).astype(BF16)


def _hgrn2(p, log_lb, log_1m_lb, one_m_lb, norm_g, s0f, s0b, t):
    n = p.shape[0]
    bsz = n // t
    col = lambda name: (lambda b, h: (b, COL[name] + h))
    state = pl.BlockSpec((1, 1, HD, HD), lambda b, h: (b, h, 0, 0))
    lbs = pl.BlockSpec((2, HD), lambda b, h: (0, h))
    st_shape = jax.ShapeDtypeStruct((bsz, HEADS, HD, HD), F32)
    return pl.pallas_call(
        _hg_kernel,
        grid=(bsz, HEADS),
        in_specs=[_seq_spec(t, col("hg_q")), _seq_spec(t, col("hg_f0")),
                  _seq_spec(t, col("hg_f1")), _seq_spec(t, col("hg_i")),
                  _seq_spec(t, col("hg_z")), lbs, lbs, lbs,
                  pl.BlockSpec((1, HD), lambda b, h: (0, 0)), state, state],
        out_specs=[pl.BlockSpec((t, HD), lambda b, h: (b, h)), state, state],
        out_shape=[jax.ShapeDtypeStruct((n, BRANCH_W), BF16), st_shape, st_shape],
        scratch_shapes=[pltpu.VMEM((t, HD), F32), pltpu.VMEM((2, t, HD), F32),
                        pltpu.VMEM((2, t, HD), F32), pltpu.VMEM((2, t, HD), F32)],
        compiler_params=_cparams(("parallel", "parallel"), 48),
        name="hgrn2",
    )(p, p, p, p, p, log_lb, log_1m_lb, one_m_lb, norm_g, s0f, s0b)


def _rope_tables(t):
    pos = jnp.arange(t)
    row = (pos // GRID_W).astype(F32)
    col = (pos % GRID_W).astype(F32)
    n = DA_DH // 4
    inv = ROPE_THETA ** (-jnp.arange(n, dtype=F32) / n)
    ar, ac = row[:, None] * inv, col[:, None] * inv
    zero = jnp.zeros_like(ar)
    cos = jnp.concatenate([jnp.cos(ar), jnp.cos(ar), jnp.cos(ac), jnp.cos(ac)], axis=-1)
    s_up = jnp.concatenate([zero, jnp.sin(ar), zero, jnp.sin(ac)], axis=-1)
    s_dn = jnp.concatenate([-jnp.sin(ar), zero, -jnp.sin(ac), zero], axis=-1)
    tile = lambda a: jnp.concatenate([a, a], axis=-1)
    return tile(cos), tile(s_up), tile(s_dn)


def _rope(x, cos, s_up, s_dn):
    half = DA_DH // 4
    return x * cos + pltpu.roll(x, half, 1) * s_up + pltpu.roll(x, HD - half, 1) * s_dn


def _da_kernel(lam_ref, q_ref, kl_ref, kc_ref, vl_ref, vc_ref, z_ref, cq_ref, uq_ref, dq_ref,
               ck_ref, uk_ref, dk_ref, ng_ref, y_ref, k_s, vt_s, *, t_lat, out_scale):
    @pl.when(pl.program_id(2) == 0)
    def _():
        if t_lat:
            k_s[0:t_lat, :] = _rope(kl_ref[...], ck_ref[...], uk_ref[...], dk_ref[...]).astype(BF16)
            vt_s[:, 0:t_lat] = vl_ref[...].T.astype(BF16)
        k_s[t_lat:, :] = kc_ref[...].astype(BF16)
        vt_s[:, t_lat:] = vc_ref[...].T.astype(BF16)

    q = q_ref[...]
    if t_lat:
        q = _rope(q, cq_ref[...], uq_ref[...], dq_ref[...])
    q = q * (DA_DH ** -0.5 * LOG2E)
    lane = lax.broadcasted_iota(jnp.int32, q.shape, 1)
    k_all = k_s[...]

    def exp_map(qm):
        s = _dot_nt(k_all, qm.astype(BF16))
        e = jnp.exp2(s - jnp.max(s, axis=0, keepdims=True))
        return e, jnp.sum(e, axis=0, keepdims=True)

    e0, l0 = exp_map(jnp.where(lane < DA_DH, q, 0.0))
    e1, l1 = exp_map(jnp.where(lane >= DA_DH, q, 0.0))
    r0 = 1.0 / l0
    a = e0 - (lam_ref[0] * l0 * (1.0 / l1)) * e1
    o = (_dot(vt_s[...], a.astype(BF16)) * r0).T
    y = o * lax.rsqrt(jnp.mean(o * o, axis=-1, keepdims=True) + EPS) * ng_ref[...]
    y_ref[...] = (y * out_scale * _silu(z_ref[...])).astype(BF16)


def _diff_attention(lam, p_q, p_lat, p_ctx, rope, norm_g, t, t_ctx, out_scale, latent):
    n = p_q.shape[0]
    bsz = n // t
    tq = _pick(t, (256, 128))
    nq = t // tq
    t_lat = t if latent else 0
    t_kl = t if latent else t_ctx
    cos, s_up, s_dn = rope
    col = lambda name: (lambda b, h, i: (b, COL[name] + h))
    qcol = lambda name: (lambda b, h, i: (b * nq + i, COL[name] + h))
    tab_q = pl.BlockSpec((tq, HD), lambda b, h, i: (i, 0))
    tab_k = _seq_spec(t_kl, lambda b, h, i: (0, 0))
    return pl.pallas_call(
        functools.partial(_da_kernel, t_lat=t_lat, out_scale=out_scale),
        grid=(bsz, HEADS, nq),
        in_specs=[pl.BlockSpec(memory_space=pltpu.SMEM),
                  pl.BlockSpec((tq, HD), qcol("da_q")),
                  _seq_spec(t_kl, col("da_k")), _seq_spec(t_ctx, col("da_k")),
                  _seq_spec(t_kl, col("da_v")), _seq_spec(t_ctx, col("da_v")),
                  pl.BlockSpec((tq, HD), qcol("da_z")),
                  tab_q, tab_q, tab_q, tab_k, tab_k, tab_k,
                  pl.BlockSpec((1, HD), lambda b, h, i: (0, 0))],
        out_specs=pl.BlockSpec((tq, HD), lambda b, h, i: (b * nq + i, h)),
        out_shape=jax.ShapeDtypeStruct((n, BRANCH_W), BF16),
        scratch_shapes=[pltpu.VMEM((t_lat + t_ctx, HD), BF16), pltpu.VMEM((HD, t_lat + t_ctx), BF16)],
        compiler_params=_cparams(("parallel", "parallel", "arbitrary"), 56),
        name="diff_attention",
    )(lam, p_q, p_lat, p_ctx, p_lat, p_ctx, p_q, cos, s_up, s_dn, cos, s_up, s_dn, norm_g)


def _merge_kernel(y0_ref, y1_ref, y2_ref, y3_ref, gl_ref, wb_ref, wo_ref, x_ref, mod_ref, fg_ref,
                  o_ref, acc_ref, *, final):
    n = pl.program_id(1)
    for k, y_ref in enumerate((y0_ref, y1_ref, y2_ref, y3_ref)):
        @pl.when(n == k)
        def _(k=k, y_ref=y_ref):
            c = jax.nn.sigmoid(gl_ref[...]) * _dot(y_ref[...], wb_ref[k])
            if k == 0:
                acc_ref[...] = c
            else:
                acc_ref[...] += c

    @pl.when(n == N_BRANCH - 1)
    def _():
        x = x_ref[...] + mod_ref[0, 2:3, :] * _dot(acc_ref[...].astype(BF16), wo_ref[...])
        if final:
            x = x * lax.rsqrt(jnp.mean(x * x, axis=-1, keepdims=True) + EPS) * fg_ref[...]
        o_ref[...] = x


def _merge(ys, p, wb_all, wo_all, layer, x2, mod, final_g, rows_per_seg, final):
    n, d = x2.shape
    tm = _pick(rows_per_seg, (256, 128))
    ysp = pl.BlockSpec((tm, BRANCH_W), lambda i, k: (i, 0))
    return pl.pallas_call(
        functools.partial(_merge_kernel, final=final),
        grid=(n // tm, N_BRANCH),
        in_specs=[ysp, ysp, ysp, ysp,
                  pl.BlockSpec((tm, d), lambda i, k: (i, k)),
                  pl.BlockSpec((None, N_BRANCH, BRANCH_W, d), lambda i, k: (layer, 0, 0, 0),
                               pipeline_mode=pl.Buffered(1)),
                  pl.BlockSpec((None, d, d), lambda i, k: (layer, 0, 0), pipeline_mode=pl.Buffered(1)),
                  pl.BlockSpec((tm, d), lambda i, k: (i, 0)),
                  pl.BlockSpec((1, 3, d), lambda i, k: (i * tm // rows_per_seg, 0, 0)),
                  pl.BlockSpec((1, d), lambda i, k: (0, 0))],
        out_specs=pl.BlockSpec((tm, d), lambda i, k: (i, 0)),
        out_shape=jax.ShapeDtypeStruct((n, d), F32),
        scratch_shapes=[pltpu.VMEM((tm, d), F32)],
        compiler_params=_cparams(("parallel", "arbitrary"), 56),
        name="merge",
    )(*ys, p, wb_all, wo_all, x2, mod, final_g)


def kernel(x, c, ctx, c_ctx, norm_g, w_ada, b_ada, w_in, fn_w, fn_b, dn_conv, dn_a_log, dn_dt_bias,
           dn_norm, hg_lb_logits, hg_norm, da_lambda, da_norm, w_branch, w_out, final_g):
    bsz, t, d = x.shape
    t_ctx = ctx.shape[1]
    depth = w_in.shape[0]
    assert d == N_BRANCH * BRANCH_W and t % CHUNK == 0 and t_ctx % CHUNK == 0 and bsz + 1 <= 8

    w_in_r = jnp.concatenate(
        [w_in[:, :, GATE_W_OFF:], w_in[:, :, :AB_OFF], w_in[:, :, AB_OFF + 4 * HEADS:GATE_W_OFF],
         w_in[:, :, AB_OFF:AB_OFF + 4 * HEADS],
         jnp.zeros((depth, d, 2 * HD - 4 * HEADS), w_in.dtype)], axis=-1).astype(BF16)
    wb16, wo16, fnw16 = w_branch.astype(BF16), w_out.astype(BF16), fn_w.astype(BF16)

    c8 = jnp.concatenate([c, c_ctx[None, :], jnp.zeros((8 - bsz - 1, d), F32)], axis=0)
    mod = _ada(c8, w_ada, b_ada).reshape(depth, 8, 3, d)

    lb_all = jnp.cumsum(jax.nn.softmax(hg_lb_logits.astype(F32), axis=1), axis=1)
    lb_all = lb_all - lb_all[:, :1]
    log_lb, log_1m_lb, one_m_lb = jnp.log(lb_all), jnp.log1p(-lb_all), 1.0 - lb_all

    c_ch, s_ch = _dft_cos_sin(FN_GW)
    cs_ch = jnp.concatenate([c_ch, s_ch], axis=-1).astype(BF16)
    dft = {}
    for tt in (t, t_ctx):
        ct, st = _dft_cos_sin(tt)
        dft[tt] = (ct.astype(BF16), (-st).astype(BF16))
    rope_l = _rope_tables(t)
    rope_c = tuple(a[:t_ctx] for a in rope_l)

    xl = x.reshape(bsz * t, d)
    xc = ctx.reshape(bsz * t_ctx, d)
    zstate = jnp.zeros((bsz, HEADS, HD, HD), F32)
    for l in range(depth):
        last = l == depth - 1
        g_l = norm_g[l][None, :]
        mod_l, mod_c = mod[l, :bsz], mod[l, bsz:bsz + 1]
        pl_ = _proj(xl, mod_l, g_l, w_in_r, l, t)
        pc_ = _proj(xc, mod_c, g_l, w_in_r, l, bsz * t_ctx)

        fn_b_l = fn_b[l][None, :]
        y_fn_l = _fourier_positions(_fourier_channels(pl_, cs_ch), pl_, *dft[t], fnw16, fn_b_l, l, t)

        dn_n, hg_n, da_n = dn_norm[l][None, :], hg_norm[l][None, :], da_norm[l][None, :]
        y_dn_c, s_f, s_b = _deltanet(pc_, dn_conv, l, dn_a_log[l], dn_dt_bias[l], dn_n,
                                     zstate, zstate, t_ctx)
        y_dn_l, _, _ = _deltanet(pl_, dn_conv, l, dn_a_log[l], dn_dt_bias[l], dn_n, s_f, s_b, t)

        lbs = (log_lb[:, l], log_1m_lb[:, l], one_m_lb[:, l])
        y_hg_c, h_f, h_b = _hgrn2(pc_, *lbs, hg_n, zstate, zstate, t_ctx)
        y_hg_l, _, _ = _hgrn2(pl_, *lbs, hg_n, h_f, h_b, t)

        lam_init = 0.8 - 0.6 * math.exp(-0.3 * l)
        lp = da_lambda[l].astype(F32)
        lam = (jnp.exp(jnp.sum(lp[0] * lp[1])) - jnp.exp(jnp.sum(lp[2] * lp[3])) + lam_init).reshape(1)
        y_da_l = _diff_attention(lam, pl_, pl_, pc_, rope_l, da_n, t, t_ctx, 1.0 - lam_init, True)

        fg = final_g[None, :]
        new_xl = _merge((y_fn_l, y_dn_l, y_hg_l, y_da_l), pl_, wb16, wo16, l, xl, mod_l, fg, t, last)
        if not last:
            y_fn_c = _fourier_positions(_fourier_channels(pc_, cs_ch), pc_, *dft[t_ctx], fnw16,
                                        fn_b_l, l, t_ctx)
            y_da_c = _diff_attention(lam, pc_, pc_, pc_, rope_c, da_n, t_ctx, t_ctx,
                                     1.0 - lam_init, False)
            xc = _merge((y_fn_c, y_dn_c, y_hg_c, y_da_c), pc_, wb16, wo16, l, xc, mod_c, fg,
                        bsz * t_ctx, False)
        xl = new_xl
    return xl.reshape(bsz, t, d)
```

```python
import functools
import math

import jax
import jax.numpy as jnp
from jax import lax
from jax.experimental import pallas as pl
from jax.experimental.pallas import tpu as pltpu

F32 = jnp.float32
BF16 = jnp.bfloat16

EPS = 1e-6
N_BRANCH = 4
HEADS = 4
HD = 128
BRANCH_W = HEADS * HD
FN_GW = 128
CHUNK = 64
SUB = 16
HALF = SUB // 2
GRID_W = 64
ROPE_THETA = 10000.0
DA_DH = 64
LOG2E = 1.4426950408889634

COL = dict(fn_u=64, fn_z=68, dn_q=72, dn_k=76, dn_v=80, dn_z=84, hg_q=88, hg_f0=92, hg_f1=96,
           hg_i=100, hg_z=104, da_q=108, da_k=112, da_v=116, da_z=120, dn_ab=124)
PROJ_W = 126 * 128
GATE_W_OFF = 7696
AB_OFF = 3072

MIB = 1024 * 1024


def _cparams(sem, vmem_mib):
    return pltpu.CompilerParams(dimension_semantics=sem, vmem_limit_bytes=vmem_mib * MIB)


def _silu(x):
    return x * jax.nn.sigmoid(x)


def _dot(a, b):
    return jnp.dot(a, b, preferred_element_type=F32)


def _dot_nt(a, b):
    return lax.dot_general(a, b, (((1,), (1,)), ((), ())), preferred_element_type=F32)


def _dot_tn(a, b):
    return lax.dot_general(a, b, (((0,), (0,)), ((), ())), preferred_element_type=F32)


def _split2(x):
    hi = x.astype(BF16)
    return hi, (x - hi.astype(F32)).astype(BF16)


def _split3(x):
    hi = x.astype(BF16)
    r = x - hi.astype(F32)
    mid = r.astype(BF16)
    return hi, mid, (r - mid.astype(F32)).astype(BF16)


def _dot3(a, b):
    a_hi, a_lo = _split2(a)
    b_hi, b_lo = _split2(b)
    return _dot(a_hi, b_hi) + (_dot(a_hi, b_lo) + _dot(a_lo, b_hi))


def _mask_dot(mask, x):
    hi, mid, lo = _split3(x)
    return _dot(mask, hi) + (_dot(mask, mid) + _dot(mask, lo))


def _dot_mask(x, mask):
    hi, mid, lo = _split3(x)
    return _dot(hi, mask) + (_dot(mid, mask) + _dot(lo, mask))


def _pick(n, cands):
    for c in cands:
        if n % c == 0:
            return c
    return n


def _ada_kernel(c_ref, w_ref, b_ref, o_ref):
    sc = _silu(c_ref[...])
    o_ref[0] = _dot(sc.astype(BF16), w_ref[0].astype(BF16)) + b_ref[0]


def _ada(c8, w_ada, b_ada):
    depth, d, d3 = w_ada.shape
    tn = _pick(d3, (1536, 768, 512, 256, 128))
    return pl.pallas_call(
        _ada_kernel,
        grid=(depth, d3 // tn),
        in_specs=[pl.BlockSpec((8, d), lambda l, n: (0, 0)),
                  pl.BlockSpec((1, d, tn), lambda l, n: (l, 0, n)),
                  pl.BlockSpec((1, 1, tn), lambda l, n: (l, 0, n))],
        out_specs=pl.BlockSpec((1, 8, tn), lambda l, n: (l, 0, n)),
        out_shape=jax.ShapeDtypeStruct((depth, 8, d3), F32),
        compiler_params=_cparams(("parallel", "parallel"), 48),
        name="adaln_mod",
    )(c8, w_ada, b_ada.reshape(depth, 1, d3))


def _proj_kernel(x_ref, mod_ref, g_ref, w_ref, o_ref, h_ref):
    @pl.when(pl.program_id(1) == 0)
    def _():
        x = x_ref[...]
        y = x * lax.rsqrt(jnp.mean(x * x, axis=-1, keepdims=True) + EPS) * g_ref[...]
        h = y * (1.0 + mod_ref[0, 1:2, :]) + mod_ref[0, 0:1, :]
        h_ref[...] = h.astype(BF16)

    o_ref[...] = _dot(h_ref[...], w_ref[...])


def _proj(x2, mod, g, w_all, layer, rows_per_seg):
    n, d = x2.shape
    tm = _pick(rows_per_seg, (1024, 512, 256, 128))
    tn = 768
    return pl.pallas_call(
        _proj_kernel,
        grid=(n // tm, PROJ_W // tn),
        in_specs=[pl.BlockSpec((tm, d), lambda i, j: (i, 0)),
                  pl.BlockSpec((1, 3, d), lambda i, j: (i * tm // rows_per_seg, 0, 0)),
                  pl.BlockSpec((1, d), lambda i, j: (0, 0)),
                  pl.BlockSpec((None, d, tn), lambda i, j: (layer, 0, j))],
        out_specs=pl.BlockSpec((tm, tn), lambda i, j: (i, j)),
        out_shape=jax.ShapeDtypeStruct((n, PROJ_W), F32),
        scratch_shapes=[pltpu.VMEM((tm, d), BF16)],
        compiler_params=_cparams(("parallel", "arbitrary"), 48),
        name="in_proj",
    )(x2, mod, g, w_all)


def _dft_cos_sin(n):
    j = jnp.arange(n, dtype=jnp.int32)
    sc = n ** -0.5
    if n <= 1024:
        ang = (2.0 * math.pi / n) * ((j[:, None] * j[None, :]) % n).astype(F32)
        return jnp.cos(ang) * sc, jnp.sin(ang) * sc
    m = n // 64
    k1 = jnp.arange(m, dtype=jnp.int32)
    k2 = jnp.arange(64, dtype=jnp.int32)
    a = (2.0 * math.pi / m) * ((j[:, None] * k1[None, :]) % m).astype(F32)
    b = (2.0 * math.pi / n) * ((j[:, None] * k2[None, :]) % n).astype(F32)
    ca, sa, cb, sb = jnp.cos(a), jnp.sin(a), jnp.cos(b), jnp.sin(b)
    c = ca[:, :, None] * cb[:, None, :] - sa[:, :, None] * sb[:, None, :]
    s = sa[:, :, None] * cb[:, None, :] + ca[:, :, None] * sb[:, None, :]
    return c.reshape(n, n) * sc, s.reshape(n, n) * sc


def _fn1_kernel(u_ref, cs_ref, o_ref):
    for g in range(BRANCH_W // FN_GW):
        r = _dot(u_ref[:, g * FN_GW:(g + 1) * FN_GW].astype(BF16), cs_ref[...])
        o_ref[:, g * FN_GW:(g + 1) * FN_GW] = r[:, :FN_GW].astype(BF16)
        o_ref[:, BRANCH_W + g * FN_GW:BRANCH_W + (g + 1) * FN_GW] = r[:, FN_GW:].astype(BF16)


def _fourier_channels(p, cs_ch):
    n = p.shape[0]
    tm = _pick(n, (1024, 512, 256, 128))
    return pl.pallas_call(
        _fn1_kernel,
        grid=(n // tm,),
        in_specs=[pl.BlockSpec((tm, BRANCH_W), lambda i: (i, COL["fn_u"] // 4)),
                  pl.BlockSpec((FN_GW, 2 * FN_GW), lambda i: (0, 0))],
        out_specs=pl.BlockSpec((tm, 2 * BRANCH_W), lambda i: (i, 0)),
        out_shape=jax.ShapeDtypeStruct((n, 2 * BRANCH_W), BF16),
        compiler_params=_cparams(("parallel",), 32),
        name="fourier_channels",
    )(p, cs_ch)


def _fn2_kernel(ct_ref, st_ref, r_ref, z_ref, w_ref, b_ref, y_ref):
    f = _dot(ct_ref[...], r_ref[:, :BRANCH_W]) + _dot(st_ref[...], r_ref[:, BRANCH_W:])
    y = _dot(f.astype(BF16), w_ref[...]) + b_ref[...]
    y_ref[...] = (y * _silu(z_ref[...])).astype(BF16)


def _fourier_positions(r, p, ct, nst, fn_w_all, fn_b, layer, t):
    n = p.shape[0]
    bsz = n // t
    tm = _pick(t, (512, 256, 128))
    nt = t // tm
    return pl.pallas_call(
        _fn2_kernel,
        grid=(bsz, nt),
        in_specs=[pl.BlockSpec((tm, t), lambda b, i: (i, 0)),
                  pl.BlockSpec((tm, t), lambda b, i: (i, 0)),
                  pl.BlockSpec((t, 2 * BRANCH_W), lambda b, i: (b, 0)),
                  pl.BlockSpec((tm, BRANCH_W), lambda b, i: (b * nt + i, COL["fn_z"] // 4)),
                  pl.BlockSpec((None, BRANCH_W, BRANCH_W), lambda b, i: (layer, 0, 0)),
                  pl.BlockSpec((1, BRANCH_W), lambda b, i: (0, 0))],
        out_specs=pl.BlockSpec((tm, BRANCH_W), lambda b, i: (b * nt + i, 0)),
        out_shape=jax.ShapeDtypeStruct((n, BRANCH_W), BF16),
        compiler_params=_cparams(("parallel", "arbitrary"), 48),
        name="fourier_positions",
    )(ct, nst, r, p, fn_w_all, fn_b)


def _seq_spec(rows, index_map):
    return pl.BlockSpec((rows, HD), index_map, pipeline_mode=pl.Buffered(1))


def _chunk_masks():
    i = lax.broadcasted_iota(jnp.int32, (CHUNK, CHUNK), 0)
    j = lax.broadcasted_iota(jnp.int32, (CHUNK, CHUNK), 1)
    incl = (i >= j, i <= j)
    strict = (i > j, i < j)
    return incl, strict


def _as_bf16_mask(m):
    return jnp.where(m, 1.0, 0.0).astype(BF16)


def _softplus(x):
    return jnp.maximum(x, 0.0) + jnp.log1p(jnp.exp(-jnp.abs(x)))


def _head_rms_gate(o, g_ref, z_ref):
    y = o * lax.rsqrt(jnp.mean(o * o, axis=-1, keepdims=True) + EPS) * g_ref[...]
    return y * _silu(z_ref[...])


def _dn_kernel(alog_ref, dtb_ref, q_ref, k_ref, v_ref, z_ref, ab_ref, cq_ref, ck_ref, cv_ref,
               ng_ref, s0f_ref, s0b_ref, y_ref, sf_ref, sb_ref,
               qs, ks, vs, m_s, n_s, qp_s, gl_s, o_s, *, group):
    h = pl.program_id(1)
    t = q_ref.shape[0]
    nchunk = t // CHUNK
    row = lax.broadcasted_iota(jnp.int32, (t, HD), 0)

    def conv(x_ref, c_ref):
        x = x_ref[...]
        xm = jnp.where(row == 0, 0.0, pltpu.roll(x, 1, 0))
        xp = jnp.where(row == t - 1, 0.0, pltpu.roll(x, t - 1, 0))
        y = xm * c_ref[0:1, :] + x * c_ref[1:2, :] + xp * c_ref[2:3, :]
        return _silu(y)

    q = conv(q_ref, cq_ref)
    qs[...] = q * lax.rsqrt(jnp.sum(q * q, axis=-1, keepdims=True) + EPS) * HD ** -0.5
    k = conv(k_ref, ck_ref)
    ks[...] = k * lax.rsqrt(jnp.sum(k * k, axis=-1, keepdims=True) + EPS)
    vs[...] = conv(v_ref, cv_ref)

    incl, _ = _chunk_masks()
    incl_b = tuple(_as_bf16_mask(m) for m in incl)
    ri = lax.broadcasted_iota(jnp.int32, (CHUNK, HD), 0)
    li = lax.broadcasted_iota(jnp.int32, (CHUNK, HD), 1)
    cj = li % CHUNK
    incl2 = (ri >= cj, ri <= cj)
    strict2 = (ri > cj, ri < cj)
    right = li >= CHUNK
    eye_right = jnp.where(li == ri + CHUNK, 1.0, 0.0)
    sel_r = lax.broadcasted_iota(jnp.int32, (HD, 4 * HD), 0)
    sel_c = lax.broadcasted_iota(jnp.int32, (HD, 4 * HD), 1)
    sel = _as_bf16_mask(sel_r == (sel_c // HD) * HEADS + h)
    neg_a = [-jnp.exp(jnp.full((1, HD), alog_ref[d, h], F32)) for d in range(2)]
    dtb = [dtb_ref[d, h] for d in range(2)]

    def prep(g, carry):
        r0 = pl.multiple_of(g * (group * CHUNK), group * CHUNK)
        rows_g = pl.ds(r0, group * CHUNK)
        ab = _dot_mask(ab_ref[rows_g, :], sel)
        gb_all = [neg_a[d] * _softplus(ab[:, d * HD:(d + 1) * HD] + dtb[d]) for d in range(2)]
        bt_all = [jax.nn.sigmoid(ab[:, (2 + d) * HD:(3 + d) * HD]) for d in range(2)]
        q_g, k_g, v_g = qs[rows_g, :], ks[rows_g, :], vs[rows_g, :]
        chains = [(c, d) for c in range(group) for d in range(2)]
        sl = lambda c: slice(c * CHUNK, (c + 1) * CHUNK)
        kcbs = [k_g[sl(c)].astype(BF16) for c in range(group)]
        kk2 = [_dot_nt(kb_, jnp.concatenate([kb_, kb_], axis=0)) for kb_ in kcbs]
        qk = [_dot_nt(q_g[sl(c)].astype(BF16), kcbs[c]) for c in range(group)]
        gcs = [_mask_dot(incl_b[d], gb_all[d][sl(c)]) for c, d in chains]
        grs = [jnp.concatenate([gc, gc], axis=0).T[:CHUNK, :] for gc in gcs]
        decays = [jnp.where(incl2[d], jnp.exp(jnp.where(incl2[d], gc - gr, 0.0)), 0.0)
                  for (c, d), gc, gr in zip(chains, gcs, grs)]
        zs = [jnp.where(strict2[d] & ~right, -(kk2[c] * bt_all[d][sl(c)] * dec), 0.0) + eye_right
              for (c, d), dec in zip(chains, decays)]
        for _ in range(6):
            zs = [_dot3(z[:, :CHUNK], z) + jnp.where(right, z, 0.0) for z in zs]
        zero_rows = jnp.zeros((CHUNK, 2 * HD), BF16)
        egs = [jnp.exp(gc) for gc in gcs]
        g_lasts = [gc[CHUNK - 1:CHUNK, :] if d == 0 else gc[0:1, :] for (c, d), gc in zip(chains, gcs)]
        wus = [_dot(z.astype(BF16), jnp.concatenate(
                   [zero_rows,
                    jnp.concatenate([(k_g[sl(c)] * bt_all[d][sl(c)] * eg).astype(BF16),
                                     (v_g[sl(c)] * bt_all[d][sl(c)]).astype(BF16)], axis=1)],
                   axis=0)).astype(BF16)
               for (c, d), z, eg in zip(chains, zs, egs)]
        mns = [_dot_tn((k_g[sl(c)] * jnp.exp(gl - gc)).astype(BF16), wu)
               for (c, d), gc, gl, wu in zip(chains, gcs, g_lasts, wus)]
        qos = [_dot(jnp.where(incl[d], qk[c] * dec[:, :CHUNK], 0.0).astype(BF16), wu)
               for (c, d), dec, wu in zip(chains, decays, wus)]
        for (c, d), eg, gl, mn, qo in zip(chains, egs, g_lasts, mns, qos):
            rows = pl.ds(r0 + c * CHUNK, CHUNK)
            mrows = pl.ds(pl.multiple_of((g * group + c) * HD, HD), HD)
            m_s[d, mrows, :] = mn[:, :HD].astype(BF16)
            n_s[d, mrows, :] = mn[:, HD:]
            qp_s[d, rows, :] = (q_g[sl(c)] * eg - qo[:, :HD]).astype(BF16)
            o_s[d, rows, :] = qo[:, HD:]
            gl_rows = pl.ds(pl.multiple_of((g * group + c) * 8, 8), 8)
            gl_s[d, gl_rows, :] = jnp.broadcast_to(jnp.exp(gl), (8, HD))
        return carry

    lax.fori_loop(0, nchunk // group, prep, 0)

    def step(d, ci, s):
        rows = pl.ds(pl.multiple_of(ci * CHUNK, CHUNK), CHUNK)
        mrows = pl.ds(pl.multiple_of(ci * HD, HD), HD)
        s_b16 = s.astype(BF16)
        o_s[d, rows, :] = o_s[d, rows, :] + _dot(qp_s[d, rows, :], s_b16)
        g_last = gl_s[d, pl.ds(pl.multiple_of(ci * 8, 8), 8), :][0:1, :]
        return s * g_last + (n_s[d, mrows, :] - _dot(m_s[d, mrows, :], s_b16))

    def body(n, carry):
        return step(0, n, carry[0]), step(1, nchunk - 1 - n, carry[1])

    s_f, s_b = lax.fori_loop(0, nchunk, body, (s0f_ref[0, 0], s0b_ref[0, 0]))
    sf_ref[0, 0] = s_f
    sb_ref[0, 0] = s_b
    y_ref[...] = _head_rms_gate(o_s[0] + o_s[1], ng_ref, z_ref).astype(BF16)


def _deltanet(p, conv_all, layer, a_log, dt_bias, norm_g, s0f, s0b, t):
    n = p.shape[0]
    bsz = n // t
    nchunk = t // CHUNK
    group = _pick(nchunk, (4, 2, 1))
    col = lambda name: (lambda b, h: (b, COL[name] + h))
    state = pl.BlockSpec((1, 1, HD, HD), lambda b, h: (b, h, 0, 0))
    smem = pl.BlockSpec(memory_space=pltpu.SMEM)
    conv = lambda off: pl.BlockSpec((None, 3, HD), lambda b, h: (layer, 0, off + h))
    st_shape = jax.ShapeDtypeStruct((bsz, HEADS, HD, HD), F32)
    seq_f32 = pltpu.VMEM((t, HD), F32)
    return pl.pallas_call(
        functools.partial(_dn_kernel, group=group),
        grid=(bsz, HEADS),
        in_specs=[smem, smem,
                  _seq_spec(t, col("dn_q")), _seq_spec(t, col("dn_k")),
                  _seq_spec(t, col("dn_v")), _seq_spec(t, col("dn_z")),
                  _seq_spec(t, lambda b, h: (b, COL["dn_ab"])),
                  conv(0), conv(HEADS), conv(2 * HEADS),
                  pl.BlockSpec((1, HD), lambda b, h: (0, 0)), state, state],
        out_specs=[pl.BlockSpec((t, HD), lambda b, h: (b, h)), state, state],
        out_shape=[jax.ShapeDtypeStruct((n, BRANCH_W), BF16), st_shape, st_shape],
        scratch_shapes=[seq_f32, seq_f32, seq_f32,
                        pltpu.VMEM((2, nchunk * HD, HD), BF16), pltpu.VMEM((2, nchunk * HD, HD), F32),
                        pltpu.VMEM((2, t, HD), BF16), pltpu.VMEM((2, nchunk * 8, HD), F32),
                        pltpu.VMEM((2, t, HD), F32)],
        compiler_params=_cparams(("parallel", "parallel"), 56),
        name="deltanet",
    )(a_log, dt_bias, p, p, p, p, p, conv_all, conv_all, conv_all, norm_g, s0f, s0b)


def _hg_kernel(q_ref, f0_ref, f1_ref, i_ref, z_ref, llb_ref, l1m_ref, oml_ref, ng_ref, s0f_ref,
               s0b_ref, y_ref, sf_ref, sb_ref, qs, lfs, ks, o_s):
    t = q_ref.shape[0]
    nchunk = t // CHUNK
    nsub = CHUNK // SUB
    qs[...] = _silu(q_ref[...])
    for d, f_ref in enumerate((f0_ref, f1_ref)):
        f = f_ref[...]
        e_f = jnp.exp(-jnp.abs(f))
        lsig = jnp.minimum(f, 0.0) - jnp.log1p(e_f)
        a = llb_ref[d:d + 1, :]
        b = l1m_ref[d:d + 1, :] + lsig
        lfs[d] = jnp.maximum(a, b) + jnp.log1p(jnp.exp(-jnp.abs(a - b)))
        ks[d] = oml_ref[d:d + 1, :] * (jnp.where(f >= 0.0, e_f, 1.0) * (1.0 / (1.0 + e_f)))

    incl, _ = _chunk_masks()
    incl_b = tuple(_as_bf16_mask(m) for m in incl)
    sub_i = lax.broadcasted_iota(jnp.int32, (SUB, HD), 0)
    half_i = lax.broadcasted_iota(jnp.int32, (HALF, HD), 0)

    def factored(q_rows, g_rows, k_rows, gk_rows, ref, q_keep=None):
        q_t = q_rows * jnp.exp(jnp.minimum(g_rows - ref, 0.0))
        k_t = k_rows * jnp.exp(jnp.minimum(ref - gk_rows, 0.0))
        if q_keep is not None:
            q_t, k_t = jnp.where(q_keep, q_t, 0.0), jnp.where(q_keep, 0.0, k_t)
        return q_t.astype(BF16), k_t.astype(BF16)

    def body(n, carry):
        chains = []
        for u in range(per_trip):
            ci = n * per_trip + u
            chains.append((0, pl.multiple_of(ci * CHUNK, CHUNK)))
            chains.append((1, pl.multiple_of((nchunk - 1 - ci) * CHUNK, CHUNK)))
        state, o_inter, pair_acc, jobs = list(carry), [], [], []
        for cidx, (d, c0) in enumerate(chains):
            s_t = state[d]
            qc = qs[pl.ds(c0, CHUNK), :]
            kc = ks[d, pl.ds(c0, CHUNK), :]
            vc = i_ref[pl.ds(c0, CHUNK), :]
            vb = vc.astype(BF16)
            gc = _mask_dot(incl_b[d], lfs[d, pl.ds(c0, CHUNK), :])
            g_last = gc[CHUNK - 1:CHUNK, :] if d == 0 else gc[0:1, :]
            o_inter.append(_dot_nt((qc * jnp.exp(gc)).astype(BF16), s_t.astype(BF16)))
            k_dec = kc * jnp.exp(g_last - gc)
            state[d] = s_t * jnp.exp(g_last) + _dot_tn(vb, k_dec.astype(BF16))
            for ib in range(nsub):
                r0 = ib * SUB
                blk = slice(r0, r0 + SUB)
                mid = r0 + HALF
                ref = gc[mid - 1:mid, :] if d == 0 else gc[mid:mid + 1, :]
                q_keep = (sub_i >= HALF) if d == 0 else (sub_i < HALF)
                jobs.append((cidx, ib, factored(qc[blk], gc[blk], kc[blk], gc[blk], ref, q_keep),
                             vb[blk]))
                if d == 0 and ib > 0:
                    rng, ref = slice(0, r0), gc[r0 - 1:r0, :]
                elif d == 1 and ib < nsub - 1:
                    rng, ref = slice(r0 + SUB, CHUNK), gc[r0 + SUB:r0 + SUB + 1, :]
                else:
                    continue
                jobs.append((cidx, ib, factored(qc[blk], gc[blk], kc[rng], gc[rng], ref), vb[rng]))
            halves = []
            for h0 in range(0, CHUNK, HALF):
                q_h, g_h = qc[h0:h0 + HALF, :], gc[h0:h0 + HALF, :]
                acc_h = jnp.zeros((HALF, HD), F32)
                for jj in range(HALF):
                    j = h0 + jj
                    term = q_h * kc[j:j + 1, :] * jnp.exp(jnp.minimum(g_h - gc[j:j + 1, :], 0.0))
                    keep = (half_i >= jj) if d == 0 else (half_i <= jj)
                    acc_h = acc_h + (jnp.sum(jnp.where(keep, term, 0.0), axis=-1, keepdims=True)
                                     * vc[j:j + 1, :])
                halves.append(acc_h)
            pair_acc.append(halves)
        scores = [_dot_nt(q_t, k_t).astype(BF16) for _, _, (q_t, k_t), _ in jobs]
        outs = [_dot(a, v_rows) for a, (_, _, _, v_rows) in zip(scores, jobs)]
        for cidx, (d, c0) in enumerate(chains):
            blocks = []
            for ib in range(nsub):
                acc = jnp.concatenate(pair_acc[cidx][2 * ib:2 * ib + 2], axis=0)
                for o_job, (jc, jb, _, _) in zip(outs, jobs):
                    if (jc, jb) == (cidx, ib):
                        acc = acc + o_job
                blocks.append(acc)
            o_s[d, pl.ds(c0, CHUNK), :] = o_inter[cidx] + jnp.concatenate(blocks, axis=0)
        return tuple(state)

    per_trip = 2 if nchunk % 2 == 0 else 1
    s_f, s_b = lax.fori_loop(0, nchunk // per_trip, body, (s0f_ref[0, 0], s0b_ref[0, 0]))
    sf_ref[0, 0] = s_f
    sb_ref[0, 0] = s_b
    y_ref[...] = _head_rms_gate(o_s[0] + o_s[1], ng_ref, z_ref).astype(BF16)


def _hgrn2(p, log_lb, log_1m_lb, one_m_lb, norm_g, s0f, s0b, t):
    n = p.shape[0]
    bsz = n // t
    col = lambda name: (lambda b, h: (b, COL[name] + h))
    state = pl.BlockSpec((1, 1, HD, HD), lambda b, h: (b, h, 0, 0))
    lbs = pl.BlockSpec((2, HD), lambda b, h: (0, h))
    st_shape = jax.ShapeDtypeStruct((bsz, HEADS, HD, HD), F32)
    return pl.pallas_call(
        _hg_kernel,
        grid=(bsz, HEADS),
        in_specs=[_seq_spec(t, col("hg_q")), _seq_spec(t, col("hg_f0")),
                  _seq_spec(t, col("hg_f1")), _seq_spec(t, col("hg_i")),
                  _seq_spec(t, col("hg_z")), lbs, lbs, lbs,
                  pl.BlockSpec((1, HD), lambda b, h: (0, 0)), state, state],
        out_specs=[pl.BlockSpec((t, HD), lambda b, h: (b, h)), state, state],
        out_shape=[jax.ShapeDtypeStruct((n, BRANCH_W), BF16), st_shape, st_shape],
        scratch_shapes=[pltpu.VMEM((t, HD), F32), pltpu.VMEM((2, t, HD), F32),
                        pltpu.VMEM((2, t, HD), F32), pltpu.VMEM((2, t, HD), F32)],
        compiler_params=_cparams(("parallel", "parallel"), 48),
        name="hgrn2",
    )(p, p, p, p, p, log_lb, log_1m_lb, one_m_lb, norm_g, s0f, s0b)


def _rope_tables(t):
    pos = jnp.arange(t)
    row = (pos // GRID_W).astype(F32)
    col = (pos % GRID_W).astype(F32)
    n = DA_DH // 4
    inv = ROPE_THETA ** (-jnp.arange(n, dtype=F32) / n)
    ar, ac = row[:, None] * inv, col[:, None] * inv
    zero = jnp.zeros_like(ar)
    cos = jnp.concatenate([jnp.cos(ar), jnp.cos(ar), jnp.cos(ac), jnp.cos(ac)], axis=-1)
    s_up = jnp.concatenate([zero, jnp.sin(ar), zero, jnp.sin(ac)], axis=-1)
    s_dn = jnp.concatenate([-jnp.sin(ar), zero, -jnp.sin(ac), zero], axis=-1)
    tile = lambda a: jnp.concatenate([a, a], axis=-1)
    return tile(cos), tile(s_up), tile(s_dn)


def _rope(x, cos, s_up, s_dn):
    half = DA_DH // 4
    return x * cos + pltpu.roll(x, half, 1) * s_up + pltpu.roll(x, HD - half, 1) * s_dn


def _da_kernel(lam_ref, q_ref, kl_ref, kc_ref, vl_ref, vc_ref, z_ref, cq_ref, uq_ref, dq_ref,
               ck_ref, uk_ref, dk_ref, ng_ref, y_ref, k_s, vt_s, s_s, *, t_lat, out_scale):
    nkb, kblk, _ = k_s.shape
    nlat = t_lat // kblk

    @pl.when(pl.program_id(2) == 0)
    def _():
        for j in range(nkb):
            if j < nlat:
                rows = slice(j * kblk, (j + 1) * kblk)
                k = _rope(kl_ref[rows, :], ck_ref[rows, :], uk_ref[rows, :], dk_ref[rows, :])
                v = vl_ref[rows, :]
            else:
                rows = slice((j - nlat) * kblk, (j - nlat + 1) * kblk)
                k, v = kc_ref[rows, :], vc_ref[rows, :]
            k_s[j] = k.astype(BF16)
            vt_s[j] = v.T.astype(BF16)

    q = q_ref[...]
    if t_lat:
        q = _rope(q, cq_ref[...], uq_ref[...], dq_ref[...])
    q = q * (DA_DH ** -0.5 * LOG2E)
    lane = lax.broadcasted_iota(jnp.int32, q.shape, 1)
    qms = [jnp.where(lane < DA_DH, q, 0.0).astype(BF16), jnp.where(lane >= DA_DH, q, 0.0).astype(BF16)]
    tq = q.shape[0]
    m = [None, None]
    l = [jnp.zeros((1, tq), F32) for _ in range(2)]
    acc = [jnp.zeros((HD, tq), F32) for _ in range(2)]

    def scores(mp, j):
        s = _dot_nt(k_s[j], qms[mp])
        s_s[mp, j] = s
        bm = jnp.max(s, axis=0, keepdims=True)
        m[mp] = bm if m[mp] is None else jnp.maximum(m[mp], bm)

    def values(mp, j):
        e = jnp.exp2(s_s[mp, j] - m[mp])
        l[mp] = l[mp] + jnp.sum(e, axis=0, keepdims=True)
        acc[mp] = acc[mp] + _dot(vt_s[j], e.astype(BF16))

    for j in range(nkb):
        scores(0, j)
    for j in range(nkb):
        scores(1, j)
        values(0, j)
    for j in range(nkb):
        values(1, j)
    o = (acc[0] * (1.0 / l[0]) - acc[1] * (lam_ref[0] * (1.0 / l[1]))).T
    y = o * lax.rsqrt(jnp.mean(o * o, axis=-1, keepdims=True) + EPS) * ng_ref[...]
    y_ref[...] = (y * out_scale * _silu(z_ref[...])).astype(BF16)


def _diff_attention(lam, p_q, p_lat, p_ctx, rope, norm_g, t, t_ctx, out_scale, latent):
    n = p_q.shape[0]
    bsz = n // t
    tq = _pick(t, (512, 256, 128))
    nq = t // tq
    t_lat = t if latent else 0
    t_kl = t if latent else t_ctx
    cos, s_up, s_dn = rope
    kblk = 256 if (t_lat % 256 == 0 and t_ctx % 256 == 0) else 128
    nkb = (t_lat + t_ctx) // kblk
    col = lambda name: (lambda b, h, i: (b, COL[name] + h))
    qcol = lambda name: (lambda b, h, i: (b * nq + i, COL[name] + h))
    tab_q = pl.BlockSpec((tq, HD), lambda b, h, i: (i, 0))
    tab_k = _seq_spec(t_kl, lambda b, h, i: (0, 0))
    return pl.pallas_call(
        functools.partial(_da_kernel, t_lat=t_lat, out_scale=out_scale),
        grid=(bsz, HEADS, nq),
        in_specs=[pl.BlockSpec(memory_space=pltpu.SMEM),
                  pl.BlockSpec((tq, HD), qcol("da_q")),
                  _seq_spec(t_kl, col("da_k")), _seq_spec(t_ctx, col("da_k")),
                  _seq_spec(t_kl, col("da_v")), _seq_spec(t_ctx, col("da_v")),
                  pl.BlockSpec((tq, HD), qcol("da_z")),
                  tab_q, tab_q, tab_q, tab_k, tab_k, tab_k,
                  pl.BlockSpec((1, HD), lambda b, h, i: (0, 0))],
        out_specs=pl.BlockSpec((tq, HD), lambda b, h, i: (b * nq + i, h)),
        out_shape=jax.ShapeDtypeStruct((n, BRANCH_W), BF16),
        scratch_shapes=[pltpu.VMEM((nkb, kblk, HD), BF16), pltpu.VMEM((nkb, HD, kblk), BF16),
                        pltpu.VMEM((2, nkb, kblk, tq), F32)],
        compiler_params=_cparams(("parallel", "parallel", "arbitrary"), 56),
        name="diff_attention",
    )(lam, p_q, p_lat, p_ctx, p_lat, p_ctx, p_q, cos, s_up, s_dn, cos, s_up, s_dn, norm_g)


def _merge_kernel(y0_ref, y1_ref, y2_ref, y3_ref, gl_ref, wb_ref, wo_ref, x_ref, mod_ref, fg_ref,
                  o_ref, *, final):
    d = x_ref.shape[1]
    acc = None
    for k, y_ref in enumerate((y0_ref, y1_ref, y2_ref, y3_ref)):
        c = jax.nn.sigmoid(gl_ref[:, k * d:(k + 1) * d]) * _dot(y_ref[...], wb_ref[k])
        acc = c if acc is None else acc + c
    x = x_ref[...] + mod_ref[0, 2:3, :] * _dot(acc.astype(BF16), wo_ref[...])
    if final:
        x = x * lax.rsqrt(jnp.mean(x * x, axis=-1, keepdims=True) + EPS) * fg_ref[...]
    o_ref[...] = x


def _merge(ys, p, wb_all, wo_all, layer, x2, mod, final_g, rows_per_seg, final):
    n, d = x2.shape
    tm = _pick(rows_per_seg, (256, 128))
    ysp = pl.BlockSpec((tm, BRANCH_W), lambda i: (i, 0))
    return pl.pallas_call(
        functools.partial(_merge_kernel, final=final),
        grid=(n // tm,),
        in_specs=[ysp, ysp, ysp, ysp,
                  pl.BlockSpec((tm, N_BRANCH * d), lambda i: (i, 0)),
                  pl.BlockSpec((None, N_BRANCH, BRANCH_W, d), lambda i: (layer, 0, 0, 0),
                               pipeline_mode=pl.Buffered(1)),
                  pl.BlockSpec((None, d, d), lambda i: (layer, 0, 0), pipeline_mode=pl.Buffered(1)),
                  pl.BlockSpec((tm, d), lambda i: (i, 0)),
                  pl.BlockSpec((1, 3, d), lambda i: (i * tm // rows_per_seg, 0, 0)),
                  pl.BlockSpec((1, d), lambda i: (0, 0))],
        out_specs=pl.BlockSpec((tm, d), lambda i: (i, 0)),
        out_shape=jax.ShapeDtypeStruct((n, d), F32),
        compiler_params=_cparams(("parallel",), 56),
        name="merge",
    )(*ys, p, wb_all, wo_all, x2, mod, final_g)


def kernel(x, c, ctx, c_ctx, norm_g, w_ada, b_ada, w_in, fn_w, fn_b, dn_conv, dn_a_log, dn_dt_bias,
           dn_norm, hg_lb_logits, hg_norm, da_lambda, da_norm, w_branch, w_out, final_g):
    bsz, t, d = x.shape
    t_ctx = ctx.shape[1]
    depth = w_in.shape[0]
    assert d == N_BRANCH * BRANCH_W and t % CHUNK == 0 and t_ctx % CHUNK == 0 and bsz + 1 <= 8

    w_in_r = jnp.concatenate(
        [w_in[:, :, GATE_W_OFF:].astype(BF16), w_in[:, :, :AB_OFF].astype(BF16),
         w_in[:, :, AB_OFF + 4 * HEADS:GATE_W_OFF].astype(BF16),
         w_in[:, :, AB_OFF:AB_OFF + 4 * HEADS].astype(BF16),
         jnp.zeros((depth, d, 2 * HD - 4 * HEADS), BF16)], axis=-1)
    wb16, wo16, fnw16 = w_branch.astype(BF16), w_out.astype(BF16), fn_w.astype(BF16)

    c8 = jnp.concatenate([c, c_ctx[None, :], jnp.zeros((8 - bsz - 1, d), F32)], axis=0)
    mod = _ada(c8, w_ada, b_ada).reshape(depth, 8, 3, d)

    lb_all = jnp.cumsum(jax.nn.softmax(hg_lb_logits.astype(F32), axis=1), axis=1)
    lb_all = lb_all - lb_all[:, :1]
    log_lb, log_1m_lb, one_m_lb = jnp.log(lb_all), jnp.log1p(-lb_all), 1.0 - lb_all

    c_ch, s_ch = _dft_cos_sin(FN_GW)
    cs_ch = jnp.concatenate([c_ch, s_ch], axis=-1).astype(BF16)
    dft = {}
    for tt in (t, t_ctx):
        ct, st = _dft_cos_sin(tt)
        dft[tt] = (ct.astype(BF16), (-st).astype(BF16))
    rope_l = _rope_tables(t)
    rope_c = tuple(a[:t_ctx] for a in rope_l)

    xl = x.reshape(bsz * t, d)
    xc = ctx.reshape(bsz * t_ctx, d)
    zstate = jnp.zeros((bsz, HEADS, HD, HD), F32)
    for l in range(depth):
        last = l == depth - 1
        g_l = norm_g[l][None, :]
        mod_l, mod_c = mod[l, :bsz], mod[l, bsz:bsz + 1]
        pl_ = _proj(xl, mod_l, g_l, w_in_r, l, t)
        pc_ = _proj(xc, mod_c, g_l, w_in_r, l, bsz * t_ctx)

        fn_b_l = fn_b[l][None, :]
        y_fn_l = _fourier_positions(_fourier_channels(pl_, cs_ch), pl_, *dft[t], fnw16, fn_b_l, l, t)

        dn_n, hg_n, da_n = dn_norm[l][None, :], hg_norm[l][None, :], da_norm[l][None, :]
        y_dn_c, s_f, s_b = _deltanet(pc_, dn_conv, l, dn_a_log[l], dn_dt_bias[l], dn_n,
                                     zstate, zstate, t_ctx)
        y_dn_l, _, _ = _deltanet(pl_, dn_conv, l, dn_a_log[l], dn_dt_bias[l], dn_n, s_f, s_b, t)

        lbs = (log_lb[:, l], log_1m_lb[:, l], one_m_lb[:, l])
        y_hg_c, h_f, h_b = _hgrn2(pc_, *lbs, hg_n, zstate, zstate, t_ctx)
        y_hg_l, _, _ = _hgrn2(pl_, *lbs, hg_n, h_f, h_b, t)

        lam_init = 0.8 - 0.6 * math.exp(-0.3 * l)
        lp = da_lambda[l].astype(F32)
        lam = (jnp.exp(jnp.sum(lp[0] * lp[1])) - jnp.exp(jnp.sum(lp[2] * lp[3])) + lam_init).reshape(1)
        y_da_l = _diff_attention(lam, pl_, pl_, pc_, rope_l, da_n, t, t_ctx, 1.0 - lam_init, True)

        fg = final_g[None, :]
        new_xl = _merge((y_fn_l, y_dn_l, y_hg_l, y_da_l), pl_, wb16, wo16, l, xl, mod_l, fg, t, last)
        if not last:
            y_fn_c = _fourier_positions(_fourier_channels(pc_, cs_ch), pc_, *dft[t_ctx], fnw16,
                                        fn_b_l, l, t_ctx)
            y_da_c = _diff_attention(lam, pc_, pc_, pc_, rope_c, da_n, t_ctx, t_ctx,
                                     1.0 - lam_init, False)
            xc = _merge((y_fn_c, y_dn_c, y_hg_c, y_da_c), pc_, wb16, wo16, l, xc, mod_c, fg,
                        bsz * t_ctx, False)
        xl = new_xl
    return xl.reshape(bsz, t, d)
```

```python
import functools
import math

import jax
import jax.numpy as jnp
from jax import lax
from jax.experimental import pallas as pl
from jax.experimental.pallas import tpu as pltpu

F32 = jnp.float32
BF16 = jnp.bfloat16

EPS = 1e-6
N_BRANCH = 4
HEADS = 4
HD = 128
BRANCH_W = HEADS * HD
FN_GW = 128
CHUNK = 64
SUB = 16
HALF = SUB // 2
GRID_W = 64
ROPE_THETA = 10000.0
DA_DH = 64
LOG2E = 1.4426950408889634

COL = dict(fn_u=64, fn_z=68, dn_q=72, dn_k=76, dn_v=80, dn_z=84, hg_q=88, hg_f0=92, hg_f1=96,
           hg_i=100, hg_z=104, da_q=108, da_k=112, da_v=116, da_z=120, dn_ab=124)
PROJ_W = 126 * 128
GATE_W_OFF = 7696
AB_OFF = 3072

MIB = 1024 * 1024


def _cparams(sem, vmem_mib):
    return pltpu.CompilerParams(dimension_semantics=sem, vmem_limit_bytes=vmem_mib * MIB)


def _silu(x):
    return x * jax.nn.sigmoid(x)


def _dot(a, b):
    return jnp.dot(a, b, preferred_element_type=F32)


def _dot_nt(a, b):
    return lax.dot_general(a, b, (((1,), (1,)), ((), ())), preferred_element_type=F32)


def _dot_tn(a, b):
    return lax.dot_general(a, b, (((0,), (0,)), ((), ())), preferred_element_type=F32)


def _split2(x):
    hi = x.astype(BF16)
    return hi, (x - hi.astype(F32)).astype(BF16)


def _split3(x):
    hi = x.astype(BF16)
    r = x - hi.astype(F32)
    mid = r.astype(BF16)
    return hi, mid, (r - mid.astype(F32)).astype(BF16)


def _dot3(a, b):
    a_hi, a_lo = _split2(a)
    b_hi, b_lo = _split2(b)
    return _dot(a_hi, b_hi) + (_dot(a_hi, b_lo) + _dot(a_lo, b_hi))


def _mask_dot(mask, x):
    hi, mid, lo = _split3(x)
    return _dot(mask, hi) + (_dot(mask, mid) + _dot(mask, lo))


def _dot_mask(x, mask):
    hi, mid, lo = _split3(x)
    return _dot(hi, mask) + (_dot(mid, mask) + _dot(lo, mask))


def _pick(n, cands):
    for c in cands:
        if n % c == 0:
            return c
    return n


def _ada_kernel(c_ref, w_ref, b_ref, o_ref):
    sc = _silu(c_ref[...])
    o_ref[0] = _dot(sc.astype(BF16), w_ref[0].astype(BF16)) + b_ref[0]


def _ada(c8, w_ada, b_ada):
    depth, d, d3 = w_ada.shape
    tn = _pick(d3, (1536, 768, 512, 256, 128))
    return pl.pallas_call(
        _ada_kernel,
        grid=(depth, d3 // tn),
        in_specs=[pl.BlockSpec((8, d), lambda l, n: (0, 0)),
                  pl.BlockSpec((1, d, tn), lambda l, n: (l, 0, n)),
                  pl.BlockSpec((1, 1, tn), lambda l, n: (l, 0, n))],
        out_specs=pl.BlockSpec((1, 8, tn), lambda l, n: (l, 0, n)),
        out_shape=jax.ShapeDtypeStruct((depth, 8, d3), F32),
        compiler_params=_cparams(("parallel", "parallel"), 48),
        name="adaln_mod",
    )(c8, w_ada, b_ada.reshape(depth, 1, d3))


def _reorder_kernel(w_ref, o_ref):
    gate_w = w_ref.shape[1] - GATE_W_OFF
    n_ab = 4 * HEADS
    rest_w = GATE_W_OFF - AB_OFF - n_ab
    ab0 = gate_w + AB_OFF + rest_w
    o_ref[:, 0:gate_w] = w_ref[:, GATE_W_OFF:].astype(BF16)
    o_ref[:, gate_w:gate_w + AB_OFF] = w_ref[:, 0:AB_OFF].astype(BF16)
    o_ref[:, gate_w + AB_OFF:ab0] = w_ref[:, AB_OFF + n_ab:GATE_W_OFF].astype(BF16)
    o_ref[:, ab0:ab0 + n_ab] = w_ref[:, AB_OFF:AB_OFF + n_ab].astype(BF16)
    o_ref[:, ab0 + n_ab:] = jnp.zeros((o_ref.shape[0], o_ref.shape[1] - ab0 - n_ab), BF16)


def _reorder_w_in(w_in):
    depth, d, w = w_in.shape
    assert w - GATE_W_OFF + GATE_W_OFF - 4 * HEADS == COL["dn_ab"] * HD
    tr = _pick(d, (128, 64, 32, 16))
    return pl.pallas_call(
        _reorder_kernel,
        grid=(depth, d // tr),
        in_specs=[pl.BlockSpec((None, tr, w), lambda l, i: (l, i, 0))],
        out_specs=pl.BlockSpec((None, tr, PROJ_W), lambda l, i: (l, i, 0)),
        out_shape=jax.ShapeDtypeStruct((depth, d, PROJ_W), BF16),
        compiler_params=_cparams(("parallel", "parallel"), 40),
        name="w_in_layout",
    )(w_in)


def _proj_kernel(x_ref, mod_ref, g_ref, w_ref, o_ref, h_ref):
    @pl.when(pl.program_id(1) == 0)
    def _():
        x = x_ref[...]
        y = x * lax.rsqrt(jnp.mean(x * x, axis=-1, keepdims=True) + EPS) * g_ref[...]
        h = y * (1.0 + mod_ref[0, 1:2, :]) + mod_ref[0, 0:1, :]
        h_ref[...] = h.astype(BF16)

    o_ref[...] = _dot(h_ref[...], w_ref[...])


def _proj(x2, mod, g, w_all, layer, rows_per_seg):
    n, d = x2.shape
    tm = _pick(rows_per_seg, (1024, 512, 256, 128))
    tn = 768
    return pl.pallas_call(
        _proj_kernel,
        grid=(n // tm, PROJ_W // tn),
        in_specs=[pl.BlockSpec((tm, d), lambda i, j: (i, 0)),
                  pl.BlockSpec((1, 3, d), lambda i, j: (i * tm // rows_per_seg, 0, 0)),
                  pl.BlockSpec((1, d), lambda i, j: (0, 0)),
                  pl.BlockSpec((None, d, tn), lambda i, j: (layer, 0, j))],
        out_specs=pl.BlockSpec((tm, tn), lambda i, j: (i, j)),
        out_shape=jax.ShapeDtypeStruct((n, PROJ_W), F32),
        scratch_shapes=[pltpu.VMEM((tm, d), BF16)],
        compiler_params=_cparams(("parallel", "arbitrary"), 48),
        name="in_proj",
    )(x2, mod, g, w_all)


def _dft_cos_sin(n):
    j = jnp.arange(n, dtype=jnp.int32)
    sc = n ** -0.5
    if n <= 1024:
        ang = (2.0 * math.pi / n) * ((j[:, None] * j[None, :]) % n).astype(F32)
        return jnp.cos(ang) * sc, jnp.sin(ang) * sc
    m = n // 64
    k1 = jnp.arange(m, dtype=jnp.int32)
    k2 = jnp.arange(64, dtype=jnp.int32)
    a = (2.0 * math.pi / m) * ((j[:, None] * k1[None, :]) % m).astype(F32)
    b = (2.0 * math.pi / n) * ((j[:, None] * k2[None, :]) % n).astype(F32)
    ca, sa, cb, sb = jnp.cos(a), jnp.sin(a), jnp.cos(b), jnp.sin(b)
    c = ca[:, :, None] * cb[:, None, :] - sa[:, :, None] * sb[:, None, :]
    s = sa[:, :, None] * cb[:, None, :] + ca[:, :, None] * sb[:, None, :]
    return c.reshape(n, n) * sc, s.reshape(n, n) * sc


def _fn1_kernel(u_ref, cs_ref, o_ref):
    for g in range(BRANCH_W // FN_GW):
        r = _dot(u_ref[:, g * FN_GW:(g + 1) * FN_GW].astype(BF16), cs_ref[...])
        o_ref[:, g * FN_GW:(g + 1) * FN_GW] = r[:, :FN_GW].astype(BF16)
        o_ref[:, BRANCH_W + g * FN_GW:BRANCH_W + (g + 1) * FN_GW] = r[:, FN_GW:].astype(BF16)


def _fourier_channels(p, cs_ch):
    n = p.shape[0]
    tm = _pick(n, (1024, 512, 256, 128))
    return pl.pallas_call(
        _fn1_kernel,
        grid=(n // tm,),
        in_specs=[pl.BlockSpec((tm, BRANCH_W), lambda i: (i, COL["fn_u"] // 4)),
                  pl.BlockSpec((FN_GW, 2 * FN_GW), lambda i: (0, 0))],
        out_specs=pl.BlockSpec((tm, 2 * BRANCH_W), lambda i: (i, 0)),
        out_shape=jax.ShapeDtypeStruct((n, 2 * BRANCH_W), BF16),
        compiler_params=_cparams(("parallel",), 32),
        name="fourier_channels",
    )(p, cs_ch)


def _fn2_kernel(ct_ref, st_ref, r_ref, z_ref, w_ref, b_ref, y_ref):
    f = _dot(ct_ref[...], r_ref[:, :BRANCH_W]) + _dot(st_ref[...], r_ref[:, BRANCH_W:])
    y = _dot(f.astype(BF16), w_ref[...]) + b_ref[...]
    y_ref[...] = (y * _silu(z_ref[...])).astype(BF16)


def _fourier_positions(r, p, ct, nst, fn_w_all, fn_b, layer, t):
    n = p.shape[0]
    bsz = n // t
    tm = _pick(t, (512, 256, 128))
    nt = t // tm
    return pl.pallas_call(
        _fn2_kernel,
        grid=(bsz, nt),
        in_specs=[pl.BlockSpec((tm, t), lambda b, i: (i, 0)),
                  pl.BlockSpec((tm, t), lambda b, i: (i, 0)),
                  pl.BlockSpec((t, 2 * BRANCH_W), lambda b, i: (b, 0)),
                  pl.BlockSpec((tm, BRANCH_W), lambda b, i: (b * nt + i, COL["fn_z"] // 4)),
                  pl.BlockSpec((None, BRANCH_W, BRANCH_W), lambda b, i: (layer, 0, 0)),
                  pl.BlockSpec((1, BRANCH_W), lambda b, i: (0, 0))],
        out_specs=pl.BlockSpec((tm, BRANCH_W), lambda b, i: (b * nt + i, 0)),
        out_shape=jax.ShapeDtypeStruct((n, BRANCH_W), BF16),
        compiler_params=_cparams(("parallel", "arbitrary"), 48),
        name="fourier_positions",
    )(ct, nst, r, p, fn_w_all, fn_b)


def _seq_spec(rows, index_map, buffers=2):
    return pl.BlockSpec((rows, HD), index_map, pipeline_mode=pl.Buffered(buffers))


def _chunk_masks():
    i = lax.broadcasted_iota(jnp.int32, (CHUNK, CHUNK), 0)
    j = lax.broadcasted_iota(jnp.int32, (CHUNK, CHUNK), 1)
    incl = (i >= j, i <= j)
    strict = (i > j, i < j)
    return incl, strict


def _as_bf16_mask(m):
    return jnp.where(m, 1.0, 0.0).astype(BF16)


def _softplus(x):
    return jnp.maximum(x, 0.0) + jnp.log1p(jnp.exp(-jnp.abs(x)))


def _head_rms_gate(o, g_ref, z_ref):
    y = o * lax.rsqrt(jnp.mean(o * o, axis=-1, keepdims=True) + EPS) * g_ref[...]
    return y * _silu(z_ref[...])


def _dn_kernel(alog_ref, dtb_ref, q_ref, k_ref, v_ref, z_ref, ab_ref, cq_ref, ck_ref, cv_ref,
               ng_ref, s0f_ref, s0b_ref, y_ref, sf_ref, sb_ref,
               qs, ks, vs, m_s, n_s, qp_s, gl_s, o_s, *, group):
    h = pl.program_id(1)
    t = q_ref.shape[0]
    nchunk = t // CHUNK
    row = lax.broadcasted_iota(jnp.int32, (t, HD), 0)

    def conv(x_ref, c_ref):
        x = x_ref[...]
        xm = jnp.where(row == 0, 0.0, pltpu.roll(x, 1, 0))
        xp = jnp.where(row == t - 1, 0.0, pltpu.roll(x, t - 1, 0))
        y = xm * c_ref[0:1, :] + x * c_ref[1:2, :] + xp * c_ref[2:3, :]
        return _silu(y)

    q = conv(q_ref, cq_ref)
    qs[...] = q * lax.rsqrt(jnp.sum(q * q, axis=-1, keepdims=True) + EPS) * HD ** -0.5
    k = conv(k_ref, ck_ref)
    ks[...] = k * lax.rsqrt(jnp.sum(k * k, axis=-1, keepdims=True) + EPS)
    vs[...] = conv(v_ref, cv_ref)

    incl, _ = _chunk_masks()
    incl_b = tuple(_as_bf16_mask(m) for m in incl)
    ri = lax.broadcasted_iota(jnp.int32, (CHUNK, HD), 0)
    li = lax.broadcasted_iota(jnp.int32, (CHUNK, HD), 1)
    cj = li % CHUNK
    incl2 = (ri >= cj, ri <= cj)
    strict2 = (ri > cj, ri < cj)
    right = li >= CHUNK
    eye_right = jnp.where(li == ri + CHUNK, 1.0, 0.0)
    sel_r = lax.broadcasted_iota(jnp.int32, (HD, 4 * HD), 0)
    sel_c = lax.broadcasted_iota(jnp.int32, (HD, 4 * HD), 1)
    sel = _as_bf16_mask(sel_r == (sel_c // HD) * HEADS + h)
    neg_a = [-jnp.exp(jnp.full((1, HD), alog_ref[d, h], F32)) for d in range(2)]
    dtb = [dtb_ref[d, h] for d in range(2)]

    def prep(g, carry):
        r0 = pl.multiple_of(g * (group * CHUNK), group * CHUNK)
        rows_g = pl.ds(r0, group * CHUNK)
        ab = _dot_mask(ab_ref[rows_g, :], sel)
        gb_all = [neg_a[d] * _softplus(ab[:, d * HD:(d + 1) * HD] + dtb[d]) for d in range(2)]
        bt_all = [jax.nn.sigmoid(ab[:, (2 + d) * HD:(3 + d) * HD]) for d in range(2)]
        q_g, k_g, v_g = qs[rows_g, :], ks[rows_g, :], vs[rows_g, :]
        chains = [(c, d) for c in range(group) for d in range(2)]
        sl = lambda c: slice(c * CHUNK, (c + 1) * CHUNK)
        kcbs = [k_g[sl(c)].astype(BF16) for c in range(group)]
        kk2 = [_dot_nt(kb_, jnp.concatenate([kb_, kb_], axis=0)) for kb_ in kcbs]
        qk = [_dot_nt(q_g[sl(c)].astype(BF16), kcbs[c]) for c in range(group)]
        gcs = [_mask_dot(incl_b[d], gb_all[d][sl(c)]) for c, d in chains]
        grs = [jnp.concatenate([gc, gc], axis=0).T[:CHUNK, :] for gc in gcs]
        decays = [jnp.where(incl2[d], jnp.exp(jnp.where(incl2[d], gc - gr, 0.0)), 0.0)
                  for (c, d), gc, gr in zip(chains, gcs, grs)]
        zs = [jnp.where(strict2[d] & ~right, -(kk2[c] * bt_all[d][sl(c)] * dec), 0.0) + eye_right
              for (c, d), dec in zip(chains, decays)]
        for _ in range(6):
            zs = [_dot3(z[:, :CHUNK], z) + jnp.where(right, z, 0.0) for z in zs]
        zero_rows = jnp.zeros((CHUNK, 2 * HD), BF16)
        egs = [jnp.exp(gc) for gc in gcs]
        g_lasts = [gc[CHUNK - 1:CHUNK, :] if d == 0 else gc[0:1, :] for (c, d), gc in zip(chains, gcs)]
        wus = [_dot(z.astype(BF16), jnp.concatenate(
                   [zero_rows,
                    jnp.concatenate([(k_g[sl(c)] * bt_all[d][sl(c)] * eg).astype(BF16),
                                     (v_g[sl(c)] * bt_all[d][sl(c)]).astype(BF16)], axis=1)],
                   axis=0)).astype(BF16)
               for (c, d), z, eg in zip(chains, zs, egs)]
        mns = [_dot_tn((k_g[sl(c)] * jnp.exp(gl - gc)).astype(BF16), wu)
               for (c, d), gc, gl, wu in zip(chains, gcs, g_lasts, wus)]
        qos = [_dot(jnp.where(incl[d], qk[c] * dec[:, :CHUNK], 0.0).astype(BF16), wu)
               for (c, d), dec, wu in zip(chains, decays, wus)]
        for (c, d), eg, gl, mn, qo in zip(chains, egs, g_lasts, mns, qos):
            rows = pl.ds(r0 + c * CHUNK, CHUNK)
            mrows = pl.ds(pl.multiple_of((g * group + c) * HD, HD), HD)
            m_s[d, mrows, :] = mn[:, :HD].astype(BF16)
            n_s[d, mrows, :] = mn[:, HD:]
            qp_s[d, rows, :] = (q_g[sl(c)] * eg - qo[:, :HD]).astype(BF16)
            o_s[d, rows, :] = qo[:, HD:]
            gl_rows = pl.ds(pl.multiple_of((g * group + c) * 8, 8), 8)
            gl_s[d, gl_rows, :] = jnp.broadcast_to(jnp.exp(gl), (8, HD))
        return carry

    lax.fori_loop(0, nchunk // group, prep, 0)

    def step(d, ci, s):
        rows = pl.ds(pl.multiple_of(ci * CHUNK, CHUNK), CHUNK)
        mrows = pl.ds(pl.multiple_of(ci * HD, HD), HD)
        s_b16 = s.astype(BF16)
        o_s[d, rows, :] = o_s[d, rows, :] + _dot(qp_s[d, rows, :], s_b16)
        g_last = gl_s[d, pl.ds(pl.multiple_of(ci * 8, 8), 8), :][0:1, :]
        return s * g_last + (n_s[d, mrows, :] - _dot(m_s[d, mrows, :], s_b16))

    def body(n, carry):
        return step(0, n, carry[0]), step(1, nchunk - 1 - n, carry[1])

    s_f, s_b = lax.fori_loop(0, nchunk, body, (s0f_ref[0, 0], s0b_ref[0, 0]))
    sf_ref[0, 0] = s_f
    sb_ref[0, 0] = s_b
    y_ref[...] = _head_rms_gate(o_s[0] + o_s[1], ng_ref, z_ref).astype(BF16)


def _deltanet(p, conv_all, layer, a_log, dt_bias, norm_g, s0f, s0b, t):
    n = p.shape[0]
    bsz = n // t
    nchunk = t // CHUNK
    group = _pick(nchunk, (4, 2, 1))
    col = lambda name: (lambda b, h: (b, COL[name] + h))
    state = pl.BlockSpec((1, 1, HD, HD), lambda b, h: (b, h, 0, 0))
    smem = pl.BlockSpec(memory_space=pltpu.SMEM)
    conv = lambda off: pl.BlockSpec((None, 3, HD), lambda b, h: (layer, 0, off + h))
    st_shape = jax.ShapeDtypeStruct((bsz, HEADS, HD, HD), F32)
    seq_f32 = pltpu.VMEM((t, HD), F32)
    return pl.pallas_call(
        functools.partial(_dn_kernel, group=group),
        grid=(bsz, HEADS),
        in_specs=[smem, smem,
                  _seq_spec(t, col("dn_q")), _seq_spec(t, col("dn_k")),
                  _seq_spec(t, col("dn_v")), _seq_spec(t, col("dn_z"), 1),
                  _seq_spec(t, lambda b, h: (b, COL["dn_ab"]), 1),
                  conv(0), conv(HEADS), conv(2 * HEADS),
                  pl.BlockSpec((1, HD), lambda b, h: (0, 0)), state, state],
        out_specs=[pl.BlockSpec((t, HD), lambda b, h: (b, h)), state, state],
        out_shape=[jax.ShapeDtypeStruct((n, BRANCH_W), BF16), st_shape, st_shape],
        scratch_shapes=[seq_f32, seq_f32, seq_f32,
                        pltpu.VMEM((2, nchunk * HD, HD), BF16), pltpu.VMEM((2, nchunk * HD, HD), F32),
                        pltpu.VMEM((2, t, HD), BF16), pltpu.VMEM((2, nchunk * 8, HD), F32),
                        pltpu.VMEM((2, t, HD), F32)],
        compiler_params=_cparams(("parallel", "parallel"), 56),
        name="deltanet",
    )(a_log, dt_bias, p, p, p, p, p, conv_all, conv_all, conv_all, norm_g, s0f, s0b)


def _hg_kernel(q_ref, f0_ref, f1_ref, i_ref, z_ref, llb_ref, l1m_ref, oml_ref, ng_ref, s0f_ref,
               s0b_ref, y_ref, sf_ref, sb_ref, qs, lfs, ks, o_s):
    t = q_ref.shape[0]
    nchunk = t // CHUNK
    nsub = CHUNK // SUB
    qs[...] = _silu(q_ref[...])
    for d, f_ref in enumerate((f0_ref, f1_ref)):
        f = f_ref[...]
        e_f = jnp.exp(-jnp.abs(f))
        lsig = jnp.minimum(f, 0.0) - jnp.log1p(e_f)
        a = llb_ref[d:d + 1, :]
        b = l1m_ref[d:d + 1, :] + lsig
        lfs[d] = jnp.maximum(a, b) + jnp.log1p(jnp.exp(-jnp.abs(a - b)))
        ks[d] = oml_ref[d:d + 1, :] * (jnp.where(f >= 0.0, e_f, 1.0) * (1.0 / (1.0 + e_f)))

    incl, _ = _chunk_masks()
    incl_b = tuple(_as_bf16_mask(m) for m in incl)
    sub_i = lax.broadcasted_iota(jnp.int32, (SUB, HD), 0)
    half_i = lax.broadcasted_iota(jnp.int32, (HALF, HD), 0)

    def factored(q_rows, g_rows, k_rows, gk_rows, ref, q_keep=None):
        q_t = q_rows * jnp.exp(jnp.minimum(g_rows - ref, 0.0))
        k_t = k_rows * jnp.exp(jnp.minimum(ref - gk_rows, 0.0))
        if q_keep is not None:
            q_t, k_t = jnp.where(q_keep, q_t, 0.0), jnp.where(q_keep, 0.0, k_t)
        return q_t.astype(BF16), k_t.astype(BF16)

    def body(n, carry):
        chains = []
        for u in range(per_trip):
            ci = n * per_trip + u
            chains.append((0, pl.multiple_of(ci * CHUNK, CHUNK)))
            chains.append((1, pl.multiple_of((nchunk - 1 - ci) * CHUNK, CHUNK)))
        state, o_inter, pair_acc, jobs = list(carry), [], [], []
        for cidx, (d, c0) in enumerate(chains):
            s_t = state[d]
            qc = qs[pl.ds(c0, CHUNK), :]
            kc = ks[d, pl.ds(c0, CHUNK), :]
            vc = i_ref[pl.ds(c0, CHUNK), :]
            vb = vc.astype(BF16)
            gc = _mask_dot(incl_b[d], lfs[d, pl.ds(c0, CHUNK), :])
            g_last = gc[CHUNK - 1:CHUNK, :] if d == 0 else gc[0:1, :]
            o_inter.append(_dot_nt((qc * jnp.exp(gc)).astype(BF16), s_t.astype(BF16)))
            k_dec = kc * jnp.exp(g_last - gc)
            state[d] = s_t * jnp.exp(g_last) + _dot_tn(vb, k_dec.astype(BF16))
            for ib in range(nsub):
                r0 = ib * SUB
                blk = slice(r0, r0 + SUB)
                mid = r0 + HALF
                ref = gc[mid - 1:mid, :] if d == 0 else gc[mid:mid + 1, :]
                q_keep = (sub_i >= HALF) if d == 0 else (sub_i < HALF)
                jobs.append((cidx, ib, factored(qc[blk], gc[blk], kc[blk], gc[blk], ref, q_keep),
                             vb[blk]))
                if d == 0 and ib > 0:
                    rng, ref = slice(0, r0), gc[r0 - 1:r0, :]
                elif d == 1 and ib < nsub - 1:
                    rng, ref = slice(r0 + SUB, CHUNK), gc[r0 + SUB:r0 + SUB + 1, :]
                else:
                    continue
                jobs.append((cidx, ib, factored(qc[blk], gc[blk], kc[rng], gc[rng], ref), vb[rng]))
            halves = []
            for h0 in range(0, CHUNK, HALF):
                q_h, g_h = qc[h0:h0 + HALF, :], gc[h0:h0 + HALF, :]
                acc_h = jnp.zeros((HALF, HD), F32)
                for jj in range(HALF):
                    j = h0 + jj
                    term = q_h * kc[j:j + 1, :] * jnp.exp(jnp.minimum(g_h - gc[j:j + 1, :], 0.0))
                    keep = (half_i >= jj) if d == 0 else (half_i <= jj)
                    acc_h = acc_h + (jnp.sum(jnp.where(keep, term, 0.0), axis=-1, keepdims=True)
                                     * vc[j:j + 1, :])
                halves.append(acc_h)
            pair_acc.append(halves)
        scores = [_dot_nt(q_t, k_t).astype(BF16) for _, _, (q_t, k_t), _ in jobs]
        outs = [_dot(a, v_rows) for a, (_, _, _, v_rows) in zip(scores, jobs)]
        for cidx, (d, c0) in enumerate(chains):
            blocks = []
            for ib in range(nsub):
                acc = jnp.concatenate(pair_acc[cidx][2 * ib:2 * ib + 2], axis=0)
                for o_job, (jc, jb, _, _) in zip(outs, jobs):
                    if (jc, jb) == (cidx, ib):
                        acc = acc + o_job
                blocks.append(acc)
            o_s[d, pl.ds(c0, CHUNK), :] = o_inter[cidx] + jnp.concatenate(blocks, axis=0)
        return tuple(state)

    per_trip = 2 if nchunk % 2 == 0 else 1
    s_f, s_b = lax.fori_loop(0, nchunk // per_trip, body, (s0f_ref[0, 0], s0b_ref[0, 0]))
    sf_ref[0, 0] = s_f
    sb_ref[0, 0] = s_b
    y_ref[...] = _head_rms_gate(o_s[0] + o_s[1], ng_ref, z_ref).astype(BF16)


def _hgrn2(p, log_lb, log_1m_lb, one_m_lb, norm_g, s0f, s0b, t):
    n = p.shape[0]
    bsz = n // t
    col = lambda name: (lambda b, h: (b, COL[name] + h))
    state = pl.BlockSpec((1, 1, HD, HD), lambda b, h: (b, h, 0, 0))
    lbs = pl.BlockSpec((2, HD), lambda b, h: (0, h))
    st_shape = jax.ShapeDtypeStruct((bsz, HEADS, HD, HD), F32)
    return pl.pallas_call(
        _hg_kernel,
        grid=(bsz, HEADS),
        in_specs=[_seq_spec(t, col("hg_q")), _seq_spec(t, col("hg_f0")),
                  _seq_spec(t, col("hg_f1")), _seq_spec(t, col("hg_i")),
                  _seq_spec(t, col("hg_z")), lbs, lbs, lbs,
                  pl.BlockSpec((1, HD), lambda b, h: (0, 0)), state, state],
        out_specs=[pl.BlockSpec((t, HD), lambda b, h: (b, h)), state, state],
        out_shape=[jax.ShapeDtypeStruct((n, BRANCH_W), BF16), st_shape, st_shape],
        scratch_shapes=[pltpu.VMEM((t, HD), F32), pltpu.VMEM((2, t, HD), F32),
                        pltpu.VMEM((2, t, HD), F32), pltpu.VMEM((2, t, HD), F32)],
        compiler_params=_cparams(("parallel", "parallel"), 48),
        name="hgrn2",
    )(p, p, p, p, p, log_lb, log_1m_lb, one_m_lb, norm_g, s0f, s0b)


def _rope_tables(t):
    pos = jnp.arange(t)
    row = (pos // GRID_W).astype(F32)
    col = (pos % GRID_W).astype(F32)
    n = DA_DH // 4
    inv = ROPE_THETA ** (-jnp.arange(n, dtype=F32) / n)
    ar, ac = row[:, None] * inv, col[:, None] * inv
    zero = jnp.zeros_like(ar)
    cos = jnp.concatenate([jnp.cos(ar), jnp.cos(ar), jnp.cos(ac), jnp.cos(ac)], axis=-1)
    s_up = jnp.concatenate([zero, jnp.sin(ar), zero, jnp.sin(ac)], axis=-1)
    s_dn = jnp.concatenate([-jnp.sin(ar), zero, -jnp.sin(ac), zero], axis=-1)
    tile = lambda a: jnp.concatenate([a, a], axis=-1)
    return tile(cos), tile(s_up), tile(s_dn)


def _rope(x, cos, s_up, s_dn):
    half = DA_DH // 4
    return x * cos + pltpu.roll(x, half, 1) * s_up + pltpu.roll(x, HD - half, 1) * s_dn


def _da_kernel(lam_ref, q_ref, kl_ref, kc_ref, vl_ref, vc_ref, z_ref, cq_ref, uq_ref, dq_ref,
               ck_ref, uk_ref, dk_ref, ng_ref, y_ref, k_s, vt_s, s_s, *, t_lat, out_scale):
    nkb, kblk, _ = k_s.shape
    nlat = t_lat // kblk

    @pl.when(pl.program_id(2) == 0)
    def _():
        for j in range(nkb):
            if j < nlat:
                rows = slice(j * kblk, (j + 1) * kblk)
                k = _rope(kl_ref[rows, :], ck_ref[rows, :], uk_ref[rows, :], dk_ref[rows, :])
                v = vl_ref[rows, :]
            else:
                rows = slice((j - nlat) * kblk, (j - nlat + 1) * kblk)
                k, v = kc_ref[rows, :], vc_ref[rows, :]
            k_s[j] = k.astype(BF16)
            vt_s[j] = v.T.astype(BF16)

    q = q_ref[...]
    if t_lat:
        q = _rope(q, cq_ref[...], uq_ref[...], dq_ref[...])
    q = q * (DA_DH ** -0.5 * LOG2E)
    lane = lax.broadcasted_iota(jnp.int32, q.shape, 1)
    qms = [jnp.where(lane < DA_DH, q, 0.0).astype(BF16), jnp.where(lane >= DA_DH, q, 0.0).astype(BF16)]
    tq = q.shape[0]
    m = [None, None]
    l = [jnp.zeros((1, tq), F32) for _ in range(2)]
    acc = [jnp.zeros((HD, tq), F32) for _ in range(2)]

    def scores(mp, j):
        s = _dot_nt(k_s[j], qms[mp])
        s_s[mp, j] = s
        bm = jnp.max(s, axis=0, keepdims=True)
        m[mp] = bm if m[mp] is None else jnp.maximum(m[mp], bm)

    def values(mp, j):
        e = jnp.exp2(s_s[mp, j] - m[mp])
        l[mp] = l[mp] + jnp.sum(e, axis=0, keepdims=True)
        acc[mp] = acc[mp] + _dot(vt_s[j], e.astype(BF16))

    for j in range(nkb):
        scores(0, j)
    for j in range(nkb):
        scores(1, j)
        values(0, j)
    for j in range(nkb):
        values(1, j)
    o = (acc[0] * (1.0 / l[0]) - acc[1] * (lam_ref[0] * (1.0 / l[1]))).T
    y = o * lax.rsqrt(jnp.mean(o * o, axis=-1, keepdims=True) + EPS) * ng_ref[...]
    y_ref[...] = (y * out_scale * _silu(z_ref[...])).astype(BF16)


def _diff_attention(lam, p_q, p_lat, p_ctx, rope, norm_g, t, t_ctx, out_scale, latent):
    n = p_q.shape[0]
    bsz = n // t
    tq = _pick(t, (512, 256, 128))
    nq = t // tq
    t_lat = t if latent else 0
    t_kl = t if latent else t_ctx
    cos, s_up, s_dn = rope
    kblk = 256 if (t_lat % 256 == 0 and t_ctx % 256 == 0) else 128
    nkb = (t_lat + t_ctx) // kblk
    col = lambda name: (lambda b, h, i: (b, COL[name] + h))
    qcol = lambda name: (lambda b, h, i: (b * nq + i, COL[name] + h))
    tab_q = pl.BlockSpec((tq, HD), lambda b, h, i: (i, 0))
    tab_k = _seq_spec(t_kl, lambda b, h, i: (0, 0), 1)
    return pl.pallas_call(
        functools.partial(_da_kernel, t_lat=t_lat, out_scale=out_scale),
        grid=(bsz, HEADS, nq),
        in_specs=[pl.BlockSpec(memory_space=pltpu.SMEM),
                  pl.BlockSpec((tq, HD), qcol("da_q")),
                  _seq_spec(t_kl, col("da_k")), _seq_spec(t_ctx, col("da_k")),
                  _seq_spec(t_kl, col("da_v")), _seq_spec(t_ctx, col("da_v")),
                  pl.BlockSpec((tq, HD), qcol("da_z")),
                  tab_q, tab_q, tab_q, tab_k, tab_k, tab_k,
                  pl.BlockSpec((1, HD), lambda b, h, i: (0, 0))],
        out_specs=pl.BlockSpec((tq, HD), lambda b, h, i: (b * nq + i, h)),
        out_shape=jax.ShapeDtypeStruct((n, BRANCH_W), BF16),
        scratch_shapes=[pltpu.VMEM((nkb, kblk, HD), BF16), pltpu.VMEM((nkb, HD, kblk), BF16),
                        pltpu.VMEM((2, nkb, kblk, tq), F32)],
        compiler_params=_cparams(("parallel", "parallel", "arbitrary"), 56),
        name="diff_attention",
    )(lam, p_q, p_lat, p_ctx, p_lat, p_ctx, p_q, cos, s_up, s_dn, cos, s_up, s_dn, norm_g)


def _merge_kernel(y0_ref, y1_ref, y2_ref, y3_ref, gl_ref, wb_ref, wo_ref, x_ref, mod_ref, fg_ref,
                  o_ref, *, final):
    d = x_ref.shape[1]
    acc = None
    for k, y_ref in enumerate((y0_ref, y1_ref, y2_ref, y3_ref)):
        c = jax.nn.sigmoid(gl_ref[:, k * d:(k + 1) * d]) * _dot(y_ref[...], wb_ref[k])
        acc = c if acc is None else acc + c
    x = x_ref[...] + mod_ref[0, 2:3, :] * _dot(acc.astype(BF16), wo_ref[...])
    if final:
        x = x * lax.rsqrt(jnp.mean(x * x, axis=-1, keepdims=True) + EPS) * fg_ref[...]
    o_ref[...] = x


def _merge(ys, p, wb_all, wo_all, layer, x2, mod, final_g, rows_per_seg, final):
    n, d = x2.shape
    tm = _pick(rows_per_seg, (256, 128))
    ysp = pl.BlockSpec((tm, BRANCH_W), lambda i: (i, 0))
    return pl.pallas_call(
        functools.partial(_merge_kernel, final=final),
        grid=(n // tm,),
        in_specs=[ysp, ysp, ysp, ysp,
                  pl.BlockSpec((tm, N_BRANCH * d), lambda i: (i, 0)),
                  pl.BlockSpec((None, N_BRANCH, BRANCH_W, d), lambda i: (layer, 0, 0, 0),
                               pipeline_mode=pl.Buffered(1)),
                  pl.BlockSpec((None, d, d), lambda i: (layer, 0, 0), pipeline_mode=pl.Buffered(1)),
                  pl.BlockSpec((tm, d), lambda i: (i, 0)),
                  pl.BlockSpec((1, 3, d), lambda i: (i * tm // rows_per_seg, 0, 0)),
                  pl.BlockSpec((1, d), lambda i: (0, 0))],
        out_specs=pl.BlockSpec((tm, d), lambda i: (i, 0)),
        out_shape=jax.ShapeDtypeStruct((n, d), F32),
        compiler_params=_cparams(("parallel",), 56),
        name="merge",
    )(*ys, p, wb_all, wo_all, x2, mod, final_g)


def kernel(x, c, ctx, c_ctx, norm_g, w_ada, b_ada, w_in, fn_w, fn_b, dn_conv, dn_a_log, dn_dt_bias,
           dn_norm, hg_lb_logits, hg_norm, da_lambda, da_norm, w_branch, w_out, final_g):
    bsz, t, d = x.shape
    t_ctx = ctx.shape[1]
    depth = w_in.shape[0]
    assert d == N_BRANCH * BRANCH_W and t % CHUNK == 0 and t_ctx % CHUNK == 0 and bsz + 1 <= 8

    w_in_r = _reorder_w_in(w_in)
    wb16, wo16, fnw16 = w_branch.astype(BF16), w_out.astype(BF16), fn_w.astype(BF16)

    c8 = jnp.concatenate([c, c_ctx[None, :], jnp.zeros((8 - bsz - 1, d), F32)], axis=0)
    mod = _ada(c8, w_ada, b_ada).reshape(depth, 8, 3, d)

    lb_all = jnp.cumsum(jax.nn.softmax(hg_lb_logits.astype(F32), axis=1), axis=1)
    lb_all = lb_all - lb_all[:, :1]
    log_lb, log_1m_lb, one_m_lb = jnp.log(lb_all), jnp.log1p(-lb_all), 1.0 - lb_all

    c_ch, s_ch = _dft_cos_sin(FN_GW)
    cs_ch = jnp.concatenate([c_ch, s_ch], axis=-1).astype(BF16)
    dft = {}
    for tt in (t, t_ctx):
        ct, st = _dft_cos_sin(tt)
        dft[tt] = (ct.astype(BF16), (-st).astype(BF16))
    rope_l = _rope_tables(t)
    rope_c = tuple(a[:t_ctx] for a in rope_l)

    xl = x.reshape(bsz * t, d)
    xc = ctx.reshape(bsz * t_ctx, d)
    zstate = jnp.zeros((bsz, HEADS, HD, HD), F32)
    for l in range(depth):
        last = l == depth - 1
        g_l = norm_g[l][None, :]
        mod_l, mod_c = mod[l, :bsz], mod[l, bsz:bsz + 1]
        pl_ = _proj(xl, mod_l, g_l, w_in_r, l, t)
        pc_ = _proj(xc, mod_c, g_l, w_in_r, l, bsz * t_ctx)

        fn_b_l = fn_b[l][None, :]
        y_fn_l = _fourier_positions(_fourier_channels(pl_, cs_ch), pl_, *dft[t], fnw16, fn_b_l, l, t)

        dn_n, hg_n, da_n = dn_norm[l][None, :], hg_norm[l][None, :], da_norm[l][None, :]
        y_dn_c, s_f, s_b = _deltanet(pc_, dn_conv, l, dn_a_log[l], dn_dt_bias[l], dn_n,
                                     zstate, zstate, t_ctx)
        y_dn_l, _, _ = _deltanet(pl_, dn_conv, l, dn_a_log[l], dn_dt_bias[l], dn_n, s_f, s_b, t)

        lbs = (log_lb[:, l], log_1m_lb[:, l], one_m_lb[:, l])
        y_hg_c, h_f, h_b = _hgrn2(pc_, *lbs, hg_n, zstate, zstate, t_ctx)
        y_hg_l, _, _ = _hgrn2(pl_, *lbs, hg_n, h_f, h_b, t)

        lam_init = 0.8 - 0.6 * math.exp(-0.3 * l)
        lp = da_lambda[l].astype(F32)
        lam = (jnp.exp(jnp.sum(lp[0] * lp[1])) - jnp.exp(jnp.sum(lp[2] * lp[3])) + lam_init).reshape(1)
        y_da_l = _diff_attention(lam, pl_, pl_, pc_, rope_l, da_n, t, t_ctx, 1.0 - lam_init, True)

        fg = final_g[None, :]
        new_xl = _merge((y_fn_l, y_dn_l, y_hg_l, y_da_l), pl_, wb16, wo16, l, xl, mod_l, fg, t, last)
        if not last:
            y_fn_c = _fourier_positions(_fourier_channels(pc_, cs_ch), pc_, *dft[t_ctx], fnw16,
                                        fn_b_l, l, t_ctx)
            y_da_c = _diff_attention(lam, pc_, pc_, pc_, rope_c, da_n, t_ctx, t_ctx,
                                     1.0 - lam_init, False)
            xc = _merge((y_fn_c, y_dn_c, y_hg_c, y_da_c), pc_, wb16, wo16, l, xc, mod_c, fg,
                        bsz * t_ctx, False)
        xl = new_xl
    return xl.reshape(bsz, t, d)
```

```python
import functools
import math

import jax
import jax.numpy as jnp
from jax import lax
from jax.experimental import pallas as pl
from jax.experimental.pallas import tpu as pltpu

F32 = jnp.float32
BF16 = jnp.bfloat16

EPS = 1e-6
N_BRANCH = 4
HEADS = 4
HD = 128
BRANCH_W = HEADS * HD
FN_GW = 128
CHUNK = 64
SUB = 16
HALF = SUB // 2
GRID_W = 64
ROPE_THETA = 10000.0
DA_DH = 64
LOG2E = 1.4426950408889634

COL = dict(fn_u=64, fn_z=68, dn_q=72, dn_k=76, dn_v=80, dn_z=84, hg_q=88, hg_f0=92, hg_f1=96,
           hg_i=100, hg_z=104, da_q=108, da_k=112, da_v=116, da_z=120, dn_ab=124)
PROJ_W = 126 * 128
GATE_W_OFF = 7696
AB_OFF = 3072

MIB = 1024 * 1024


def _cparams(sem, vmem_mib):
    return pltpu.CompilerParams(dimension_semantics=sem, vmem_limit_bytes=vmem_mib * MIB)


def _silu(x):
    return x * jax.nn.sigmoid(x)


def _dot(a, b):
    return jnp.dot(a, b, preferred_element_type=F32)


def _dot_nt(a, b):
    return lax.dot_general(a, b, (((1,), (1,)), ((), ())), preferred_element_type=F32)


def _dot_tn(a, b):
    return lax.dot_general(a, b, (((0,), (0,)), ((), ())), preferred_element_type=F32)


def _split2(x):
    hi = x.astype(BF16)
    return hi, (x - hi.astype(F32)).astype(BF16)


def _split3(x):
    hi = x.astype(BF16)
    r = x - hi.astype(F32)
    mid = r.astype(BF16)
    return hi, mid, (r - mid.astype(F32)).astype(BF16)


def _dot3(a, b):
    a_hi, a_lo = _split2(a)
    b_hi, b_lo = _split2(b)
    return _dot(a_hi, b_hi) + (_dot(a_hi, b_lo) + _dot(a_lo, b_hi))


def _mask_dot(mask, x):
    hi, mid, lo = _split3(x)
    return _dot(mask, hi) + (_dot(mask, mid) + _dot(mask, lo))


def _dot_mask(x, mask):
    hi, mid, lo = _split3(x)
    return _dot(hi, mask) + (_dot(mid, mask) + _dot(lo, mask))


def _pick(n, cands):
    for c in cands:
        if n % c == 0:
            return c
    return n


def _ada_kernel(c_ref, w_ref, b_ref, o_ref):
    sc = _silu(c_ref[...])
    o_ref[0] = _dot(sc.astype(BF16), w_ref[0].astype(BF16)) + b_ref[0]


def _ada(c8, w_ada, b_ada):
    depth, d, d3 = w_ada.shape
    tn = _pick(d3, (1536, 768, 512, 256, 128))
    return pl.pallas_call(
        _ada_kernel,
        grid=(depth, d3 // tn),
        in_specs=[pl.BlockSpec((8, d), lambda l, n: (0, 0)),
                  pl.BlockSpec((1, d, tn), lambda l, n: (l, 0, n)),
                  pl.BlockSpec((1, 1, tn), lambda l, n: (l, 0, n))],
        out_specs=pl.BlockSpec((1, 8, tn), lambda l, n: (l, 0, n)),
        out_shape=jax.ShapeDtypeStruct((depth, 8, d3), F32),
        compiler_params=_cparams(("parallel", "parallel"), 48),
        name="adaln_mod",
    )(c8, w_ada, b_ada.reshape(depth, 1, d3))


def _transposed_w_in(w_in):
    depth, d, _ = w_in.shape
    n_ab = 4 * HEADS
    wt = jnp.swapaxes(w_in, 1, 2)
    return jnp.concatenate(
        [wt[:, GATE_W_OFF:], wt[:, :AB_OFF], wt[:, AB_OFF + n_ab:GATE_W_OFF], wt[:, AB_OFF:AB_OFF + n_ab],
         jnp.zeros((depth, 2 * HD - n_ab, d), w_in.dtype)], axis=1).astype(BF16)


def _proj_kernel(x_ref, mod_ref, g_ref, w_ref, o_ref, h_ref):
    @pl.when(pl.program_id(1) == 0)
    def _():
        x = x_ref[...]
        y = x * lax.rsqrt(jnp.mean(x * x, axis=-1, keepdims=True) + EPS) * g_ref[...]
        h = y * (1.0 + mod_ref[0, 1:2, :]) + mod_ref[0, 0:1, :]
        h_ref[...] = h.astype(BF16)

    o_ref[...] = _dot_nt(h_ref[...], w_ref[...])


def _proj(x2, mod, g, wt_all, layer, rows_per_seg):
    n, d = x2.shape
    tm = _pick(rows_per_seg, (1024, 512, 256, 128))
    tn = 1792
    return pl.pallas_call(
        _proj_kernel,
        grid=(n // tm, PROJ_W // tn),
        in_specs=[pl.BlockSpec((tm, d), lambda i, j: (i, 0), pipeline_mode=pl.Buffered(1)),
                  pl.BlockSpec((1, 3, d), lambda i, j: (i * tm // rows_per_seg, 0, 0)),
                  pl.BlockSpec((1, d), lambda i, j: (0, 0)),
                  pl.BlockSpec((None, tn, d), lambda i, j: (layer, j, 0))],
        out_specs=pl.BlockSpec((tm, tn), lambda i, j: (i, j)),
        out_shape=jax.ShapeDtypeStruct((n, PROJ_W), F32),
        scratch_shapes=[pltpu.VMEM((tm, d), BF16)],
        compiler_params=_cparams(("parallel", "arbitrary"), 56),
        name="in_proj",
    )(x2, mod, g, wt_all)


def _dft_cos_sin(n):
    j = jnp.arange(n, dtype=jnp.int32)
    sc = n ** -0.5
    if n <= 1024:
        ang = (2.0 * math.pi / n) * ((j[:, None] * j[None, :]) % n).astype(F32)
        return jnp.cos(ang) * sc, jnp.sin(ang) * sc
    m = n // 64
    k1 = jnp.arange(m, dtype=jnp.int32)
    k2 = jnp.arange(64, dtype=jnp.int32)
    a = (2.0 * math.pi / m) * ((j[:, None] * k1[None, :]) % m).astype(F32)
    b = (2.0 * math.pi / n) * ((j[:, None] * k2[None, :]) % n).astype(F32)
    ca, sa, cb, sb = jnp.cos(a), jnp.sin(a), jnp.cos(b), jnp.sin(b)
    c = ca[:, :, None] * cb[:, None, :] - sa[:, :, None] * sb[:, None, :]
    s = sa[:, :, None] * cb[:, None, :] + ca[:, :, None] * sb[:, None, :]
    return c.reshape(n, n) * sc, s.reshape(n, n) * sc


def _fn1_kernel(u_ref, cs_ref, o_ref):
    for g in range(BRANCH_W // FN_GW):
        r = _dot(u_ref[:, g * FN_GW:(g + 1) * FN_GW].astype(BF16), cs_ref[...])
        o_ref[:, g * FN_GW:(g + 1) * FN_GW] = r[:, :FN_GW].astype(BF16)
        o_ref[:, BRANCH_W + g * FN_GW:BRANCH_W + (g + 1) * FN_GW] = r[:, FN_GW:].astype(BF16)


def _fourier_channels(p, cs_ch):
    n = p.shape[0]
    tm = _pick(n, (1024, 512, 256, 128))
    return pl.pallas_call(
        _fn1_kernel,
        grid=(n // tm,),
        in_specs=[pl.BlockSpec((tm, BRANCH_W), lambda i: (i, COL["fn_u"] // 4)),
                  pl.BlockSpec((FN_GW, 2 * FN_GW), lambda i: (0, 0))],
        out_specs=pl.BlockSpec((tm, 2 * BRANCH_W), lambda i: (i, 0)),
        out_shape=jax.ShapeDtypeStruct((n, 2 * BRANCH_W), BF16),
        compiler_params=_cparams(("parallel",), 32),
        name="fourier_channels",
    )(p, cs_ch)


def _fn2_kernel(ct_ref, st_ref, r_ref, z_ref, w_ref, b_ref, y_ref):
    f = _dot(ct_ref[...], r_ref[:, :BRANCH_W]) + _dot(st_ref[...], r_ref[:, BRANCH_W:])
    y = _dot(f.astype(BF16), w_ref[...]) + b_ref[...]
    y_ref[...] = (y * _silu(z_ref[...])).astype(BF16)


def _fourier_positions(r, p, ct, nst, fn_w_all, fn_b, layer, t):
    n = p.shape[0]
    bsz = n // t
    tm = _pick(t, (512, 256, 128))
    nt = t // tm
    return pl.pallas_call(
        _fn2_kernel,
        grid=(bsz, nt),
        in_specs=[pl.BlockSpec((tm, t), lambda b, i: (i, 0)),
                  pl.BlockSpec((tm, t), lambda b, i: (i, 0)),
                  pl.BlockSpec((t, 2 * BRANCH_W), lambda b, i: (b, 0)),
                  pl.BlockSpec((tm, BRANCH_W), lambda b, i: (b * nt + i, COL["fn_z"] // 4)),
                  pl.BlockSpec((None, BRANCH_W, BRANCH_W), lambda b, i: (layer, 0, 0)),
                  pl.BlockSpec((1, BRANCH_W), lambda b, i: (0, 0))],
        out_specs=pl.BlockSpec((tm, BRANCH_W), lambda b, i: (b * nt + i, 0)),
        out_shape=jax.ShapeDtypeStruct((n, BRANCH_W), BF16),
        compiler_params=_cparams(("parallel", "arbitrary"), 48),
        name="fourier_positions",
    )(ct, nst, r, p, fn_w_all, fn_b)


def _seq_spec(rows, index_map, buffers=2):
    return pl.BlockSpec((rows, HD), index_map, pipeline_mode=pl.Buffered(buffers))


def _chunk_masks():
    i = lax.broadcasted_iota(jnp.int32, (CHUNK, CHUNK), 0)
    j = lax.broadcasted_iota(jnp.int32, (CHUNK, CHUNK), 1)
    incl = (i >= j, i <= j)
    strict = (i > j, i < j)
    return incl, strict


def _as_bf16_mask(m):
    return jnp.where(m, 1.0, 0.0).astype(BF16)


def _softplus(x):
    return jnp.maximum(x, 0.0) + jnp.log1p(jnp.exp(-jnp.abs(x)))


def _head_rms_gate(o, g_ref, z_ref):
    y = o * lax.rsqrt(jnp.mean(o * o, axis=-1, keepdims=True) + EPS) * g_ref[...]
    return y * _silu(z_ref[...])


def _dn_kernel(alog_ref, dtb_ref, q_ref, k_ref, v_ref, z_ref, ab_ref, cq_ref, ck_ref, cv_ref,
               ng_ref, s0f_ref, s0b_ref, y_ref, sf_ref, sb_ref,
               qs, ks, vs, m_s, n_s, qp_s, gl_s, o_s, *, group):
    h = pl.program_id(1)
    t = q_ref.shape[0]
    nchunk = t // CHUNK
    row = lax.broadcasted_iota(jnp.int32, (t, HD), 0)

    def conv(x_ref, c_ref):
        x = x_ref[...]
        xm = jnp.where(row == 0, 0.0, pltpu.roll(x, 1, 0))
        xp = jnp.where(row == t - 1, 0.0, pltpu.roll(x, t - 1, 0))
        y = xm * c_ref[0:1, :] + x * c_ref[1:2, :] + xp * c_ref[2:3, :]
        return _silu(y)

    q = conv(q_ref, cq_ref)
    qs[...] = q * lax.rsqrt(jnp.sum(q * q, axis=-1, keepdims=True) + EPS) * HD ** -0.5
    k = conv(k_ref, ck_ref)
    ks[...] = k * lax.rsqrt(jnp.sum(k * k, axis=-1, keepdims=True) + EPS)
    vs[...] = conv(v_ref, cv_ref)

    incl, _ = _chunk_masks()
    incl_b = tuple(_as_bf16_mask(m) for m in incl)
    ri = lax.broadcasted_iota(jnp.int32, (CHUNK, HD), 0)
    li = lax.broadcasted_iota(jnp.int32, (CHUNK, HD), 1)
    cj = li % CHUNK
    incl2 = (ri >= cj, ri <= cj)
    strict2 = (ri > cj, ri < cj)
    right = li >= CHUNK
    eye_right = jnp.where(li == ri + CHUNK, 1.0, 0.0)
    sel_r = lax.broadcasted_iota(jnp.int32, (HD, 4 * HD), 0)
    sel_c = lax.broadcasted_iota(jnp.int32, (HD, 4 * HD), 1)
    sel = _as_bf16_mask(sel_r == (sel_c // HD) * HEADS + h)
    neg_a = [-jnp.exp(jnp.full((1, HD), alog_ref[d, h], F32)) for d in range(2)]
    dtb = [dtb_ref[d, h] for d in range(2)]

    def prep(g, carry):
        r0 = pl.multiple_of(g * (group * CHUNK), group * CHUNK)
        rows_g = pl.ds(r0, group * CHUNK)
        ab = _dot_mask(ab_ref[rows_g, :], sel)
        gb_all = [neg_a[d] * _softplus(ab[:, d * HD:(d + 1) * HD] + dtb[d]) for d in range(2)]
        bt_all = [jax.nn.sigmoid(ab[:, (2 + d) * HD:(3 + d) * HD]) for d in range(2)]
        q_g, k_g, v_g = qs[rows_g, :], ks[rows_g, :], vs[rows_g, :]
        chains = [(c, d) for c in range(group) for d in range(2)]
        sl = lambda c: slice(c * CHUNK, (c + 1) * CHUNK)
        kcbs = [k_g[sl(c)].astype(BF16) for c in range(group)]
        kk2 = [_dot_nt(kb_, jnp.concatenate([kb_, kb_], axis=0)) for kb_ in kcbs]
        qk = [_dot_nt(q_g[sl(c)].astype(BF16), kcbs[c]) for c in range(group)]
        gcs = [_mask_dot(incl_b[d], gb_all[d][sl(c)]) for c, d in chains]
        grs = [jnp.concatenate([gc, gc], axis=0).T[:CHUNK, :] for gc in gcs]
        decays = [jnp.where(incl2[d], jnp.exp(jnp.where(incl2[d], gc - gr, 0.0)), 0.0)
                  for (c, d), gc, gr in zip(chains, gcs, grs)]
        zs = [jnp.where(strict2[d] & ~right, -(kk2[c] * bt_all[d][sl(c)] * dec), 0.0) + eye_right
              for (c, d), dec in zip(chains, decays)]
        for _ in range(6):
            zs = [_dot3(z[:, :CHUNK], z) + jnp.where(right, z, 0.0) for z in zs]
        zero_rows = jnp.zeros((CHUNK, 2 * HD), BF16)
        egs = [jnp.exp(gc) for gc in gcs]
        g_lasts = [gc[CHUNK - 1:CHUNK, :] if d == 0 else gc[0:1, :] for (c, d), gc in zip(chains, gcs)]
        wus = [_dot(z.astype(BF16), jnp.concatenate(
                   [zero_rows,
                    jnp.concatenate([(k_g[sl(c)] * bt_all[d][sl(c)] * eg).astype(BF16),
                                     (v_g[sl(c)] * bt_all[d][sl(c)]).astype(BF16)], axis=1)],
                   axis=0)).astype(BF16)
               for (c, d), z, eg in zip(chains, zs, egs)]
        mns = [_dot_tn((k_g[sl(c)] * jnp.exp(gl - gc)).astype(BF16), wu)
               for (c, d), gc, gl, wu in zip(chains, gcs, g_lasts, wus)]
        qos = [_dot(jnp.where(incl[d], qk[c] * dec[:, :CHUNK], 0.0).astype(BF16), wu)
               for (c, d), dec, wu in zip(chains, decays, wus)]
        for (c, d), eg, gl, mn, qo in zip(chains, egs, g_lasts, mns, qos):
            rows = pl.ds(r0 + c * CHUNK, CHUNK)
            mrows = pl.ds(pl.multiple_of((g * group + c) * HD, HD), HD)
            m_s[d, mrows, :] = mn[:, :HD].astype(BF16)
            n_s[d, mrows, :] = mn[:, HD:]
            qp_s[d, rows, :] = (q_g[sl(c)] * eg - qo[:, :HD]).astype(BF16)
            o_s[d, rows, :] = qo[:, HD:]
            gl_rows = pl.ds(pl.multiple_of((g * group + c) * 8, 8), 8)
            gl_s[d, gl_rows, :] = jnp.broadcast_to(jnp.exp(gl), (8, HD))
        return carry

    lax.fori_loop(0, nchunk // group, prep, 0)

    def step(d, ci, s):
        rows = pl.ds(pl.multiple_of(ci * CHUNK, CHUNK), CHUNK)
        mrows = pl.ds(pl.multiple_of(ci * HD, HD), HD)
        s_b16 = s.astype(BF16)
        o_s[d, rows, :] = o_s[d, rows, :] + _dot(qp_s[d, rows, :], s_b16)
        g_last = gl_s[d, pl.ds(pl.multiple_of(ci * 8, 8), 8), :][0:1, :]
        return s * g_last + (n_s[d, mrows, :] - _dot(m_s[d, mrows, :], s_b16))

    def body(n, carry):
        return step(0, n, carry[0]), step(1, nchunk - 1 - n, carry[1])

    s_f, s_b = lax.fori_loop(0, nchunk, body, (s0f_ref[0, 0], s0b_ref[0, 0]))
    sf_ref[0, 0] = s_f
    sb_ref[0, 0] = s_b
    y_ref[...] = _head_rms_gate(o_s[0] + o_s[1], ng_ref, z_ref).astype(BF16)


def _deltanet(p, conv_all, layer, a_log, dt_bias, norm_g, s0f, s0b, t):
    n = p.shape[0]
    bsz = n // t
    nchunk = t // CHUNK
    group = _pick(nchunk, (4, 2, 1))
    col = lambda name: (lambda b, h: (b, COL[name] + h))
    state = pl.BlockSpec((1, 1, HD, HD), lambda b, h: (b, h, 0, 0))
    smem = pl.BlockSpec(memory_space=pltpu.SMEM)
    conv = lambda off: pl.BlockSpec((None, 3, HD), lambda b, h: (layer, 0, off + h))
    st_shape = jax.ShapeDtypeStruct((bsz, HEADS, HD, HD), F32)
    seq_f32 = pltpu.VMEM((t, HD), F32)
    return pl.pallas_call(
        functools.partial(_dn_kernel, group=group),
        grid=(bsz, HEADS),
        in_specs=[smem, smem,
                  _seq_spec(t, col("dn_q")), _seq_spec(t, col("dn_k")),
                  _seq_spec(t, col("dn_v")), _seq_spec(t, col("dn_z"), 1),
                  _seq_spec(t, lambda b, h: (b, COL["dn_ab"]), 1),
                  conv(0), conv(HEADS), conv(2 * HEADS),
                  pl.BlockSpec((1, HD), lambda b, h: (0, 0)), state, state],
        out_specs=[pl.BlockSpec((t, HD), lambda b, h: (b, h)), state, state],
        out_shape=[jax.ShapeDtypeStruct((n, BRANCH_W), BF16), st_shape, st_shape],
        scratch_shapes=[seq_f32, seq_f32, seq_f32,
                        pltpu.VMEM((2, nchunk * HD, HD), BF16), pltpu.VMEM((2, nchunk * HD, HD), F32),
                        pltpu.VMEM((2, t, HD), BF16), pltpu.VMEM((2, nchunk * 8, HD), F32),
                        pltpu.VMEM((2, t, HD), F32)],
        compiler_params=_cparams(("parallel", "parallel"), 56),
        name="deltanet",
    )(a_log, dt_bias, p, p, p, p, p, conv_all, conv_all, conv_all, norm_g, s0f, s0b)


def _hg_kernel(q_ref, f0_ref, f1_ref, i_ref, z_ref, llb_ref, l1m_ref, oml_ref, ng_ref, s0f_ref,
               s0b_ref, y_ref, sf_ref, sb_ref, qs, lfs, ks, o_s):
    t = q_ref.shape[0]
    nchunk = t // CHUNK
    nsub = CHUNK // SUB
    qs[...] = _silu(q_ref[...])
    for d, f_ref in enumerate((f0_ref, f1_ref)):
        f = f_ref[...]
        e_f = jnp.exp(-jnp.abs(f))
        lsig = jnp.minimum(f, 0.0) - jnp.log1p(e_f)
        a = llb_ref[d:d + 1, :]
        b = l1m_ref[d:d + 1, :] + lsig
        lfs[d] = jnp.maximum(a, b) + jnp.log1p(jnp.exp(-jnp.abs(a - b)))
        ks[d] = oml_ref[d:d + 1, :] * (jnp.where(f >= 0.0, e_f, 1.0) * (1.0 / (1.0 + e_f)))

    incl, _ = _chunk_masks()
    incl_b = tuple(_as_bf16_mask(m) for m in incl)
    sub_i = lax.broadcasted_iota(jnp.int32, (SUB, HD), 0)
    half_i = lax.broadcasted_iota(jnp.int32, (HALF, HD), 0)

    def factored(q_rows, g_rows, k_rows, gk_rows, ref, q_keep=None):
        q_t = q_rows * jnp.exp(jnp.minimum(g_rows - ref, 0.0))
        k_t = k_rows * jnp.exp(jnp.minimum(ref - gk_rows, 0.0))
        if q_keep is not None:
            q_t, k_t = jnp.where(q_keep, q_t, 0.0), jnp.where(q_keep, 0.0, k_t)
        return q_t.astype(BF16), k_t.astype(BF16)

    def body(n, carry):
        chains = []
        for u in range(per_trip):
            ci = n * per_trip + u
            chains.append((0, pl.multiple_of(ci * CHUNK, CHUNK)))
            chains.append((1, pl.multiple_of((nchunk - 1 - ci) * CHUNK, CHUNK)))
        state, o_inter, pair_acc, jobs = list(carry), [], [], []
        for cidx, (d, c0) in enumerate(chains):
            s_t = state[d]
            qc = qs[pl.ds(c0, CHUNK), :]
            kc = ks[d, pl.ds(c0, CHUNK), :]
            vc = i_ref[pl.ds(c0, CHUNK), :]
            vb = vc.astype(BF16)
            gc = _mask_dot(incl_b[d], lfs[d, pl.ds(c0, CHUNK), :])
            g_last = gc[CHUNK - 1:CHUNK, :] if d == 0 else gc[0:1, :]
            o_inter.append(_dot_nt((qc * jnp.exp(gc)).astype(BF16), s_t.astype(BF16)))
            k_dec = kc * jnp.exp(g_last - gc)
            state[d] = s_t * jnp.exp(g_last) + _dot_tn(vb, k_dec.astype(BF16))
            for ib in range(nsub):
                r0 = ib * SUB
                blk = slice(r0, r0 + SUB)
                mid = r0 + HALF
                ref = gc[mid - 1:mid, :] if d == 0 else gc[mid:mid + 1, :]
                q_keep = (sub_i >= HALF) if d == 0 else (sub_i < HALF)
                jobs.append((cidx, ib, factored(qc[blk], gc[blk], kc[blk], gc[blk], ref, q_keep),
                             vb[blk]))
                if d == 0 and ib > 0:
                    rng, ref = slice(0, r0), gc[r0 - 1:r0, :]
                elif d == 1 and ib < nsub - 1:
                    rng, ref = slice(r0 + SUB, CHUNK), gc[r0 + SUB:r0 + SUB + 1, :]
                else:
                    continue
                jobs.append((cidx, ib, factored(qc[blk], gc[blk], kc[rng], gc[rng], ref), vb[rng]))
            halves = []
            for h0 in range(0, CHUNK, HALF):
                q_h, g_h = qc[h0:h0 + HALF, :], gc[h0:h0 + HALF, :]
                acc_h = jnp.zeros((HALF, HD), F32)
                for jj in range(HALF):
                    j = h0 + jj
                    term = q_h * kc[j:j + 1, :] * jnp.exp(jnp.minimum(g_h - gc[j:j + 1, :], 0.0))
                    keep = (half_i >= jj) if d == 0 else (half_i <= jj)
                    acc_h = acc_h + (jnp.sum(jnp.where(keep, term, 0.0), axis=-1, keepdims=True)
                                     * vc[j:j + 1, :])
                halves.append(acc_h)
            pair_acc.append(halves)
        scores = [_dot_nt(q_t, k_t).astype(BF16) for _, _, (q_t, k_t), _ in jobs]
        outs = [_dot(a, v_rows) for a, (_, _, _, v_rows) in zip(scores, jobs)]
        for cidx, (d, c0) in enumerate(chains):
            blocks = []
            for ib in range(nsub):
                acc = jnp.concatenate(pair_acc[cidx][2 * ib:2 * ib + 2], axis=0)
                for o_job, (jc, jb, _, _) in zip(outs, jobs):
                    if (jc, jb) == (cidx, ib):
                        acc = acc + o_job
                blocks.append(acc)
            o_s[d, pl.ds(c0, CHUNK), :] = o_inter[cidx] + jnp.concatenate(blocks, axis=0)
        return tuple(state)

    per_trip = 2 if nchunk % 2 == 0 else 1
    s_f, s_b = lax.fori_loop(0, nchunk // per_trip, body, (s0f_ref[0, 0], s0b_ref[0, 0]))
    sf_ref[0, 0] = s_f
    sb_ref[0, 0] = s_b
    y_ref[...] = _head_rms_gate(o_s[0] + o_s[1], ng_ref, z_ref).astype(BF16)


def _hgrn2(p, log_lb, log_1m_lb, one_m_lb, norm_g, s0f, s0b, t):
    n = p.shape[0]
    bsz = n // t
    col = lambda name: (lambda b, h: (b, COL[name] + h))
    state = pl.BlockSpec((1, 1, HD, HD), lambda b, h: (b, h, 0, 0))
    lbs = pl.BlockSpec((2, HD), lambda b, h: (0, h))
    st_shape = jax.ShapeDtypeStruct((bsz, HEADS, HD, HD), F32)
    return pl.pallas_call(
        _hg_kernel,
        grid=(bsz, HEADS),
        in_specs=[_seq_spec(t, col("hg_q")), _seq_spec(t, col("hg_f0")),
                  _seq_spec(t, col("hg_f1")), _seq_spec(t, col("hg_i")),
                  _seq_spec(t, col("hg_z")), lbs, lbs, lbs,
                  pl.BlockSpec((1, HD), lambda b, h: (0, 0)), state, state],
        out_specs=[pl.BlockSpec((t, HD), lambda b, h: (b, h)), state, state],
        out_shape=[jax.ShapeDtypeStruct((n, BRANCH_W), BF16), st_shape, st_shape],
        scratch_shapes=[pltpu.VMEM((t, HD), F32), pltpu.VMEM((2, t, HD), F32),
                        pltpu.VMEM((2, t, HD), F32), pltpu.VMEM((2, t, HD), F32)],
        compiler_params=_cparams(("parallel", "parallel"), 48),
        name="hgrn2",
    )(p, p, p, p, p, log_lb, log_1m_lb, one_m_lb, norm_g, s0f, s0b)


def _rope_tables(t):
    pos = jnp.arange(t)
    row = (pos // GRID_W).astype(F32)
    col = (pos % GRID_W).astype(F32)
    n = DA_DH // 4
    inv = ROPE_THETA ** (-jnp.arange(n, dtype=F32) / n)
    ar, ac = row[:, None] * inv, col[:, None] * inv
    zero = jnp.zeros_like(ar)
    cos = jnp.concatenate([jnp.cos(ar), jnp.cos(ar), jnp.cos(ac), jnp.cos(ac)], axis=-1)
    s_up = jnp.concatenate([zero, jnp.sin(ar), zero, jnp.sin(ac)], axis=-1)
    s_dn = jnp.concatenate([-jnp.sin(ar), zero, -jnp.sin(ac), zero], axis=-1)
    tile = lambda a: jnp.concatenate([a, a], axis=-1)
    return tile(cos), tile(s_up), tile(s_dn)


def _rope(x, cos, s_up, s_dn):
    half = DA_DH // 4
    return x * cos + pltpu.roll(x, half, 1) * s_up + pltpu.roll(x, HD - half, 1) * s_dn


def _da_kernel(lam_ref, q_ref, kl_ref, kc_ref, vl_ref, vc_ref, z_ref, cq_ref, uq_ref, dq_ref,
               ck_ref, uk_ref, dk_ref, ng_ref, y_ref, k_s, vt_s, s_s, *, t_lat, out_scale):
    nkb, kblk, _ = k_s.shape
    nlat = t_lat // kblk

    @pl.when(pl.program_id(2) == 0)
    def _():
        for j in range(nkb):
            if j < nlat:
                rows = slice(j * kblk, (j + 1) * kblk)
                k = _rope(kl_ref[rows, :], ck_ref[rows, :], uk_ref[rows, :], dk_ref[rows, :])
                v = vl_ref[rows, :]
            else:
                rows = slice((j - nlat) * kblk, (j - nlat + 1) * kblk)
                k, v = kc_ref[rows, :], vc_ref[rows, :]
            k_s[j] = k.astype(BF16)
            vt_s[j] = v.T.astype(BF16)

    q = q_ref[...]
    if t_lat:
        q = _rope(q, cq_ref[...], uq_ref[...], dq_ref[...])
    q = q * (DA_DH ** -0.5 * LOG2E)
    lane = lax.broadcasted_iota(jnp.int32, q.shape, 1)
    qms = [jnp.where(lane < DA_DH, q, 0.0).astype(BF16), jnp.where(lane >= DA_DH, q, 0.0).astype(BF16)]
    tq = q.shape[0]
    m = [None, None]
    l = [jnp.zeros((1, tq), F32) for _ in range(2)]
    acc = [jnp.zeros((HD, tq), F32) for _ in range(2)]

    def scores(mp, j):
        s = _dot_nt(k_s[j], qms[mp])
        s_s[mp, j] = s
        bm = jnp.max(s, axis=0, keepdims=True)
        m[mp] = bm if m[mp] is None else jnp.maximum(m[mp], bm)

    def values(mp, j):
        e = jnp.exp2(s_s[mp, j] - m[mp])
        l[mp] = l[mp] + jnp.sum(e, axis=0, keepdims=True)
        acc[mp] = acc[mp] + _dot(vt_s[j], e.astype(BF16))

    for j in range(nkb):
        scores(0, j)
    for j in range(nkb):
        scores(1, j)
        values(0, j)
    for j in range(nkb):
        values(1, j)
    o = (acc[0] * (1.0 / l[0]) - acc[1] * (lam_ref[0] * (1.0 / l[1]))).T
    y = o * lax.rsqrt(jnp.mean(o * o, axis=-1, keepdims=True) + EPS) * ng_ref[...]
    y_ref[...] = (y * out_scale * _silu(z_ref[...])).astype(BF16)


def _diff_attention(lam, p_q, p_lat, p_ctx, rope, norm_g, t, t_ctx, out_scale, latent):
    n = p_q.shape[0]
    bsz = n // t
    tq = _pick(t, (512, 256, 128))
    nq = t // tq
    t_lat = t if latent else 0
    t_kl = t if latent else t_ctx
    cos, s_up, s_dn = rope
    kblk = 256 if (t_lat % 256 == 0 and t_ctx % 256 == 0) else 128
    nkb = (t_lat + t_ctx) // kblk
    col = lambda name: (lambda b, h, i: (b, COL[name] + h))
    qcol = lambda name: (lambda b, h, i: (b * nq + i, COL[name] + h))
    tab_q = pl.BlockSpec((tq, HD), lambda b, h, i: (i, 0))
    tab_k = _seq_spec(t_kl, lambda b, h, i: (0, 0), 1)
    return pl.pallas_call(
        functools.partial(_da_kernel, t_lat=t_lat, out_scale=out_scale),
        grid=(bsz, HEADS, nq),
        in_specs=[pl.BlockSpec(memory_space=pltpu.SMEM),
                  pl.BlockSpec((tq, HD), qcol("da_q")),
                  _seq_spec(t_kl, col("da_k")), _seq_spec(t_ctx, col("da_k")),
                  _seq_spec(t_kl, col("da_v")), _seq_spec(t_ctx, col("da_v")),
                  pl.BlockSpec((tq, HD), qcol("da_z")),
                  tab_q, tab_q, tab_q, tab_k, tab_k, tab_k,
                  pl.BlockSpec((1, HD), lambda b, h, i: (0, 0))],
        out_specs=pl.BlockSpec((tq, HD), lambda b, h, i: (b * nq + i, h)),
        out_shape=jax.ShapeDtypeStruct((n, BRANCH_W), BF16),
        scratch_shapes=[pltpu.VMEM((nkb, kblk, HD), BF16), pltpu.VMEM((nkb, HD, kblk), BF16),
                        pltpu.VMEM((2, nkb, kblk, tq), F32)],
        compiler_params=_cparams(("parallel", "parallel", "arbitrary"), 56),
        name="diff_attention",
    )(lam, p_q, p_lat, p_ctx, p_lat, p_ctx, p_q, cos, s_up, s_dn, cos, s_up, s_dn, norm_g)


def _merge_kernel(y0_ref, y1_ref, y2_ref, y3_ref, gl_ref, wb_ref, wo_ref, x_ref, mod_ref, fg_ref,
                  o_ref, *, final):
    d = x_ref.shape[1]
    acc = None
    for k, y_ref in enumerate((y0_ref, y1_ref, y2_ref, y3_ref)):
        c = jax.nn.sigmoid(gl_ref[:, k * d:(k + 1) * d]) * _dot(y_ref[...], wb_ref[k])
        acc = c if acc is None else acc + c
    x = x_ref[...] + mod_ref[0, 2:3, :] * _dot(acc.astype(BF16), wo_ref[...])
    if final:
        x = x * lax.rsqrt(jnp.mean(x * x, axis=-1, keepdims=True) + EPS) * fg_ref[...]
    o_ref[...] = x


def _merge(ys, p, wb_all, wo_all, layer, x2, mod, final_g, rows_per_seg, final):
    n, d = x2.shape
    tm = _pick(rows_per_seg, (256, 128))
    ysp = pl.BlockSpec((tm, BRANCH_W), lambda i: (i, 0))
    return pl.pallas_call(
        functools.partial(_merge_kernel, final=final),
        grid=(n // tm,),
        in_specs=[ysp, ysp, ysp, ysp,
                  pl.BlockSpec((tm, N_BRANCH * d), lambda i: (i, 0)),
                  pl.BlockSpec((None, N_BRANCH, BRANCH_W, d), lambda i: (layer, 0, 0, 0),
                               pipeline_mode=pl.Buffered(1)),
                  pl.BlockSpec((None, d, d), lambda i: (layer, 0, 0), pipeline_mode=pl.Buffered(1)),
                  pl.BlockSpec((tm, d), lambda i: (i, 0)),
                  pl.BlockSpec((1, 3, d), lambda i: (i * tm // rows_per_seg, 0, 0)),
                  pl.BlockSpec((1, d), lambda i: (0, 0))],
        out_specs=pl.BlockSpec((tm, d), lambda i: (i, 0)),
        out_shape=jax.ShapeDtypeStruct((n, d), F32),
        compiler_params=_cparams(("parallel",), 56),
        name="merge",
    )(*ys, p, wb_all, wo_all, x2, mod, final_g)


def kernel(x, c, ctx, c_ctx, norm_g, w_ada, b_ada, w_in, fn_w, fn_b, dn_conv, dn_a_log, dn_dt_bias,
           dn_norm, hg_lb_logits, hg_norm, da_lambda, da_norm, w_branch, w_out, final_g):
    bsz, t, d = x.shape
    t_ctx = ctx.shape[1]
    depth = w_in.shape[0]
    assert d == N_BRANCH * BRANCH_W and t % CHUNK == 0 and t_ctx % CHUNK == 0 and bsz + 1 <= 8

    w_in_r = _transposed_w_in(w_in)
    wb16, wo16, fnw16 = w_branch.astype(BF16), w_out.astype(BF16), fn_w.astype(BF16)

    c8 = jnp.concatenate([c, c_ctx[None, :], jnp.zeros((8 - bsz - 1, d), F32)], axis=0)
    mod = _ada(c8, w_ada, b_ada).reshape(depth, 8, 3, d)

    lb_all = jnp.cumsum(jax.nn.softmax(hg_lb_logits.astype(F32), axis=1), axis=1)
    lb_all = lb_all - lb_all[:, :1]
    log_lb, log_1m_lb, one_m_lb = jnp.log(lb_all), jnp.log1p(-lb_all), 1.0 - lb_all

    c_ch, s_ch = _dft_cos_sin(FN_GW)
    cs_ch = jnp.concatenate([c_ch, s_ch], axis=-1).astype(BF16)
    dft = {}
    for tt in (t, t_ctx):
        ct, st = _dft_cos_sin(tt)
        dft[tt] = (ct.astype(BF16), (-st).astype(BF16))
    rope_l = _rope_tables(t)
    rope_c = tuple(a[:t_ctx] for a in rope_l)

    xl = x.reshape(bsz * t, d)
    xc = ctx.reshape(bsz * t_ctx, d)
    zstate = jnp.zeros((bsz, HEADS, HD, HD), F32)
    for l in range(depth):
        last = l == depth - 1
        g_l = norm_g[l][None, :]
        mod_l, mod_c = mod[l, :bsz], mod[l, bsz:bsz + 1]
        pl_ = _proj(xl, mod_l, g_l, w_in_r, l, t)
        pc_ = _proj(xc, mod_c, g_l, w_in_r, l, bsz * t_ctx)

        fn_b_l = fn_b[l][None, :]
        y_fn_l = _fourier_positions(_fourier_channels(pl_, cs_ch), pl_, *dft[t], fnw16, fn_b_l, l, t)

        dn_n, hg_n, da_n = dn_norm[l][None, :], hg_norm[l][None, :], da_norm[l][None, :]
        y_dn_c, s_f, s_b = _deltanet(pc_, dn_conv, l, dn_a_log[l], dn_dt_bias[l], dn_n,
                                     zstate, zstate, t_ctx)
        y_dn_l, _, _ = _deltanet(pl_, dn_conv, l, dn_a_log[l], dn_dt_bias[l], dn_n, s_f, s_b, t)

        lbs = (log_lb[:, l], log_1m_lb[:, l], one_m_lb[:, l])
        y_hg_c, h_f, h_b = _hgrn2(pc_, *lbs, hg_n, zstate, zstate, t_ctx)
        y_hg_l, _, _ = _hgrn2(pl_, *lbs, hg_n, h_f, h_b, t)

        lam_init = 0.8 - 0.6 * math.exp(-0.3 * l)
        lp = da_lambda[l].astype(F32)
        lam = (jnp.exp(jnp.sum(lp[0] * lp[1])) - jnp.exp(jnp.sum(lp[2] * lp[3])) + lam_init).reshape(1)
        y_da_l = _diff_attention(lam, pl_, pl_, pc_, rope_l, da_n, t, t_ctx, 1.0 - lam_init, True)

        fg = final_g[None, :]
        new_xl = _merge((y_fn_l, y_dn_l, y_hg_l, y_da_l), pl_, wb16, wo16, l, xl, mod_l, fg, t, last)
        if not last:
            y_fn_c = _fourier_positions(_fourier_channels(pc_, cs_ch), pc_, *dft[t_ctx], fnw16,
                                        fn_b_l, l, t_ctx)
            y_da_c = _diff_attention(lam, pc_, pc_, pc_, rope_c, da_n, t_ctx, t_ctx,
                                     1.0 - lam_init, False)
            xc = _merge((y_fn_c, y_dn_c, y_hg_c, y_da_c), pc_, wb16, wo16, l, xc, mod_c, fg,
                        bsz * t_ctx, False)
        xl = new_xl
    return xl.reshape(bsz, t, d)
```

```python
import functools
import math

import jax
import jax.numpy as jnp
from jax import lax
from jax.experimental import pallas as pl
from jax.experimental.pallas import tpu as pltpu

F32 = jnp.float32
BF16 = jnp.bfloat16

EPS = 1e-6
N_BRANCH = 4
HEADS = 4
HD = 128
BRANCH_W = HEADS * HD
FN_GW = 128
CHUNK = 64
SUB = 16
HALF = SUB // 2
GRID_W = 64
ROPE_THETA = 10000.0
DA_DH = 64
LOG2E = 1.4426950408889634

COL = dict(fn_u=64, fn_z=68, dn_q=72, dn_k=76, dn_v=80, dn_z=84, hg_q=88, hg_f0=92, hg_f1=96,
           hg_i=100, hg_z=104, da_q=108, da_k=112, da_v=116, da_z=120, dn_ab=124)
PROJ_W = 126 * 128
GATE_W_OFF = 7696
AB_OFF = 3072

MIB = 1024 * 1024


def _cparams(sem, vmem_mib):
    return pltpu.CompilerParams(dimension_semantics=sem, vmem_limit_bytes=vmem_mib * MIB)


def _silu(x):
    return x * jax.nn.sigmoid(x)


def _dot(a, b):
    return jnp.dot(a, b, preferred_element_type=F32)


def _dot_nt(a, b):
    return lax.dot_general(a, b, (((1,), (1,)), ((), ())), preferred_element_type=F32)


def _dot_tn(a, b):
    return lax.dot_general(a, b, (((0,), (0,)), ((), ())), preferred_element_type=F32)


def _split2(x):
    hi = x.astype(BF16)
    return hi, (x - hi.astype(F32)).astype(BF16)


def _split3(x):
    hi = x.astype(BF16)
    r = x - hi.astype(F32)
    mid = r.astype(BF16)
    return hi, mid, (r - mid.astype(F32)).astype(BF16)


def _dot3(a, b):
    a_hi, a_lo = _split2(a)
    b_hi, b_lo = _split2(b)
    return _dot(a_hi, b_hi) + (_dot(a_hi, b_lo) + _dot(a_lo, b_hi))


def _mask_dot(mask, x):
    hi, mid, lo = _split3(x)
    return _dot(mask, hi) + (_dot(mask, mid) + _dot(mask, lo))


def _dot_mask(x, mask):
    hi, mid, lo = _split3(x)
    return _dot(hi, mask) + (_dot(mid, mask) + _dot(lo, mask))


def _pick(n, cands):
    for c in cands:
        if n % c == 0:
            return c
    return n


def _ada_kernel(c_ref, w_ref, b_ref, o_ref):
    sc = _silu(c_ref[...])
    o_ref[0] = _dot(sc.astype(BF16), w_ref[0].astype(BF16)) + b_ref[0]


def _ada(c8, w_ada, b_ada):
    depth, d, d3 = w_ada.shape
    tn = _pick(d3, (1536, 768, 512, 256, 128))
    return pl.pallas_call(
        _ada_kernel,
        grid=(depth, d3 // tn),
        in_specs=[pl.BlockSpec((8, d), lambda l, n: (0, 0)),
                  pl.BlockSpec((1, d, tn), lambda l, n: (l, 0, n)),
                  pl.BlockSpec((1, 1, tn), lambda l, n: (l, 0, n))],
        out_specs=pl.BlockSpec((1, 8, tn), lambda l, n: (l, 0, n)),
        out_shape=jax.ShapeDtypeStruct((depth, 8, d3), F32),
        compiler_params=_cparams(("parallel", "parallel"), 48),
        name="adaln_mod",
    )(c8, w_ada, b_ada.reshape(depth, 1, d3))


def _transposed_w_in(w_in):
    depth, d, _ = w_in.shape
    n_ab = 4 * HEADS
    wt = jnp.swapaxes(w_in, 1, 2)
    return jnp.concatenate(
        [wt[:, GATE_W_OFF:], wt[:, :AB_OFF], wt[:, AB_OFF + n_ab:GATE_W_OFF], wt[:, AB_OFF:AB_OFF + n_ab],
         jnp.zeros((depth, 2 * HD - n_ab, d), w_in.dtype)], axis=1).astype(BF16)


def _proj_kernel(x_ref, mod_ref, g_ref, w_ref, o_ref, h_ref):
    @pl.when(pl.program_id(1) == 0)
    def _():
        x = x_ref[...]
        y = x * lax.rsqrt(jnp.mean(x * x, axis=-1, keepdims=True) + EPS) * g_ref[...]
        h = y * (1.0 + mod_ref[0, 1:2, :]) + mod_ref[0, 0:1, :]
        h_ref[...] = h.astype(BF16)

    o_ref[...] = _dot_nt(h_ref[...], w_ref[...])


def _proj(x2, mod, g, wt_all, layer, rows_per_seg):
    n, d = x2.shape
    tm = _pick(rows_per_seg, (1024, 512, 256, 128))
    tn = 1792
    return pl.pallas_call(
        _proj_kernel,
        grid=(n // tm, PROJ_W // tn),
        in_specs=[pl.BlockSpec((tm, d), lambda i, j: (i, 0), pipeline_mode=pl.Buffered(1)),
                  pl.BlockSpec((1, 3, d), lambda i, j: (i * tm // rows_per_seg, 0, 0)),
                  pl.BlockSpec((1, d), lambda i, j: (0, 0)),
                  pl.BlockSpec((None, tn, d), lambda i, j: (layer, j, 0))],
        out_specs=pl.BlockSpec((tm, tn), lambda i, j: (i, j)),
        out_shape=jax.ShapeDtypeStruct((n, PROJ_W), F32),
        scratch_shapes=[pltpu.VMEM((tm, d), BF16)],
        compiler_params=_cparams(("parallel", "arbitrary"), 56),
        name="in_proj",
    )(x2, mod, g, wt_all)


def _dft_cos_sin(n):
    j = jnp.arange(n, dtype=jnp.int32)
    sc = n ** -0.5
    if n <= 1024:
        ang = (2.0 * math.pi / n) * ((j[:, None] * j[None, :]) % n).astype(F32)
        return jnp.cos(ang) * sc, jnp.sin(ang) * sc
    m = n // 64
    k1 = jnp.arange(m, dtype=jnp.int32)
    k2 = jnp.arange(64, dtype=jnp.int32)
    a = (2.0 * math.pi / m) * ((j[:, None] * k1[None, :]) % m).astype(F32)
    b = (2.0 * math.pi / n) * ((j[:, None] * k2[None, :]) % n).astype(F32)
    ca, sa, cb, sb = jnp.cos(a), jnp.sin(a), jnp.cos(b), jnp.sin(b)
    c = ca[:, :, None] * cb[:, None, :] - sa[:, :, None] * sb[:, None, :]
    s = sa[:, :, None] * cb[:, None, :] + ca[:, :, None] * sb[:, None, :]
    return c.reshape(n, n) * sc, s.reshape(n, n) * sc


def _fn1_kernel(u_ref, cs_ref, o_ref):
    for g in range(BRANCH_W // FN_GW):
        r = _dot(u_ref[:, g * FN_GW:(g + 1) * FN_GW].astype(BF16), cs_ref[...])
        o_ref[:, g * FN_GW:(g + 1) * FN_GW] = r[:, :FN_GW].astype(BF16)
        o_ref[:, BRANCH_W + g * FN_GW:BRANCH_W + (g + 1) * FN_GW] = r[:, FN_GW:].astype(BF16)


def _fourier_channels(p, cs_ch):
    n = p.shape[0]
    tm = _pick(n, (1024, 512, 256, 128))
    return pl.pallas_call(
        _fn1_kernel,
        grid=(n // tm,),
        in_specs=[pl.BlockSpec((tm, BRANCH_W), lambda i: (i, COL["fn_u"] // 4)),
                  pl.BlockSpec((FN_GW, 2 * FN_GW), lambda i: (0, 0))],
        out_specs=pl.BlockSpec((tm, 2 * BRANCH_W), lambda i: (i, 0)),
        out_shape=jax.ShapeDtypeStruct((n, 2 * BRANCH_W), BF16),
        compiler_params=_cparams(("parallel",), 32),
        name="fourier_channels",
    )(p, cs_ch)


def _fn2_kernel(ct_ref, st_ref, r_ref, z_ref, w_ref, b_ref, y_ref):
    f = _dot(ct_ref[...], r_ref[:, :BRANCH_W]) + _dot(st_ref[...], r_ref[:, BRANCH_W:])
    y = _dot(f.astype(BF16), w_ref[...]) + b_ref[...]
    y_ref[...] = (y * _silu(z_ref[...])).astype(BF16)


def _fourier_positions(r, p, ct, nst, fn_w_all, fn_b, layer, t):
    n = p.shape[0]
    bsz = n // t
    tm = _pick(t, (512, 256, 128))
    nt = t // tm
    return pl.pallas_call(
        _fn2_kernel,
        grid=(bsz, nt),
        in_specs=[pl.BlockSpec((tm, t), lambda b, i: (i, 0)),
                  pl.BlockSpec((tm, t), lambda b, i: (i, 0)),
                  pl.BlockSpec((t, 2 * BRANCH_W), lambda b, i: (b, 0)),
                  pl.BlockSpec((tm, BRANCH_W), lambda b, i: (b * nt + i, COL["fn_z"] // 4)),
                  pl.BlockSpec((None, BRANCH_W, BRANCH_W), lambda b, i: (layer, 0, 0)),
                  pl.BlockSpec((1, BRANCH_W), lambda b, i: (0, 0))],
        out_specs=pl.BlockSpec((tm, BRANCH_W), lambda b, i: (b * nt + i, 0)),
        out_shape=jax.ShapeDtypeStruct((n, BRANCH_W), BF16),
        compiler_params=_cparams(("parallel", "arbitrary"), 48),
        name="fourier_positions",
    )(ct, nst, r, p, fn_w_all, fn_b)


def _seq_spec(rows, index_map, buffers=2):
    return pl.BlockSpec((rows, HD), index_map, pipeline_mode=pl.Buffered(buffers))


def _chunk_masks():
    i = lax.broadcasted_iota(jnp.int32, (CHUNK, CHUNK), 0)
    j = lax.broadcasted_iota(jnp.int32, (CHUNK, CHUNK), 1)
    incl = (i >= j, i <= j)
    strict = (i > j, i < j)
    return incl, strict


def _as_bf16_mask(m):
    return jnp.where(m, 1.0, 0.0).astype(BF16)


def _softplus(x):
    return jnp.maximum(x, 0.0) + _log1p_unit(jnp.exp(-jnp.abs(x)))


def _log1p_unit(x):
    return jnp.log(1.0 + x)


def _head_rms_gate(o, g_ref, z_ref):
    y = o * lax.rsqrt(jnp.mean(o * o, axis=-1, keepdims=True) + EPS) * g_ref[...]
    return y * _silu(z_ref[...])


def _dn_kernel(alog_ref, dtb_ref, q_ref, k_ref, v_ref, z_ref, ab_ref, cq_ref, ck_ref, cv_ref,
               ng_ref, s0f_ref, s0b_ref, y_ref, sf_ref, sb_ref,
               qs, ks, vs, m_s, n_s, qp_s, gl_s, o_s, *, group):
    h = pl.program_id(1)
    t = q_ref.shape[0]
    nchunk = t // CHUNK
    row = lax.broadcasted_iota(jnp.int32, (t, HD), 0)

    def conv(x_ref, c_ref):
        x = x_ref[...]
        xm = jnp.where(row == 0, 0.0, pltpu.roll(x, 1, 0))
        xp = jnp.where(row == t - 1, 0.0, pltpu.roll(x, t - 1, 0))
        y = xm * c_ref[0:1, :] + x * c_ref[1:2, :] + xp * c_ref[2:3, :]
        return _silu(y)

    q = conv(q_ref, cq_ref)
    qs[...] = q * lax.rsqrt(jnp.sum(q * q, axis=-1, keepdims=True) + EPS) * HD ** -0.5
    k = conv(k_ref, ck_ref)
    ks[...] = k * lax.rsqrt(jnp.sum(k * k, axis=-1, keepdims=True) + EPS)
    vs[...] = conv(v_ref, cv_ref)

    incl, _ = _chunk_masks()
    incl_b = tuple(_as_bf16_mask(m) for m in incl)
    ri = lax.broadcasted_iota(jnp.int32, (CHUNK, HD), 0)
    li = lax.broadcasted_iota(jnp.int32, (CHUNK, HD), 1)
    cj = li % CHUNK
    incl2 = (ri >= cj, ri <= cj)
    strict2 = (ri > cj, ri < cj)
    right = li >= CHUNK
    eye_right = jnp.where(li == ri + CHUNK, 1.0, 0.0)
    sel_r = lax.broadcasted_iota(jnp.int32, (HD, 2 * HD), 0)
    sel_c = lax.broadcasted_iota(jnp.int32, (HD, 2 * HD), 1)
    sels = [_as_bf16_mask(sel_r == (sel_c // HD) * 2 * HEADS + d * HEADS + h) for d in range(2)]
    neg_a = [-jnp.exp(jnp.full((1, HD), alog_ref[d, h], F32)) for d in range(2)]
    dtb = [dtb_ref[d, h] for d in range(2)]
    ngroup = nchunk // group
    aligned = lambda x, m: x if isinstance(x, int) else pl.multiple_of(x, m)

    def step(d, ci, s):
        rows = pl.ds(aligned(ci * CHUNK, CHUNK), CHUNK)
        mrows = pl.ds(aligned(ci * HD, HD), HD)
        s_b16 = s.astype(BF16)
        o_s[d, rows, :] = o_s[d, rows, :] + _dot(qp_s[d, rows, :], s_b16)
        g_last = gl_s[d, pl.ds(aligned(ci * 8, 8), 8), :][0:1, :]
        return s * g_last + (n_s[d, mrows, :] - _dot(m_s[d, mrows, :], s_b16))

    def prep(g, between):
        base = (g * group, nchunk - (g + 1) * group)
        rows_g = [pl.ds(aligned(b0 * CHUNK, CHUNK), group * CHUNK) for b0 in base]
        q_g = [qs[r, :] for r in rows_g]
        k_g = [ks[r, :] for r in rows_g]
        v_g = [vs[r, :] for r in rows_g]
        ab = [_dot_mask(ab_ref[rows_g[d], :], sels[d]) for d in range(2)]
        gb_all = [neg_a[d] * _softplus(ab[d][:, :HD] + dtb[d]) for d in range(2)]
        bt_all = [jax.nn.sigmoid(ab[d][:, HD:]) for d in range(2)]
        chains = [(c, d) for c in range(group) for d in range(2)]
        sl = lambda c: slice(c * CHUNK, (c + 1) * CHUNK)
        kcbs = [k_g[d][sl(c)].astype(BF16) for c, d in chains]
        kk2 = [_dot_nt(kb_, jnp.concatenate([kb_, kb_], axis=0)) for kb_ in kcbs]
        qk = [_dot_nt(q_g[d][sl(c)].astype(BF16), kb_) for (c, d), kb_ in zip(chains, kcbs)]
        gcs = [_mask_dot(incl_b[d], gb_all[d][sl(c)]) for c, d in chains]
        between(0)
        grs = [jnp.concatenate([gc, gc], axis=0).T[:CHUNK, :] for gc in gcs]
        decays = [jnp.where(incl2[d], jnp.exp(jnp.where(incl2[d], gc - gr, 0.0)), 0.0)
                  for (c, d), gc, gr in zip(chains, gcs, grs)]
        zs = [jnp.where(strict2[d] & ~right, -(kk * bt_all[d][sl(c)] * dec), 0.0) + eye_right
              for (c, d), kk, dec in zip(chains, kk2, decays)]
        for level in range(6):
            zs = [_dot3(z[:, :CHUNK], z) + jnp.where(right, z, 0.0) for z in zs]
            if level in (1, 3):
                between((level + 1) // 2)
        zero_rows = jnp.zeros((CHUNK, 2 * HD), BF16)
        egs = [jnp.exp(gc) for gc in gcs]
        g_lasts = [gc[CHUNK - 1:CHUNK, :] if d == 0 else gc[0:1, :] for (c, d), gc in zip(chains, gcs)]
        wus = [_dot(z.astype(BF16), jnp.concatenate(
                   [zero_rows,
                    jnp.concatenate([(k_g[d][sl(c)] * bt_all[d][sl(c)] * eg).astype(BF16),
                                     (v_g[d][sl(c)] * bt_all[d][sl(c)]).astype(BF16)], axis=1)],
                   axis=0)).astype(BF16)
               for (c, d), z, eg in zip(chains, zs, egs)]
        between(3)
        mns = [_dot_tn((k_g[d][sl(c)] * jnp.exp(gl - gc)).astype(BF16), wu)
               for (c, d), gc, gl, wu in zip(chains, gcs, g_lasts, wus)]
        qos = [_dot(jnp.where(incl[d], qk_ * dec[:, :CHUNK], 0.0).astype(BF16), wu)
               for (c, d), qk_, dec, wu in zip(chains, qk, decays, wus)]
        for (c, d), eg, gl, mn, qo in zip(chains, egs, g_lasts, mns, qos):
            ci = base[d] + c
            rows = pl.ds(aligned(ci * CHUNK, CHUNK), CHUNK)
            mrows = pl.ds(aligned(ci * HD, HD), HD)
            m_s[d, mrows, :] = mn[:, :HD].astype(BF16)
            n_s[d, mrows, :] = mn[:, HD:]
            qp_s[d, rows, :] = (q_g[d][sl(c)] * eg - qo[:, :HD]).astype(BF16)
            o_s[d, rows, :] = qo[:, HD:]
            gl_s[d, pl.ds(aligned(ci * 8, 8), 8), :] = jnp.broadcast_to(jnp.exp(gl), (8, HD))

    def recurrence_steps(g, state):
        def between(u):
            if u < group:
                state[0] = step(0, g * group + u, state[0])
                state[1] = step(1, nchunk - 1 - g * group - u, state[1])
        return between

    prep(0, lambda u: None)

    def body(g, carry):
        state = list(carry)
        prep(g, recurrence_steps(g - 1, state))
        return tuple(state)

    state = list(lax.fori_loop(1, ngroup, body, (s0f_ref[0, 0], s0b_ref[0, 0])))
    last = recurrence_steps(ngroup - 1, state)
    for u in range(group):
        last(u)
    s_f, s_b = state
    sf_ref[0, 0] = s_f
    sb_ref[0, 0] = s_b
    y_ref[...] = _head_rms_gate(o_s[0] + o_s[1], ng_ref, z_ref).astype(BF16)


def _deltanet(p, conv_all, layer, a_log, dt_bias, norm_g, s0f, s0b, t):
    n = p.shape[0]
    bsz = n // t
    nchunk = t // CHUNK
    group = _pick(nchunk, (4, 2, 1))
    col = lambda name: (lambda b, h: (b, COL[name] + h))
    state = pl.BlockSpec((1, 1, HD, HD), lambda b, h: (b, h, 0, 0))
    smem = pl.BlockSpec(memory_space=pltpu.SMEM)
    conv = lambda off: pl.BlockSpec((None, 3, HD), lambda b, h: (layer, 0, off + h))
    st_shape = jax.ShapeDtypeStruct((bsz, HEADS, HD, HD), F32)
    seq_f32 = pltpu.VMEM((t, HD), F32)
    return pl.pallas_call(
        functools.partial(_dn_kernel, group=group),
        grid=(bsz, HEADS),
        in_specs=[smem, smem,
                  _seq_spec(t, col("dn_q")), _seq_spec(t, col("dn_k")),
                  _seq_spec(t, col("dn_v")), _seq_spec(t, col("dn_z"), 1),
                  _seq_spec(t, lambda b, h: (b, COL["dn_ab"]), 1),
                  conv(0), conv(HEADS), conv(2 * HEADS),
                  pl.BlockSpec((1, HD), lambda b, h: (0, 0)), state, state],
        out_specs=[pl.BlockSpec((t, HD), lambda b, h: (b, h)), state, state],
        out_shape=[jax.ShapeDtypeStruct((n, BRANCH_W), BF16), st_shape, st_shape],
        scratch_shapes=[seq_f32, seq_f32, seq_f32,
                        pltpu.VMEM((2, nchunk * HD, HD), BF16), pltpu.VMEM((2, nchunk * HD, HD), F32),
                        pltpu.VMEM((2, t, HD), BF16), pltpu.VMEM((2, nchunk * 8, HD), F32),
                        pltpu.VMEM((2, t, HD), F32)],
        compiler_params=_cparams(("parallel", "parallel"), 56),
        name="deltanet",
    )(a_log, dt_bias, p, p, p, p, p, conv_all, conv_all, conv_all, norm_g, s0f, s0b)


def _hg_kernel(q_ref, f0_ref, f1_ref, i_ref, z_ref, llb_ref, l1m_ref, oml_ref, ng_ref, s0f_ref,
               s0b_ref, y_ref, sf_ref, sb_ref, qs, lfs, ks, o_s):
    t = q_ref.shape[0]
    nchunk = t // CHUNK
    nsub = CHUNK // SUB
    qs[...] = _silu(q_ref[...])
    for d, f_ref in enumerate((f0_ref, f1_ref)):
        f = f_ref[...]
        e_f = jnp.exp(-jnp.abs(f))
        lsig = jnp.minimum(f, 0.0) - _log1p_unit(e_f)
        a = llb_ref[d:d + 1, :]
        b = l1m_ref[d:d + 1, :] + lsig
        lfs[d] = jnp.maximum(a, b) + _log1p_unit(jnp.exp(-jnp.abs(a - b)))
        ks[d] = oml_ref[d:d + 1, :] * (jnp.where(f >= 0.0, e_f, 1.0) * (1.0 / (1.0 + e_f)))

    incl, _ = _chunk_masks()
    incl_b = tuple(_as_bf16_mask(m) for m in incl)
    sub_i = lax.broadcasted_iota(jnp.int32, (SUB, HD), 0)
    half_i = lax.broadcasted_iota(jnp.int32, (HALF, HD), 0)

    def factored(q_rows, g_rows, k_rows, gk_rows, ref, q_keep=None):
        q_t = q_rows * jnp.exp(jnp.minimum(g_rows - ref, 0.0))
        k_t = k_rows * jnp.exp(jnp.minimum(ref - gk_rows, 0.0))
        if q_keep is not None:
            q_t, k_t = jnp.where(q_keep, q_t, 0.0), jnp.where(q_keep, 0.0, k_t)
        return q_t.astype(BF16), k_t.astype(BF16)

    def body(n, carry):
        chains = []
        for u in range(per_trip):
            ci = n * per_trip + u
            chains.append((0, pl.multiple_of(ci * CHUNK, CHUNK)))
            chains.append((1, pl.multiple_of((nchunk - 1 - ci) * CHUNK, CHUNK)))
        state, o_inter, pair_acc, jobs = list(carry), [], [], []
        for cidx, (d, c0) in enumerate(chains):
            s_t = state[d]
            qc = qs[pl.ds(c0, CHUNK), :]
            kc = ks[d, pl.ds(c0, CHUNK), :]
            vc = i_ref[pl.ds(c0, CHUNK), :]
            vb = vc.astype(BF16)
            gc = _mask_dot(incl_b[d], lfs[d, pl.ds(c0, CHUNK), :])
            g_last = gc[CHUNK - 1:CHUNK, :] if d == 0 else gc[0:1, :]
            o_inter.append(_dot_nt((qc * jnp.exp(gc)).astype(BF16), s_t.astype(BF16)))
            k_dec = kc * jnp.exp(g_last - gc)
            state[d] = s_t * jnp.exp(g_last) + _dot_tn(vb, k_dec.astype(BF16))
            for ib in range(nsub):
                r0 = ib * SUB
                blk = slice(r0, r0 + SUB)
                mid = r0 + HALF
                ref = gc[mid - 1:mid, :] if d == 0 else gc[mid:mid + 1, :]
                q_keep = (sub_i >= HALF) if d == 0 else (sub_i < HALF)
                jobs.append((cidx, ib, factored(qc[blk], gc[blk], kc[blk], gc[blk], ref, q_keep),
                             vb[blk]))
                if d == 0 and ib > 0:
                    rng, ref = slice(0, r0), gc[r0 - 1:r0, :]
                elif d == 1 and ib < nsub - 1:
                    rng, ref = slice(r0 + SUB, CHUNK), gc[r0 + SUB:r0 + SUB + 1, :]
                else:
                    continue
                jobs.append((cidx, ib, factored(qc[blk], gc[blk], kc[rng], gc[rng], ref), vb[rng]))
            halves = []
            for h0 in range(0, CHUNK, HALF):
                q_h, g_h = qc[h0:h0 + HALF, :], gc[h0:h0 + HALF, :]
                acc_h = jnp.zeros((HALF, HD), F32)
                for jj in range(HALF):
                    j = h0 + jj
                    term = q_h * kc[j:j + 1, :] * jnp.exp(jnp.minimum(g_h - gc[j:j + 1, :], 0.0))
                    keep = (half_i >= jj) if d == 0 else (half_i <= jj)
                    acc_h = acc_h + (jnp.sum(jnp.where(keep, term, 0.0), axis=-1, keepdims=True)
                                     * vc[j:j + 1, :])
                halves.append(acc_h)
            pair_acc.append(halves)
        scores = [_dot_nt(q_t, k_t).astype(BF16) for _, _, (q_t, k_t), _ in jobs]
        outs = [_dot(a, v_rows) for a, (_, _, _, v_rows) in zip(scores, jobs)]
        for cidx, (d, c0) in enumerate(chains):
            blocks = []
            for ib in range(nsub):
                acc = jnp.concatenate(pair_acc[cidx][2 * ib:2 * ib + 2], axis=0)
                for o_job, (jc, jb, _, _) in zip(outs, jobs):
                    if (jc, jb) == (cidx, ib):
                        acc = acc + o_job
                blocks.append(acc)
            o_s[d, pl.ds(c0, CHUNK), :] = o_inter[cidx] + jnp.concatenate(blocks, axis=0)
        return tuple(state)

    per_trip = _pick(nchunk, (4, 2, 1))
    s_f, s_b = lax.fori_loop(0, nchunk // per_trip, body, (s0f_ref[0, 0], s0b_ref[0, 0]))
    sf_ref[0, 0] = s_f
    sb_ref[0, 0] = s_b
    y_ref[...] = _head_rms_gate(o_s[0] + o_s[1], ng_ref, z_ref).astype(BF16)


def _hgrn2(p, log_lb, log_1m_lb, one_m_lb, norm_g, s0f, s0b, t):
    n = p.shape[0]
    bsz = n // t
    col = lambda name: (lambda b, h: (b, COL[name] + h))
    state = pl.BlockSpec((1, 1, HD, HD), lambda b, h: (b, h, 0, 0))
    lbs = pl.BlockSpec((2, HD), lambda b, h: (0, h))
    st_shape = jax.ShapeDtypeStruct((bsz, HEADS, HD, HD), F32)
    return pl.pallas_call(
        _hg_kernel,
        grid=(bsz, HEADS),
        in_specs=[_seq_spec(t, col("hg_q")), _seq_spec(t, col("hg_f0")),
                  _seq_spec(t, col("hg_f1")), _seq_spec(t, col("hg_i")),
                  _seq_spec(t, col("hg_z")), lbs, lbs, lbs,
                  pl.BlockSpec((1, HD), lambda b, h: (0, 0)), state, state],
        out_specs=[pl.BlockSpec((t, HD), lambda b, h: (b, h)), state, state],
        out_shape=[jax.ShapeDtypeStruct((n, BRANCH_W), BF16), st_shape, st_shape],
        scratch_shapes=[pltpu.VMEM((t, HD), F32), pltpu.VMEM((2, t, HD), F32),
                        pltpu.VMEM((2, t, HD), F32), pltpu.VMEM((2, t, HD), F32)],
        compiler_params=_cparams(("parallel", "parallel"), 48),
        name="hgrn2",
    )(p, p, p, p, p, log_lb, log_1m_lb, one_m_lb, norm_g, s0f, s0b)


def _rope_tables(t):
    pos = jnp.arange(t)
    row = (pos // GRID_W).astype(F32)
    col = (pos % GRID_W).astype(F32)
    n = DA_DH // 4
    inv = ROPE_THETA ** (-jnp.arange(n, dtype=F32) / n)
    ar, ac = row[:, None] * inv, col[:, None] * inv
    zero = jnp.zeros_like(ar)
    cos = jnp.concatenate([jnp.cos(ar), jnp.cos(ar), jnp.cos(ac), jnp.cos(ac)], axis=-1)
    s_up = jnp.concatenate([zero, jnp.sin(ar), zero, jnp.sin(ac)], axis=-1)
    s_dn = jnp.concatenate([-jnp.sin(ar), zero, -jnp.sin(ac), zero], axis=-1)
    tile = lambda a: jnp.concatenate([a, a], axis=-1)
    return tile(cos), tile(s_up), tile(s_dn)


def _rope(x, cos, s_up, s_dn):
    half = DA_DH // 4
    return x * cos + pltpu.roll(x, half, 1) * s_up + pltpu.roll(x, HD - half, 1) * s_dn


def _da_kernel(lam_ref, q_ref, kl_ref, kc_ref, vl_ref, vc_ref, z_ref, cq_ref, uq_ref, dq_ref,
               ck_ref, uk_ref, dk_ref, ng_ref, y_ref, k_s, vt_s, s_s, *, t_lat, out_scale):
    nkb, kblk, _ = k_s.shape
    nlat = t_lat // kblk

    @pl.when(pl.program_id(2) == 0)
    def _():
        for j in range(nkb):
            if j < nlat:
                rows = slice(j * kblk, (j + 1) * kblk)
                k = _rope(kl_ref[rows, :], ck_ref[rows, :], uk_ref[rows, :], dk_ref[rows, :])
                v = vl_ref[rows, :]
            else:
                rows = slice((j - nlat) * kblk, (j - nlat + 1) * kblk)
                k, v = kc_ref[rows, :], vc_ref[rows, :]
            k_s[j] = k.astype(BF16)
            vt_s[j] = v.T.astype(BF16)

    q = q_ref[...]
    if t_lat:
        q = _rope(q, cq_ref[...], uq_ref[...], dq_ref[...])
    q = q * (DA_DH ** -0.5 * LOG2E)
    lane = lax.broadcasted_iota(jnp.int32, q.shape, 1)
    qms = [jnp.where(lane < DA_DH, q, 0.0).astype(BF16), jnp.where(lane >= DA_DH, q, 0.0).astype(BF16)]
    tq = q.shape[0]
    m = [None, None]
    l = [jnp.zeros((1, tq), F32) for _ in range(2)]
    acc = [jnp.zeros((HD, tq), F32) for _ in range(2)]

    def scores(mp, j):
        s = _dot_nt(k_s[j], qms[mp])
        s_s[mp, j] = s
        bm = jnp.max(s, axis=0, keepdims=True)
        m[mp] = bm if m[mp] is None else jnp.maximum(m[mp], bm)

    def values(mp, j):
        e = jnp.exp2(s_s[mp, j] - m[mp])
        l[mp] = l[mp] + jnp.sum(e, axis=0, keepdims=True)
        acc[mp] = acc[mp] + _dot(vt_s[j], e.astype(BF16))

    for j in range(nkb):
        scores(0, j)
    for j in range(nkb):
        scores(1, j)
        values(0, j)
    for j in range(nkb):
        values(1, j)
    o = (acc[0] * (1.0 / l[0]) - acc[1] * (lam_ref[0] * (1.0 / l[1]))).T
    y = o * lax.rsqrt(jnp.mean(o * o, axis=-1, keepdims=True) + EPS) * ng_ref[...]
    y_ref[...] = (y * out_scale * _silu(z_ref[...])).astype(BF16)


def _diff_attention(lam, p_q, p_lat, p_ctx, rope, norm_g, t, t_ctx, out_scale, latent):
    n = p_q.shape[0]
    bsz = n // t
    tq = _pick(t, (512, 256, 128))
    nq = t // tq
    t_lat = t if latent else 0
    t_kl = t if latent else t_ctx
    cos, s_up, s_dn = rope
    kblk = 256 if (t_lat % 256 == 0 and t_ctx % 256 == 0) else 128
    nkb = (t_lat + t_ctx) // kblk
    col = lambda name: (lambda b, h, i: (b, COL[name] + h))
    qcol = lambda name: (lambda b, h, i: (b * nq + i, COL[name] + h))
    tab_q = pl.BlockSpec((tq, HD), lambda b, h, i: (i, 0))
    tab_k = _seq_spec(t_kl, lambda b, h, i: (0, 0), 1)
    return pl.pallas_call(
        functools.partial(_da_kernel, t_lat=t_lat, out_scale=out_scale),
        grid=(bsz, HEADS, nq),
        in_specs=[pl.BlockSpec(memory_space=pltpu.SMEM),
                  pl.BlockSpec((tq, HD), qcol("da_q")),
                  _seq_spec(t_kl, col("da_k")), _seq_spec(t_ctx, col("da_k")),
                  _seq_spec(t_kl, col("da_v")), _seq_spec(t_ctx, col("da_v")),
                  pl.BlockSpec((tq, HD), qcol("da_z")),
                  tab_q, tab_q, tab_q, tab_k, tab_k, tab_k,
                  pl.BlockSpec((1, HD), lambda b, h, i: (0, 0))],
        out_specs=pl.BlockSpec((tq, HD), lambda b, h, i: (b * nq + i, h)),
        out_shape=jax.ShapeDtypeStruct((n, BRANCH_W), BF16),
        scratch_shapes=[pltpu.VMEM((nkb, kblk, HD), BF16), pltpu.VMEM((nkb, HD, kblk), BF16),
                        pltpu.VMEM((2, nkb, kblk, tq), F32)],
        compiler_params=_cparams(("parallel", "parallel", "arbitrary"), 56),
        name="diff_attention",
    )(lam, p_q, p_lat, p_ctx, p_lat, p_ctx, p_q, cos, s_up, s_dn, cos, s_up, s_dn, norm_g)


def _merge_kernel(y0_ref, y1_ref, y2_ref, y3_ref, gl_ref, wb_ref, wo_ref, x_ref, mod_ref, fg_ref,
                  o_ref, *, final):
    d = x_ref.shape[1]
    acc = None
    for k, y_ref in enumerate((y0_ref, y1_ref, y2_ref, y3_ref)):
        c = jax.nn.sigmoid(gl_ref[:, k * d:(k + 1) * d]) * _dot(y_ref[...], wb_ref[k])
        acc = c if acc is None else acc + c
    x = x_ref[...] + mod_ref[0, 2:3, :] * _dot(acc.astype(BF16), wo_ref[...])
    if final:
        x = x * lax.rsqrt(jnp.mean(x * x, axis=-1, keepdims=True) + EPS) * fg_ref[...]
    o_ref[...] = x


def _merge(ys, p, wb_all, wo_all, layer, x2, mod, final_g, rows_per_seg, final):
    n, d = x2.shape
    tm = _pick(rows_per_seg, (256, 128))
    ysp = pl.BlockSpec((tm, BRANCH_W), lambda i: (i, 0))
    return pl.pallas_call(
        functools.partial(_merge_kernel, final=final),
        grid=(n // tm,),
        in_specs=[ysp, ysp, ysp, ysp,
                  pl.BlockSpec((tm, N_BRANCH * d), lambda i: (i, 0)),
                  pl.BlockSpec((None, N_BRANCH, BRANCH_W, d), lambda i: (layer, 0, 0, 0),
                               pipeline_mode=pl.Buffered(1)),
                  pl.BlockSpec((None, d, d), lambda i: (layer, 0, 0), pipeline_mode=pl.Buffered(1)),
                  pl.BlockSpec((tm, d), lambda i: (i, 0)),
                  pl.BlockSpec((1, 3, d), lambda i: (i * tm // rows_per_seg, 0, 0)),
                  pl.BlockSpec((1, d), lambda i: (0, 0))],
        out_specs=pl.BlockSpec((tm, d), lambda i: (i, 0)),
        out_shape=jax.ShapeDtypeStruct((n, d), F32),
        compiler_params=_cparams(("parallel",), 56),
        name="merge",
    )(*ys, p, wb_all, wo_all, x2, mod, final_g)


def kernel(x, c, ctx, c_ctx, norm_g, w_ada, b_ada, w_in, fn_w, fn_b, dn_conv, dn_a_log, dn_dt_bias,
           dn_norm, hg_lb_logits, hg_norm, da_lambda, da_norm, w_branch, w_out, final_g):
    bsz, t, d = x.shape
    t_ctx = ctx.shape[1]
    depth = w_in.shape[0]
    assert d == N_BRANCH * BRANCH_W and t % CHUNK == 0 and t_ctx % CHUNK == 0 and bsz + 1 <= 8

    w_in_r = _transposed_w_in(w_in)
    wb16, wo16, fnw16 = w_branch.astype(BF16), w_out.astype(BF16), fn_w.astype(BF16)

    c8 = jnp.concatenate([c, c_ctx[None, :], jnp.zeros((8 - bsz - 1, d), F32)], axis=0)
    mod = _ada(c8, w_ada, b_ada).reshape(depth, 8, 3, d)

    lb_all = jnp.cumsum(jax.nn.softmax(hg_lb_logits.astype(F32), axis=1), axis=1)
    lb_all = lb_all - lb_all[:, :1]
    log_lb, log_1m_lb, one_m_lb = jnp.log(lb_all), jnp.log1p(-lb_all), 1.0 - lb_all

    c_ch, s_ch = _dft_cos_sin(FN_GW)
    cs_ch = jnp.concatenate([c_ch, s_ch], axis=-1).astype(BF16)
    dft = {}
    for tt in (t, t_ctx):
        ct, st = _dft_cos_sin(tt)
        dft[tt] = (ct.astype(BF16), (-st).astype(BF16))
    rope_l = _rope_tables(t)
    rope_c = tuple(a[:t_ctx] for a in rope_l)

    xl = x.reshape(bsz * t, d)
    xc = ctx.reshape(bsz * t_ctx, d)
    zstate = jnp.zeros((bsz, HEADS, HD, HD), F32)
    for l in range(depth):
        last = l == depth - 1
        g_l = norm_g[l][None, :]
        mod_l, mod_c = mod[l, :bsz], mod[l, bsz:bsz + 1]
        pl_ = _proj(xl, mod_l, g_l, w_in_r, l, t)
        pc_ = _proj(xc, mod_c, g_l, w_in_r, l, bsz * t_ctx)

        fn_b_l = fn_b[l][None, :]
        y_fn_l = _fourier_positions(_fourier_channels(pl_, cs_ch), pl_, *dft[t], fnw16, fn_b_l, l, t)

        dn_n, hg_n, da_n = dn_norm[l][None, :], hg_norm[l][None, :], da_norm[l][None, :]
        y_dn_c, s_f, s_b = _deltanet(pc_, dn_conv, l, dn_a_log[l], dn_dt_bias[l], dn_n,
                                     zstate, zstate, t_ctx)
        y_dn_l, _, _ = _deltanet(pl_, dn_conv, l, dn_a_log[l], dn_dt_bias[l], dn_n, s_f, s_b, t)

        lbs = (log_lb[:, l], log_1m_lb[:, l], one_m_lb[:, l])
        y_hg_c, h_f, h_b = _hgrn2(pc_, *lbs, hg_n, zstate, zstate, t_ctx)
        y_hg_l, _, _ = _hgrn2(pl_, *lbs, hg_n, h_f, h_b, t)

        lam_init = 0.8 - 0.6 * math.exp(-0.3 * l)
        lp = da_lambda[l].astype(F32)
        lam = (jnp.exp(jnp.sum(lp[0] * lp[1])) - jnp.exp(jnp.sum(lp[2] * lp[3])) + lam_init).reshape(1)
        y_da_l = _diff_attention(lam, pl_, pl_, pc_, rope_l, da_n, t, t_ctx, 1.0 - lam_init, True)

        fg = final_g[None, :]
        new_xl = _merge((y_fn_l, y_dn_l, y_hg_l, y_da_l), pl_, wb16, wo16, l, xl, mod_l, fg, t, last)
        if not last:
            y_fn_c = _fourier_positions(_fourier_channels(pc_, cs_ch), pc_, *dft[t_ctx], fnw16,
                                        fn_b_l, l, t_ctx)
            y_da_c = _diff_attention(lam, pc_, pc_, pc_, rope_c, da_n, t_ctx, t_ctx,
                                     1.0 - lam_init, False)
            xc = _merge((y_fn_c, y_dn_c, y_hg_c, y_da_c), pc_, wb16, wo16, l, xc, mod_c, fg,
                        bsz * t_ctx, False)
        xl = new_xl
    return xl.reshape(bsz, t, d)
```

```python
import functools
import math

import jax
import jax.numpy as jnp
from jax import lax
from jax.experimental import pallas as pl
from jax.experimental.pallas import tpu as pltpu

F32 = jnp.float32
BF16 = jnp.bfloat16

EPS = 1e-6
N_BRANCH = 4
HEADS = 4
HD = 128
BRANCH_W = HEADS * HD
FN_GW = 128
CHUNK = 64
SUB = 16
HALF = SUB // 2
GRID_W = 64
ROPE_THETA = 10000.0
DA_DH = 64
LOG2E = 1.4426950408889634

COL = dict(fn_u=64, fn_z=68, dn_q=72, dn_k=76, dn_v=80, dn_z=84, hg_q=88, hg_f0=92, hg_f1=96,
           hg_i=100, hg_z=104, da_q=108, da_k=112, da_v=116, da_z=120, dn_ab=124)
PROJ_W = 126 * 128
GATE_W_OFF = 7696
AB_OFF = 3072

MIB = 1024 * 1024


def _cparams(sem, vmem_mib):
    return pltpu.CompilerParams(dimension_semantics=sem, vmem_limit_bytes=vmem_mib * MIB)


def _silu(x):
    return x * jax.nn.sigmoid(x)


def _dot(a, b):
    return jnp.dot(a, b, preferred_element_type=F32)


def _dot_nt(a, b):
    return lax.dot_general(a, b, (((1,), (1,)), ((), ())), preferred_element_type=F32)


def _dot_tn(a, b):
    return lax.dot_general(a, b, (((0,), (0,)), ((), ())), preferred_element_type=F32)


def _split2(x):
    hi = x.astype(BF16)
    return hi, (x - hi.astype(F32)).astype(BF16)


def _split3(x):
    hi = x.astype(BF16)
    r = x - hi.astype(F32)
    mid = r.astype(BF16)
    return hi, mid, (r - mid.astype(F32)).astype(BF16)


def _dot3(a, b):
    a_hi, a_lo = _split2(a)
    b_hi, b_lo = _split2(b)
    return _dot(a_hi, b_hi) + (_dot(a_hi, b_lo) + _dot(a_lo, b_hi))


def _mask_dot(mask, x):
    hi, mid, lo = _split3(x)
    return _dot(mask, hi) + (_dot(mask, mid) + _dot(mask, lo))


def _dot_mask(x, mask):
    hi, mid, lo = _split3(x)
    return _dot(hi, mask) + (_dot(mid, mask) + _dot(lo, mask))


def _pick(n, cands):
    for c in cands:
        if n % c == 0:
            return c
    return n


def _ada_kernel(c_ref, w_ref, b_ref, o_ref):
    sc = _silu(c_ref[...])
    o_ref[0] = _dot(sc.astype(BF16), w_ref[0].astype(BF16)) + b_ref[0]


def _ada(c8, w_ada, b_ada):
    depth, d, d3 = w_ada.shape
    tn = _pick(d3, (1536, 768, 512, 256, 128))
    return pl.pallas_call(
        _ada_kernel,
        grid=(depth, d3 // tn),
        in_specs=[pl.BlockSpec((8, d), lambda l, n: (0, 0)),
                  pl.BlockSpec((1, d, tn), lambda l, n: (l, 0, n)),
                  pl.BlockSpec((1, 1, tn), lambda l, n: (l, 0, n))],
        out_specs=pl.BlockSpec((1, 8, tn), lambda l, n: (l, 0, n)),
        out_shape=jax.ShapeDtypeStruct((depth, 8, d3), F32),
        compiler_params=_cparams(("parallel", "parallel"), 48),
        name="adaln_mod",
    )(c8, w_ada, b_ada.reshape(depth, 1, d3))


def _transposed_w_in(w_in):
    depth, d, _ = w_in.shape
    n_ab = 4 * HEADS
    wt = jnp.swapaxes(w_in, 1, 2)
    return jnp.concatenate(
        [wt[:, GATE_W_OFF:], wt[:, :AB_OFF], wt[:, AB_OFF + n_ab:GATE_W_OFF], wt[:, AB_OFF:AB_OFF + n_ab],
         jnp.zeros((depth, 2 * HD - n_ab, d), w_in.dtype)], axis=1).astype(BF16)


def _proj_kernel(x_ref, mod_ref, g_ref, w_ref, o_ref, h_ref):
    @pl.when(pl.program_id(1) == 0)
    def _():
        x = x_ref[...]
        y = x * lax.rsqrt(jnp.mean(x * x, axis=-1, keepdims=True) + EPS) * g_ref[...]
        h = y * (1.0 + mod_ref[0, 1:2, :]) + mod_ref[0, 0:1, :]
        h_ref[...] = h.astype(BF16)

    o_ref[...] = _dot_nt(h_ref[...], w_ref[...])


def _proj(x2, mod, g, wt_all, layer, rows_per_seg):
    n, d = x2.shape
    tm = _pick(rows_per_seg, (1024, 512, 256, 128))
    tn = 1792
    return pl.pallas_call(
        _proj_kernel,
        grid=(n // tm, PROJ_W // tn),
        in_specs=[pl.BlockSpec((tm, d), lambda i, j: (i, 0), pipeline_mode=pl.Buffered(1)),
                  pl.BlockSpec((1, 3, d), lambda i, j: (i * tm // rows_per_seg, 0, 0)),
                  pl.BlockSpec((1, d), lambda i, j: (0, 0)),
                  pl.BlockSpec((None, tn, d), lambda i, j: (layer, j, 0))],
        out_specs=pl.BlockSpec((tm, tn), lambda i, j: (i, j)),
        out_shape=jax.ShapeDtypeStruct((n, PROJ_W), F32),
        scratch_shapes=[pltpu.VMEM((tm, d), BF16)],
        compiler_params=_cparams(("parallel", "arbitrary"), 56),
        name="in_proj",
    )(x2, mod, g, wt_all)


def _dft_cos_sin(n):
    j = jnp.arange(n, dtype=jnp.int32)
    sc = n ** -0.5
    if n <= 1024:
        ang = (2.0 * math.pi / n) * ((j[:, None] * j[None, :]) % n).astype(F32)
        return jnp.cos(ang) * sc, jnp.sin(ang) * sc
    m = n // 64
    k1 = jnp.arange(m, dtype=jnp.int32)
    k2 = jnp.arange(64, dtype=jnp.int32)
    a = (2.0 * math.pi / m) * ((j[:, None] * k1[None, :]) % m).astype(F32)
    b = (2.0 * math.pi / n) * ((j[:, None] * k2[None, :]) % n).astype(F32)
    ca, sa, cb, sb = jnp.cos(a), jnp.sin(a), jnp.cos(b), jnp.sin(b)
    c = ca[:, :, None] * cb[:, None, :] - sa[:, :, None] * sb[:, None, :]
    s = sa[:, :, None] * cb[:, None, :] + ca[:, :, None] * sb[:, None, :]
    return c.reshape(n, n) * sc, s.reshape(n, n) * sc


def _fn1_kernel(u_ref, cs_ref, o_ref):
    for g in range(BRANCH_W // FN_GW):
        r = _dot(u_ref[:, g * FN_GW:(g + 1) * FN_GW].astype(BF16), cs_ref[...])
        o_ref[:, g * FN_GW:(g + 1) * FN_GW] = r[:, :FN_GW].astype(BF16)
        o_ref[:, BRANCH_W + g * FN_GW:BRANCH_W + (g + 1) * FN_GW] = r[:, FN_GW:].astype(BF16)


def _fourier_channels(p, cs_ch):
    n = p.shape[0]
    tm = _pick(n, (1024, 512, 256, 128))
    return pl.pallas_call(
        _fn1_kernel,
        grid=(n // tm,),
        in_specs=[pl.BlockSpec((tm, BRANCH_W), lambda i: (i, COL["fn_u"] // 4)),
                  pl.BlockSpec((FN_GW, 2 * FN_GW), lambda i: (0, 0))],
        out_specs=pl.BlockSpec((tm, 2 * BRANCH_W), lambda i: (i, 0)),
        out_shape=jax.ShapeDtypeStruct((n, 2 * BRANCH_W), BF16),
        compiler_params=_cparams(("parallel",), 32),
        name="fourier_channels",
    )(p, cs_ch)


def _fn2_kernel(ct_ref, st_ref, r_ref, z_ref, w_ref, b_ref, y_ref):
    f = _dot(ct_ref[...], r_ref[:, :BRANCH_W]) + _dot(st_ref[...], r_ref[:, BRANCH_W:])
    y = _dot(f.astype(BF16), w_ref[...]) + b_ref[...]
    y_ref[...] = (y * _silu(z_ref[...])).astype(BF16)


def _fourier_positions(r, p, ct, nst, fn_w_all, fn_b, layer, t):
    n = p.shape[0]
    bsz = n // t
    tm = _pick(t, (512, 256, 128))
    nt = t // tm
    return pl.pallas_call(
        _fn2_kernel,
        grid=(bsz, nt),
        in_specs=[pl.BlockSpec((tm, t), lambda b, i: (i, 0)),
                  pl.BlockSpec((tm, t), lambda b, i: (i, 0)),
                  pl.BlockSpec((t, 2 * BRANCH_W), lambda b, i: (b, 0)),
                  pl.BlockSpec((tm, BRANCH_W), lambda b, i: (b * nt + i, COL["fn_z"] // 4)),
                  pl.BlockSpec((None, BRANCH_W, BRANCH_W), lambda b, i: (layer, 0, 0)),
                  pl.BlockSpec((1, BRANCH_W), lambda b, i: (0, 0))],
        out_specs=pl.BlockSpec((tm, BRANCH_W), lambda b, i: (b * nt + i, 0)),
        out_shape=jax.ShapeDtypeStruct((n, BRANCH_W), BF16),
        compiler_params=_cparams(("parallel", "arbitrary"), 48),
        name="fourier_positions",
    )(ct, nst, r, p, fn_w_all, fn_b)


def _seq_spec(rows, index_map, buffers=2):
    return pl.BlockSpec((rows, HD), index_map, pipeline_mode=pl.Buffered(buffers))


def _chunk_masks():
    i = lax.broadcasted_iota(jnp.int32, (CHUNK, CHUNK), 0)
    j = lax.broadcasted_iota(jnp.int32, (CHUNK, CHUNK), 1)
    incl = (i >= j, i <= j)
    strict = (i > j, i < j)
    return incl, strict


def _as_bf16_mask(m):
    return jnp.where(m, 1.0, 0.0).astype(BF16)


def _softplus(x):
    return jnp.maximum(x, 0.0) + _log1p_unit(jnp.exp(-jnp.abs(x)))


def _log1p_unit(x):
    return jnp.log(1.0 + x)


def _head_rms_gate(o, g_ref, z_ref):
    y = o * lax.rsqrt(jnp.mean(o * o, axis=-1, keepdims=True) + EPS) * g_ref[...]
    return y * _silu(z_ref[...])


def _aligned(x, m):
    return x if isinstance(x, int) else pl.multiple_of(x, m)


def _dn_parts(alog_ref, dtb_ref, q_ref, k_ref, v_ref, ab_ref, cq_ref, ck_ref, cv_ref,
              qs, ks, vs, m_s, n_s, qp_s, gl_s, o_s, group):
    h = pl.program_id(1)
    t = q_ref.shape[0]
    nchunk = t // CHUNK
    row = lax.broadcasted_iota(jnp.int32, (t, HD), 0)

    def conv(x_ref, c_ref):
        x = x_ref[...]
        xm = jnp.where(row == 0, 0.0, pltpu.roll(x, 1, 0))
        xp = jnp.where(row == t - 1, 0.0, pltpu.roll(x, t - 1, 0))
        y = xm * c_ref[0:1, :] + x * c_ref[1:2, :] + xp * c_ref[2:3, :]
        return _silu(y)

    q = conv(q_ref, cq_ref)
    qs[...] = q * lax.rsqrt(jnp.sum(q * q, axis=-1, keepdims=True) + EPS) * HD ** -0.5
    k = conv(k_ref, ck_ref)
    ks[...] = k * lax.rsqrt(jnp.sum(k * k, axis=-1, keepdims=True) + EPS)
    vs[...] = conv(v_ref, cv_ref)

    incl, _ = _chunk_masks()
    incl_b = tuple(_as_bf16_mask(m) for m in incl)
    ri = lax.broadcasted_iota(jnp.int32, (CHUNK, HD), 0)
    li = lax.broadcasted_iota(jnp.int32, (CHUNK, HD), 1)
    cj = li % CHUNK
    incl2 = (ri >= cj, ri <= cj)
    strict2 = (ri > cj, ri < cj)
    right = li >= CHUNK
    eye_right = jnp.where(li == ri + CHUNK, 1.0, 0.0)
    sel_r = lax.broadcasted_iota(jnp.int32, (HD, 2 * HD), 0)
    sel_c = lax.broadcasted_iota(jnp.int32, (HD, 2 * HD), 1)
    sels = [_as_bf16_mask(sel_r == (sel_c // HD) * 2 * HEADS + d * HEADS + h) for d in range(2)]
    neg_a = [-jnp.exp(jnp.full((1, HD), alog_ref[d, h], F32)) for d in range(2)]
    dtb = [dtb_ref[d, h] for d in range(2)]
    aligned = _aligned
    slot_of = lambda g: (g % 2) * group

    def step(d, ci, pos, s):
        rows = pl.ds(aligned(ci * CHUNK, CHUNK), CHUNK)
        prow = pl.ds(aligned(pos * CHUNK, CHUNK), CHUNK)
        mrows = pl.ds(aligned(pos * HD, HD), HD)
        s_b16 = s.astype(BF16)
        o_s[d, rows, :] = o_s[d, rows, :] + _dot(qp_s[d, prow, :], s_b16)
        g_last = gl_s[d, pl.ds(aligned(pos * 8, 8), 8), :][0:1, :]
        return s * g_last + (n_s[d, mrows, :] - _dot(m_s[d, mrows, :], s_b16))

    def prep(g, between):
        base = (g * group, nchunk - (g + 1) * group)
        rows_g = [pl.ds(aligned(b0 * CHUNK, CHUNK), group * CHUNK) for b0 in base]
        q_g = [qs[r, :] for r in rows_g]
        k_g = [ks[r, :] for r in rows_g]
        v_g = [vs[r, :] for r in rows_g]
        ab = [_dot_mask(ab_ref[rows_g[d], :], sels[d]) for d in range(2)]
        gb_all = [neg_a[d] * _softplus(ab[d][:, :HD] + dtb[d]) for d in range(2)]
        bt_all = [jax.nn.sigmoid(ab[d][:, HD:]) for d in range(2)]
        chains = [(c, d) for c in range(group) for d in range(2)]
        sl = lambda c: slice(c * CHUNK, (c + 1) * CHUNK)
        kcbs = [k_g[d][sl(c)].astype(BF16) for c, d in chains]
        kk2 = [_dot_nt(kb_, jnp.concatenate([kb_, kb_], axis=0)) for kb_ in kcbs]
        qk = [_dot_nt(q_g[d][sl(c)].astype(BF16), kb_) for (c, d), kb_ in zip(chains, kcbs)]
        gcs = [_mask_dot(incl_b[d], gb_all[d][sl(c)]) for c, d in chains]
        between(0)
        grs = [jnp.concatenate([gc, gc], axis=0).T[:CHUNK, :] for gc in gcs]
        decays = [jnp.where(incl2[d], jnp.exp(jnp.where(incl2[d], gc - gr, 0.0)), 0.0)
                  for (c, d), gc, gr in zip(chains, gcs, grs)]
        zs = [jnp.where(strict2[d] & ~right, -(kk * bt_all[d][sl(c)] * dec), 0.0) + eye_right
              for (c, d), kk, dec in zip(chains, kk2, decays)]
        for level in range(6):
            zs = [_dot3(z[:, :CHUNK], z) + jnp.where(right, z, 0.0) for z in zs]
            if level in (1, 3):
                between((level + 1) // 2)
        zero_rows = jnp.zeros((CHUNK, 2 * HD), BF16)
        egs = [jnp.exp(gc) for gc in gcs]
        g_lasts = [gc[CHUNK - 1:CHUNK, :] if d == 0 else gc[0:1, :] for (c, d), gc in zip(chains, gcs)]
        wus = [_dot(z.astype(BF16), jnp.concatenate(
                   [zero_rows,
                    jnp.concatenate([(k_g[d][sl(c)] * bt_all[d][sl(c)] * eg).astype(BF16),
                                     (v_g[d][sl(c)] * bt_all[d][sl(c)]).astype(BF16)], axis=1)],
                   axis=0)).astype(BF16)
               for (c, d), z, eg in zip(chains, zs, egs)]
        between(3)
        mns = [_dot_tn((k_g[d][sl(c)] * jnp.exp(gl - gc)).astype(BF16), wu)
               for (c, d), gc, gl, wu in zip(chains, gcs, g_lasts, wus)]
        qos = [_dot(jnp.where(incl[d], qk_ * dec[:, :CHUNK], 0.0).astype(BF16), wu)
               for (c, d), qk_, dec, wu in zip(chains, qk, decays, wus)]
        for (c, d), eg, gl, mn, qo in zip(chains, egs, g_lasts, mns, qos):
            ci, pos = base[d] + c, slot_of(g) + c
            mrows = pl.ds(aligned(pos * HD, HD), HD)
            m_s[d, mrows, :] = mn[:, :HD].astype(BF16)
            n_s[d, mrows, :] = mn[:, HD:]
            qp_s[d, pl.ds(aligned(pos * CHUNK, CHUNK), CHUNK), :] = (
                q_g[d][sl(c)] * eg - qo[:, :HD]).astype(BF16)
            o_s[d, pl.ds(aligned(ci * CHUNK, CHUNK), CHUNK), :] = qo[:, HD:]
            gl_s[d, pl.ds(aligned(pos * 8, 8), 8), :] = jnp.broadcast_to(jnp.exp(gl), (8, HD))

    def recurrence_steps(g, state):
        def between(u):
            if u < group:
                state[0] = step(0, g * group + u, slot_of(g) + u, state[0])
                state[1] = step(1, nchunk - 1 - g * group - u, slot_of(g) + group - 1 - u, state[1])
        return between

    return prep, recurrence_steps


def _hg_parts(q_ref, f_refs, i_ref, llb_ref, l1m_ref, oml_ref, o_s, group):
    t = q_ref.shape[0]
    nchunk = t // CHUNK
    nsub = CHUNK // SUB

    def gates(d, rows):
        f = f_refs[d][rows, :]
        e_f = jnp.exp(-jnp.abs(f))
        lsig = jnp.minimum(f, 0.0) - _log1p_unit(e_f)
        a = llb_ref[d:d + 1, :]
        b = l1m_ref[d:d + 1, :] + lsig
        log_f = jnp.maximum(a, b) + _log1p_unit(jnp.exp(-jnp.abs(a - b)))
        k = oml_ref[d:d + 1, :] * (jnp.where(f >= 0.0, e_f, 1.0) * (1.0 / (1.0 + e_f)))
        return _silu(q_ref[rows, :]), k, log_f, i_ref[rows, :]

    incl, _ = _chunk_masks()
    incl_b = tuple(_as_bf16_mask(m) for m in incl)
    sub_i = lax.broadcasted_iota(jnp.int32, (SUB, HD), 0)
    half_i = lax.broadcasted_iota(jnp.int32, (HALF, HD), 0)

    def factored(q_rows, g_rows, k_rows, gk_rows, ref, q_keep=None):
        q_t = q_rows * jnp.exp(jnp.minimum(g_rows - ref, 0.0))
        k_t = k_rows * jnp.exp(jnp.minimum(ref - gk_rows, 0.0))
        if q_keep is not None:
            q_t, k_t = jnp.where(q_keep, q_t, 0.0), jnp.where(q_keep, 0.0, k_t)
        return q_t.astype(BF16), k_t.astype(BF16)

    def trip(g, state):
        base = (g * group, nchunk - (g + 1) * group)
        work = {}

        def stage_gates():
            rows_g = [pl.ds(_aligned(b0 * CHUNK, CHUNK), group * CHUNK) for b0 in base]
            q_g, k_g, lf_g, v_g = zip(*[gates(d, rows_g[d]) for d in range(2)])
            chains = [(d, u if d == 0 else group - 1 - u) for u in range(group) for d in range(2)]
            o_inter, pair_acc, jobs = [], [], []
            for cidx, (d, pos) in enumerate(chains):
                chunk_body(cidx, d, slice(pos * CHUNK, (pos + 1) * CHUNK), q_g[d], k_g[d], lf_g[d], v_g[d],
                           state, o_inter, pair_acc, jobs)
            work.update(chains=chains, o_inter=o_inter, pair_acc=pair_acc, jobs=jobs)

        def stage_scores():
            work["scores"] = [_dot_nt(q_t, k_t).astype(BF16) for _, _, (q_t, k_t), _ in work["jobs"]]

        def stage_values():
            jobs = work["jobs"]
            outs = [_dot(a, v_rows) for a, (_, _, _, v_rows) in zip(work["scores"], jobs)]
            for cidx, (d, pos) in enumerate(work["chains"]):
                blocks = []
                for ib in range(nsub):
                    acc = jnp.concatenate(work["pair_acc"][cidx][2 * ib:2 * ib + 2], axis=0)
                    for o_job, (jc, jb, _, _) in zip(outs, jobs):
                        if (jc, jb) == (cidx, ib):
                            acc = acc + o_job
                    blocks.append(acc)
                rows = pl.ds(_aligned((base[d] + pos) * CHUNK, CHUNK), CHUNK)
                o_s[d, rows, :] = work["o_inter"][cidx] + jnp.concatenate(blocks, axis=0)

        return [stage_gates, stage_scores, stage_values]

    def chunk_body(cidx, d, sl, q_g, k_g, lf_g, v_g, state, o_inter, pair_acc, jobs):
        s_t = state[d]
        qc, kc, vc = q_g[sl], k_g[sl], v_g[sl]
        vb = vc.astype(BF16)
        gc = _mask_dot(incl_b[d], lf_g[sl])
        g_last = gc[CHUNK - 1:CHUNK, :] if d == 0 else gc[0:1, :]
        o_inter.append(_dot_nt((qc * jnp.exp(gc)).astype(BF16), s_t.astype(BF16)))
        k_dec = kc * jnp.exp(g_last - gc)
        state[d] = s_t * jnp.exp(g_last) + _dot_tn(vb, k_dec.astype(BF16))
        for ib in range(nsub):
            r0 = ib * SUB
            blk = slice(r0, r0 + SUB)
            mid = r0 + HALF
            ref = gc[mid - 1:mid, :] if d == 0 else gc[mid:mid + 1, :]
            q_keep = (sub_i >= HALF) if d == 0 else (sub_i < HALF)
            jobs.append((cidx, ib, factored(qc[blk], gc[blk], kc[blk], gc[blk], ref, q_keep), vb[blk]))
            if d == 0 and ib > 0:
                rng, ref = slice(0, r0), gc[r0 - 1:r0, :]
            elif d == 1 and ib < nsub - 1:
                rng, ref = slice(r0 + SUB, CHUNK), gc[r0 + SUB:r0 + SUB + 1, :]
            else:
                continue
            jobs.append((cidx, ib, factored(qc[blk], gc[blk], kc[rng], gc[rng], ref), vb[rng]))
        halves = []
        for h0 in range(0, CHUNK, HALF):
            q_h, g_h = qc[h0:h0 + HALF, :], gc[h0:h0 + HALF, :]
            acc_h = jnp.zeros((HALF, HD), F32)
            for jj in range(HALF):
                j = h0 + jj
                term = q_h * kc[j:j + 1, :] * jnp.exp(jnp.minimum(g_h - gc[j:j + 1, :], 0.0))
                keep = (half_i >= jj) if d == 0 else (half_i <= jj)
                acc_h = acc_h + (jnp.sum(jnp.where(keep, term, 0.0), axis=-1, keepdims=True)
                                 * vc[j:j + 1, :])
            halves.append(acc_h)
        pair_acc.append(halves)

    return trip


def _recur_kernel(alog_ref, dtb_ref,
                  dq_ref, dk_ref, dv_ref, dz_ref, dab_ref, cq_ref, ck_ref, cv_ref, dng_ref,
                  ds0f_ref, ds0b_ref,
                  hq_ref, hf0_ref, hf1_ref, hi_ref, hz_ref, llb_ref, l1m_ref, oml_ref, hng_ref,
                  hs0f_ref, hs0b_ref,
                  ydn_ref, dsf_ref, dsb_ref, yhg_ref, hsf_ref, hsb_ref,
                  qs, ks, vs, m_s, n_s, qp_s, gl_s, do_s, ho_s, *, group):
    ngroup = (dq_ref.shape[0] // CHUNK) // group
    dn_prep, dn_steps = _dn_parts(alog_ref, dtb_ref, dq_ref, dk_ref, dv_ref, dab_ref, cq_ref, ck_ref,
                                  cv_ref, qs, ks, vs, m_s, n_s, qp_s, gl_s, do_s, group)
    hg_trip = _hg_parts(hq_ref, (hf0_ref, hf1_ref), hi_ref, llb_ref, l1m_ref, oml_ref, ho_s, group)

    def run_trip(g, dn_state, hg_state):
        recur = dn_steps(g - 1, dn_state) if dn_state is not None else (lambda u: None)
        stages = hg_trip(g, hg_state)

        def between(u):
            recur(u)
            if u < len(stages):
                stages[u]()
        dn_prep(g, between)

    hg_state = [hs0f_ref[0, 0], hs0b_ref[0, 0]]
    run_trip(0, None, hg_state)

    def body(g, carry):
        dn_state, hg_st = list(carry[:2]), list(carry[2:])
        run_trip(g, dn_state, hg_st)
        return tuple(dn_state) + tuple(hg_st)

    carry = lax.fori_loop(1, ngroup, body, (ds0f_ref[0, 0], ds0b_ref[0, 0]) + tuple(hg_state))
    dn_state = list(carry[:2])
    last = dn_steps(ngroup - 1, dn_state)
    for u in range(group):
        last(u)
    dsf_ref[0, 0], dsb_ref[0, 0] = dn_state
    hsf_ref[0, 0], hsb_ref[0, 0] = carry[2], carry[3]
    ydn_ref[...] = _head_rms_gate(do_s[0] + do_s[1], dng_ref, dz_ref).astype(BF16)
    yhg_ref[...] = _head_rms_gate(ho_s[0] + ho_s[1], hng_ref, hz_ref).astype(BF16)


def _recurrent_mixers(p, conv_all, layer, a_log, dt_bias, dn_norm, hg_lbs, hg_norm, dn_s0, hg_s0, t):
    n = p.shape[0]
    bsz = n // t
    nchunk = t // CHUNK
    group = _pick(nchunk, (4, 2, 1))
    col = lambda name, bufs=1: _seq_spec(t, lambda b, h: (b, COL[name] + h), bufs)
    state = pl.BlockSpec((1, 1, HD, HD), lambda b, h: (b, h, 0, 0))
    smem = pl.BlockSpec(memory_space=pltpu.SMEM)
    conv = lambda off: pl.BlockSpec((None, 3, HD), lambda b, h: (layer, 0, off + h))
    lbs = pl.BlockSpec((2, HD), lambda b, h: (0, h))
    gain = pl.BlockSpec((1, HD), lambda b, h: (0, 0))
    y_spec = pl.BlockSpec((t, HD), lambda b, h: (b, h))
    y_shape = jax.ShapeDtypeStruct((n, BRANCH_W), BF16)
    st_shape = jax.ShapeDtypeStruct((bsz, HEADS, HD, HD), F32)
    seq_f32 = pltpu.VMEM((t, HD), F32)
    dir_f32 = pltpu.VMEM((2, t, HD), F32)
    slots = 2 * group
    return pl.pallas_call(
        functools.partial(_recur_kernel, group=group),
        grid=(bsz, HEADS),
        in_specs=[smem, smem,
                  col("dn_q", 2), col("dn_k", 2), col("dn_v", 2), col("dn_z"),
                  _seq_spec(t, lambda b, h: (b, COL["dn_ab"]), 1),
                  conv(0), conv(HEADS), conv(2 * HEADS), gain, state, state,
                  col("hg_q"), col("hg_f0"), col("hg_f1"), col("hg_i"), col("hg_z"),
                  lbs, lbs, lbs, gain, state, state],
        out_specs=[y_spec, state, state, y_spec, state, state],
        out_shape=[y_shape, st_shape, st_shape, y_shape, st_shape, st_shape],
        scratch_shapes=[seq_f32, seq_f32, seq_f32,
                        pltpu.VMEM((2, slots * HD, HD), BF16), pltpu.VMEM((2, slots * HD, HD), F32),
                        pltpu.VMEM((2, slots * CHUNK, HD), BF16), pltpu.VMEM((2, slots * 8, HD), F32),
                        dir_f32, dir_f32],
        compiler_params=_cparams(("parallel", "parallel"), 58),
        name="recurrent_mixers",
    )(a_log, dt_bias, p, p, p, p, p, conv_all, conv_all, conv_all, dn_norm, *dn_s0,
      p, p, p, p, p, *hg_lbs, hg_norm, *hg_s0)


def _rope_tables(t):
    pos = jnp.arange(t)
    row = (pos // GRID_W).astype(F32)
    col = (pos % GRID_W).astype(F32)
    n = DA_DH // 4
    inv = ROPE_THETA ** (-jnp.arange(n, dtype=F32) / n)
    ar, ac = row[:, None] * inv, col[:, None] * inv
    zero = jnp.zeros_like(ar)
    cos = jnp.concatenate([jnp.cos(ar), jnp.cos(ar), jnp.cos(ac), jnp.cos(ac)], axis=-1)
    s_up = jnp.concatenate([zero, jnp.sin(ar), zero, jnp.sin(ac)], axis=-1)
    s_dn = jnp.concatenate([-jnp.sin(ar), zero, -jnp.sin(ac), zero], axis=-1)
    tile = lambda a: jnp.concatenate([a, a], axis=-1)
    return tile(cos), tile(s_up), tile(s_dn)


def _rope(x, cos, s_up, s_dn):
    half = DA_DH // 4
    return x * cos + pltpu.roll(x, half, 1) * s_up + pltpu.roll(x, HD - half, 1) * s_dn


def _da_kernel(lam_ref, q_ref, kl_ref, kc_ref, vl_ref, vc_ref, z_ref, cq_ref, uq_ref, dq_ref,
               ck_ref, uk_ref, dk_ref, ng_ref, y_ref, k_s, vt_s, s_s, *, t_lat, out_scale):
    nkb, kblk, _ = k_s.shape
    nlat = t_lat // kblk

    @pl.when(pl.program_id(2) == 0)
    def _():
        for j in range(nkb):
            if j < nlat:
                rows = slice(j * kblk, (j + 1) * kblk)
                k = _rope(kl_ref[rows, :], ck_ref[rows, :], uk_ref[rows, :], dk_ref[rows, :])
                v = vl_ref[rows, :]
            else:
                rows = slice((j - nlat) * kblk, (j - nlat + 1) * kblk)
                k, v = kc_ref[rows, :], vc_ref[rows, :]
            k_s[j] = k.astype(BF16)
            vt_s[j] = v.T.astype(BF16)

    q = q_ref[...]
    if t_lat:
        q = _rope(q, cq_ref[...], uq_ref[...], dq_ref[...])
    q = q * (DA_DH ** -0.5 * LOG2E)
    lane = lax.broadcasted_iota(jnp.int32, q.shape, 1)
    qms = [jnp.where(lane < DA_DH, q, 0.0).astype(BF16), jnp.where(lane >= DA_DH, q, 0.0).astype(BF16)]
    tq = q.shape[0]
    m = [None, None]
    l = [jnp.zeros((1, tq), F32) for _ in range(2)]
    acc = [jnp.zeros((HD, tq), F32) for _ in range(2)]

    def scores(mp, j):
        s = _dot_nt(k_s[j], qms[mp])
        s_s[mp, j] = s
        bm = jnp.max(s, axis=0, keepdims=True)
        m[mp] = bm if m[mp] is None else jnp.maximum(m[mp], bm)

    def values(mp, j):
        e = jnp.exp2(s_s[mp, j] - m[mp])
        l[mp] = l[mp] + jnp.sum(e, axis=0, keepdims=True)
        acc[mp] = acc[mp] + _dot(vt_s[j], e.astype(BF16))

    for j in range(nkb):
        scores(0, j)
    for j in range(nkb):
        scores(1, j)
        values(0, j)
    for j in range(nkb):
        values(1, j)
    o = (acc[0] * (1.0 / l[0]) - acc[1] * (lam_ref[0] * (1.0 / l[1]))).T
    y = o * lax.rsqrt(jnp.mean(o * o, axis=-1, keepdims=True) + EPS) * ng_ref[...]
    y_ref[...] = (y * out_scale * _silu(z_ref[...])).astype(BF16)


def _diff_attention(lam, p_q, p_lat, p_ctx, rope, norm_g, t, t_ctx, out_scale, latent):
    n = p_q.shape[0]
    bsz = n // t
    tq = _pick(t, (512, 256, 128))
    nq = t // tq
    t_lat = t if latent else 0
    t_kl = t if latent else t_ctx
    cos, s_up, s_dn = rope
    kblk = 256 if (t_lat % 256 == 0 and t_ctx % 256 == 0) else 128
    nkb = (t_lat + t_ctx) // kblk
    col = lambda name: (lambda b, h, i: (b, COL[name] + h))
    qcol = lambda name: (lambda b, h, i: (b * nq + i, COL[name] + h))
    tab_q = pl.BlockSpec((tq, HD), lambda b, h, i: (i, 0))
    tab_k = _seq_spec(t_kl, lambda b, h, i: (0, 0), 1)
    return pl.pallas_call(
        functools.partial(_da_kernel, t_lat=t_lat, out_scale=out_scale),
        grid=(bsz, HEADS, nq),
        in_specs=[pl.BlockSpec(memory_space=pltpu.SMEM),
                  pl.BlockSpec((tq, HD), qcol("da_q")),
                  _seq_spec(t_kl, col("da_k")), _seq_spec(t_ctx, col("da_k")),
                  _seq_spec(t_kl, col("da_v")), _seq_spec(t_ctx, col("da_v")),
                  pl.BlockSpec((tq, HD), qcol("da_z")),
                  tab_q, tab_q, tab_q, tab_k, tab_k, tab_k,
                  pl.BlockSpec((1, HD), lambda b, h, i: (0, 0))],
        out_specs=pl.BlockSpec((tq, HD), lambda b, h, i: (b * nq + i, h)),
        out_shape=jax.ShapeDtypeStruct((n, BRANCH_W), BF16),
        scratch_shapes=[pltpu.VMEM((nkb, kblk, HD), BF16), pltpu.VMEM((nkb, HD, kblk), BF16),
                        pltpu.VMEM((2, nkb, kblk, tq), F32)],
        compiler_params=_cparams(("parallel", "parallel", "arbitrary"), 56),
        name="diff_attention",
    )(lam, p_q, p_lat, p_ctx, p_lat, p_ctx, p_q, cos, s_up, s_dn, cos, s_up, s_dn, norm_g)


def _merge_kernel(y0_ref, y1_ref, y2_ref, y3_ref, gl_ref, wb_ref, wo_ref, x_ref, mod_ref, fg_ref,
                  o_ref, *, final):
    d = x_ref.shape[1]
    acc = None
    for k, y_ref in enumerate((y0_ref, y1_ref, y2_ref, y3_ref)):
        c = jax.nn.sigmoid(gl_ref[:, k * d:(k + 1) * d]) * _dot(y_ref[...], wb_ref[k])
        acc = c if acc is None else acc + c
    x = x_ref[...] + mod_ref[0, 2:3, :] * _dot(acc.astype(BF16), wo_ref[...])
    if final:
        x = x * lax.rsqrt(jnp.mean(x * x, axis=-1, keepdims=True) + EPS) * fg_ref[...]
    o_ref[...] = x


def _merge(ys, p, wb_all, wo_all, layer, x2, mod, final_g, rows_per_seg, final):
    n, d = x2.shape
    tm = _pick(rows_per_seg, (256, 128))
    ysp = pl.BlockSpec((tm, BRANCH_W), lambda i: (i, 0))
    return pl.pallas_call(
        functools.partial(_merge_kernel, final=final),
        grid=(n // tm,),
        in_specs=[ysp, ysp, ysp, ysp,
                  pl.BlockSpec((tm, N_BRANCH * d), lambda i: (i, 0)),
                  pl.BlockSpec((None, N_BRANCH, BRANCH_W, d), lambda i: (layer, 0, 0, 0),
                               pipeline_mode=pl.Buffered(1)),
                  pl.BlockSpec((None, d, d), lambda i: (layer, 0, 0), pipeline_mode=pl.Buffered(1)),
                  pl.BlockSpec((tm, d), lambda i: (i, 0)),
                  pl.BlockSpec((1, 3, d), lambda i: (i * tm // rows_per_seg, 0, 0)),
                  pl.BlockSpec((1, d), lambda i: (0, 0))],
        out_specs=pl.BlockSpec((tm, d), lambda i: (i, 0)),
        out_shape=jax.ShapeDtypeStruct((n, d), F32),
        compiler_params=_cparams(("parallel",), 56),
        name="merge",
    )(*ys, p, wb_all, wo_all, x2, mod, final_g)


def kernel(x, c, ctx, c_ctx, norm_g, w_ada, b_ada, w_in, fn_w, fn_b, dn_conv, dn_a_log, dn_dt_bias,
           dn_norm, hg_lb_logits, hg_norm, da_lambda, da_norm, w_branch, w_out, final_g):
    bsz, t, d = x.shape
    t_ctx = ctx.shape[1]
    depth = w_in.shape[0]
    assert d == N_BRANCH * BRANCH_W and t % CHUNK == 0 and t_ctx % CHUNK == 0 and bsz + 1 <= 8

    w_in_r = _transposed_w_in(w_in)
    wb16, wo16, fnw16 = w_branch.astype(BF16), w_out.astype(BF16), fn_w.astype(BF16)

    c8 = jnp.concatenate([c, c_ctx[None, :], jnp.zeros((8 - bsz - 1, d), F32)], axis=0)
    mod = _ada(c8, w_ada, b_ada).reshape(depth, 8, 3, d)

    lb_all = jnp.cumsum(jax.nn.softmax(hg_lb_logits.astype(F32), axis=1), axis=1)
    lb_all = lb_all - lb_all[:, :1]
    log_lb, log_1m_lb, one_m_lb = jnp.log(lb_all), jnp.log1p(-lb_all), 1.0 - lb_all

    c_ch, s_ch = _dft_cos_sin(FN_GW)
    cs_ch = jnp.concatenate([c_ch, s_ch], axis=-1).astype(BF16)
    dft = {}
    for tt in (t, t_ctx):
        ct, st = _dft_cos_sin(tt)
        dft[tt] = (ct.astype(BF16), (-st).astype(BF16))
    rope_l = _rope_tables(t)
    rope_c = tuple(a[:t_ctx] for a in rope_l)

    xl = x.reshape(bsz * t, d)
    xc = ctx.reshape(bsz * t_ctx, d)
    zstate = jnp.zeros((bsz, HEADS, HD, HD), F32)
    for l in range(depth):
        last = l == depth - 1
        g_l = norm_g[l][None, :]
        mod_l, mod_c = mod[l, :bsz], mod[l, bsz:bsz + 1]
        pl_ = _proj(xl, mod_l, g_l, w_in_r, l, t)
        pc_ = _proj(xc, mod_c, g_l, w_in_r, l, bsz * t_ctx)

        fn_b_l = fn_b[l][None, :]
        y_fn_l = _fourier_positions(_fourier_channels(pl_, cs_ch), pl_, *dft[t], fnw16, fn_b_l, l, t)

        dn_n, hg_n, da_n = dn_norm[l][None, :], hg_norm[l][None, :], da_norm[l][None, :]
        lbs = (log_lb[:, l], log_1m_lb[:, l], one_m_lb[:, l])
        rec_args = (dn_conv, l, dn_a_log[l], dn_dt_bias[l], dn_n, lbs, hg_n)
        y_dn_c, s_f, s_b, y_hg_c, h_f, h_b = _recurrent_mixers(
            pc_, *rec_args, (zstate, zstate), (zstate, zstate), t_ctx)
        y_dn_l, _, _, y_hg_l, _, _ = _recurrent_mixers(pl_, *rec_args, (s_f, s_b), (h_f, h_b), t)

        lam_init = 0.8 - 0.6 * math.exp(-0.3 * l)
        lp = da_lambda[l].astype(F32)
        lam = (jnp.exp(jnp.sum(lp[0] * lp[1])) - jnp.exp(jnp.sum(lp[2] * lp[3])) + lam_init).reshape(1)
        y_da_l = _diff_attention(lam, pl_, pl_, pc_, rope_l, da_n, t, t_ctx, 1.0 - lam_init, True)

        fg = final_g[None, :]
        new_xl = _merge((y_fn_l, y_dn_l, y_hg_l, y_da_l), pl_, wb16, wo16, l, xl, mod_l, fg, t, last)
        if not last:
            y_fn_c = _fourier_positions(_fourier_channels(pc_, cs_ch), pc_, *dft[t_ctx], fnw16,
                                        fn_b_l, l, t_ctx)
            y_da_c = _diff_attention(lam, pc_, pc_, pc_, rope_c, da_n, t_ctx, t_ctx,
                                     1.0 - lam_init, False)
            xc = _merge((y_fn_c, y_dn_c, y_hg_c, y_da_c), pc_, wb16, wo16, l, xc, mod_c, fg,
                        bsz * t_ctx, False)
        xl = new_xl
    return xl.reshape(bsz, t, d)
```

```python
import functools
import math

import jax
import jax.numpy as jnp
from jax import lax
from jax.experimental import pallas as pl
from jax.experimental.pallas import tpu as pltpu

F32 = jnp.float32
BF16 = jnp.bfloat16

EPS = 1e-6
N_BRANCH = 4
HEADS = 4
HD = 128
BRANCH_W = HEADS * HD
FN_GW = 128
CHUNK = 64
SUB = 16
HALF = SUB // 2
GRID_W = 64
ROPE_THETA = 10000.0
DA_DH = 64
LOG2E = 1.4426950408889634

COL = dict(fn_u=64, fn_z=68, dn_q=72, dn_k=76, dn_v=80, dn_z=84, hg_q=88, hg_f0=92, hg_f1=96,
           hg_i=100, hg_z=104, da_q=108, da_k=112, da_v=116, da_z=120, dn_ab=124)
PROJ_W = 126 * 128
GATE_W_OFF = 7696
AB_OFF = 3072

MIB = 1024 * 1024


def _cparams(sem, vmem_mib):
    return pltpu.CompilerParams(dimension_semantics=sem, vmem_limit_bytes=vmem_mib * MIB)


def _silu(x):
    return x * jax.nn.sigmoid(x)


def _dot(a, b):
    return jnp.dot(a, b, preferred_element_type=F32)


def _dot_nt(a, b):
    return lax.dot_general(a, b, (((1,), (1,)), ((), ())), preferred_element_type=F32)


def _dot_tn(a, b):
    return lax.dot_general(a, b, (((0,), (0,)), ((), ())), preferred_element_type=F32)


def _split2(x):
    hi = x.astype(BF16)
    return hi, (x - hi.astype(F32)).astype(BF16)


def _split3(x):
    hi = x.astype(BF16)
    r = x - hi.astype(F32)
    mid = r.astype(BF16)
    return hi, mid, (r - mid.astype(F32)).astype(BF16)


def _dot3(a, b):
    a_hi, a_lo = _split2(a)
    b_hi, b_lo = _split2(b)
    return _dot(a_hi, b_hi) + (_dot(a_hi, b_lo) + _dot(a_lo, b_hi))


def _mask_dot(mask, x):
    hi, mid, lo = _split3(x)
    return _dot(mask, hi) + (_dot(mask, mid) + _dot(mask, lo))


def _dot_mask(x, mask):
    hi, mid, lo = _split3(x)
    return _dot(hi, mask) + (_dot(mid, mask) + _dot(lo, mask))


def _pick(n, cands):
    for c in cands:
        if n % c == 0:
            return c
    return n


def _ada_kernel(c_ref, w_ref, b_ref, o_ref):
    sc = _silu(c_ref[...])
    o_ref[0] = _dot(sc.astype(BF16), w_ref[0].astype(BF16)) + b_ref[0]


def _ada(c8, w_ada, b_ada):
    depth, d, d3 = w_ada.shape
    tn = _pick(d3, (1536, 768, 512, 256, 128))
    return pl.pallas_call(
        _ada_kernel,
        grid=(depth, d3 // tn),
        in_specs=[pl.BlockSpec((8, d), lambda l, n: (0, 0)),
                  pl.BlockSpec((1, d, tn), lambda l, n: (l, 0, n)),
                  pl.BlockSpec((1, 1, tn), lambda l, n: (l, 0, n))],
        out_specs=pl.BlockSpec((1, 8, tn), lambda l, n: (l, 0, n)),
        out_shape=jax.ShapeDtypeStruct((depth, 8, d3), F32),
        compiler_params=_cparams(("parallel", "parallel"), 48),
        name="adaln_mod",
    )(c8, w_ada, b_ada.reshape(depth, 1, d3))


def _transposed_w_in(w_in):
    depth, d, _ = w_in.shape
    n_ab = 4 * HEADS
    wt = jnp.swapaxes(w_in, 1, 2)
    return jnp.concatenate(
        [wt[:, GATE_W_OFF:], wt[:, :AB_OFF], wt[:, AB_OFF + n_ab:GATE_W_OFF], wt[:, AB_OFF:AB_OFF + n_ab],
         jnp.zeros((depth, 2 * HD - n_ab, d), w_in.dtype)], axis=1).astype(BF16)


def _proj_kernel(x_ref, mod_ref, g_ref, w_ref, o_ref, h_ref):
    @pl.when(pl.program_id(1) == 0)
    def _():
        x = x_ref[...]
        y = x * lax.rsqrt(jnp.mean(x * x, axis=-1, keepdims=True) + EPS) * g_ref[...]
        h = y * (1.0 + mod_ref[0, 1:2, :]) + mod_ref[0, 0:1, :]
        h_ref[...] = h.astype(BF16)

    o_ref[...] = _dot_nt(h_ref[...], w_ref[...])


def _proj(x2, mod, g, wt_all, layer, rows_per_seg):
    n, d = x2.shape
    tm = _pick(rows_per_seg, (1024, 512, 256, 128))
    tn = 1792
    return pl.pallas_call(
        _proj_kernel,
        grid=(n // tm, PROJ_W // tn),
        in_specs=[pl.BlockSpec((tm, d), lambda i, j: (i, 0), pipeline_mode=pl.Buffered(1)),
                  pl.BlockSpec((1, 3, d), lambda i, j: (i * tm // rows_per_seg, 0, 0)),
                  pl.BlockSpec((1, d), lambda i, j: (0, 0)),
                  pl.BlockSpec((None, tn, d), lambda i, j: (layer, j, 0))],
        out_specs=pl.BlockSpec((tm, tn), lambda i, j: (i, j)),
        out_shape=jax.ShapeDtypeStruct((n, PROJ_W), F32),
        scratch_shapes=[pltpu.VMEM((tm, d), BF16)],
        compiler_params=_cparams(("parallel", "arbitrary"), 56),
        name="in_proj",
    )(x2, mod, g, wt_all)


def _dft_cos_sin(n):
    j = jnp.arange(n, dtype=jnp.int32)
    sc = n ** -0.5
    if n <= 1024:
        ang = (2.0 * math.pi / n) * ((j[:, None] * j[None, :]) % n).astype(F32)
        return jnp.cos(ang) * sc, jnp.sin(ang) * sc
    m = n // 64
    k1 = jnp.arange(m, dtype=jnp.int32)
    k2 = jnp.arange(64, dtype=jnp.int32)
    a = (2.0 * math.pi / m) * ((j[:, None] * k1[None, :]) % m).astype(F32)
    b = (2.0 * math.pi / n) * ((j[:, None] * k2[None, :]) % n).astype(F32)
    ca, sa, cb, sb = jnp.cos(a), jnp.sin(a), jnp.cos(b), jnp.sin(b)
    c = ca[:, :, None] * cb[:, None, :] - sa[:, :, None] * sb[:, None, :]
    s = sa[:, :, None] * cb[:, None, :] + ca[:, :, None] * sb[:, None, :]
    return c.reshape(n, n) * sc, s.reshape(n, n) * sc


def _fn_kernel(ct_ref, st_ref, u_ref, cs_ref, z_ref, w_ref, b_ref, y_ref, uc_s, us_s):
    @pl.when(pl.program_id(1) == 0)
    def _():
        for g in range(BRANCH_W // FN_GW):
            cols = slice(g * FN_GW, (g + 1) * FN_GW)
            r = _dot(u_ref[:, cols].astype(BF16), cs_ref[...])
            uc_s[:, cols] = r[:, :FN_GW].astype(BF16)
            us_s[:, cols] = r[:, FN_GW:].astype(BF16)

    f = _dot(ct_ref[...], uc_s[...]) + _dot(st_ref[...], us_s[...])
    y = _dot(f.astype(BF16), w_ref[...]) + b_ref[...]
    y_ref[...] = (y * _silu(z_ref[...])).astype(BF16)


def _fourier(p, cs_ch, ct, nst, fn_w_all, fn_b, layer, t):
    n = p.shape[0]
    bsz = n // t
    tm = _pick(t, (512, 256, 128))
    nt = t // tm
    return pl.pallas_call(
        _fn_kernel,
        grid=(bsz, nt),
        in_specs=[pl.BlockSpec((tm, t), lambda b, i: (i, 0)),
                  pl.BlockSpec((tm, t), lambda b, i: (i, 0)),
                  pl.BlockSpec((t, BRANCH_W), lambda b, i: (b, COL["fn_u"] // 4),
                               pipeline_mode=pl.Buffered(1)),
                  pl.BlockSpec((FN_GW, 2 * FN_GW), lambda b, i: (0, 0)),
                  pl.BlockSpec((tm, BRANCH_W), lambda b, i: (b * nt + i, COL["fn_z"] // 4)),
                  pl.BlockSpec((None, BRANCH_W, BRANCH_W), lambda b, i: (layer, 0, 0)),
                  pl.BlockSpec((1, BRANCH_W), lambda b, i: (0, 0))],
        out_specs=pl.BlockSpec((tm, BRANCH_W), lambda b, i: (b * nt + i, 0)),
        out_shape=jax.ShapeDtypeStruct((n, BRANCH_W), BF16),
        scratch_shapes=[pltpu.VMEM((t, BRANCH_W), BF16), pltpu.VMEM((t, BRANCH_W), BF16)],
        compiler_params=_cparams(("parallel", "arbitrary"), 48),
        name="fourier",
    )(ct, nst, p, cs_ch, p, fn_w_all, fn_b)


def _seq_spec(rows, index_map, buffers=2):
    return pl.BlockSpec((rows, HD), index_map, pipeline_mode=pl.Buffered(buffers))


def _chunk_masks():
    i = lax.broadcasted_iota(jnp.int32, (CHUNK, CHUNK), 0)
    j = lax.broadcasted_iota(jnp.int32, (CHUNK, CHUNK), 1)
    incl = (i >= j, i <= j)
    strict = (i > j, i < j)
    return incl, strict


def _as_bf16_mask(m):
    return jnp.where(m, 1.0, 0.0).astype(BF16)


def _softplus(x):
    return jnp.maximum(x, 0.0) + _log1p_unit(jnp.exp(-jnp.abs(x)))


def _log1p_unit(x):
    return jnp.log(1.0 + x)


def _head_rms_gate(o, g_ref, z_ref):
    y = o * lax.rsqrt(jnp.mean(o * o, axis=-1, keepdims=True) + EPS) * g_ref[...]
    return y * _silu(z_ref[...])


def _aligned(x, m):
    return x if isinstance(x, int) else pl.multiple_of(x, m)


def _dn_parts(alog_ref, dtb_ref, q_ref, k_ref, v_ref, ab_ref, cq_ref, ck_ref, cv_ref,
              qs, ks, vs, m_s, n_s, qp_s, gl_s, o_s, group):
    h = pl.program_id(1)
    t = q_ref.shape[0]
    nchunk = t // CHUNK
    row = lax.broadcasted_iota(jnp.int32, (t, HD), 0)

    def conv(x_ref, c_ref):
        x = x_ref[...]
        xm = jnp.where(row == 0, 0.0, pltpu.roll(x, 1, 0))
        xp = jnp.where(row == t - 1, 0.0, pltpu.roll(x, t - 1, 0))
        y = xm * c_ref[0:1, :] + x * c_ref[1:2, :] + xp * c_ref[2:3, :]
        return _silu(y)

    q = conv(q_ref, cq_ref)
    qs[...] = q * lax.rsqrt(jnp.sum(q * q, axis=-1, keepdims=True) + EPS) * HD ** -0.5
    k = conv(k_ref, ck_ref)
    ks[...] = k * lax.rsqrt(jnp.sum(k * k, axis=-1, keepdims=True) + EPS)
    vs[...] = conv(v_ref, cv_ref)

    incl, _ = _chunk_masks()
    incl_b = tuple(_as_bf16_mask(m) for m in incl)
    ri = lax.broadcasted_iota(jnp.int32, (CHUNK, HD), 0)
    li = lax.broadcasted_iota(jnp.int32, (CHUNK, HD), 1)
    cj = li % CHUNK
    incl2 = (ri >= cj, ri <= cj)
    strict2 = (ri > cj, ri < cj)
    right = li >= CHUNK
    eye_right = jnp.where(li == ri + CHUNK, 1.0, 0.0)
    sel_r = lax.broadcasted_iota(jnp.int32, (HD, 2 * HD), 0)
    sel_c = lax.broadcasted_iota(jnp.int32, (HD, 2 * HD), 1)
    sels = [_as_bf16_mask(sel_r == (sel_c // HD) * 2 * HEADS + d * HEADS + h) for d in range(2)]
    neg_a = [-jnp.exp(jnp.full((1, HD), alog_ref[d, h], F32)) for d in range(2)]
    dtb = [dtb_ref[d, h] for d in range(2)]
    aligned = _aligned
    slot_of = lambda g: (g % 2) * group

    def step(d, ci, pos, s):
        rows = pl.ds(aligned(ci * CHUNK, CHUNK), CHUNK)
        prow = pl.ds(aligned(pos * CHUNK, CHUNK), CHUNK)
        mrows = pl.ds(aligned(pos * HD, HD), HD)
        s_b16 = s.astype(BF16)
        o_s[d, rows, :] = o_s[d, rows, :] + _dot(qp_s[d, prow, :], s_b16)
        g_last = gl_s[d, pl.ds(aligned(pos * 8, 8), 8), :][0:1, :]
        return s * g_last + (n_s[d, mrows, :] - _dot(m_s[d, mrows, :], s_b16))

    def prep(g, between):
        base = (g * group, nchunk - (g + 1) * group)
        rows_g = [pl.ds(aligned(b0 * CHUNK, CHUNK), group * CHUNK) for b0 in base]
        q_g = [qs[r, :] for r in rows_g]
        k_g = [ks[r, :] for r in rows_g]
        v_g = [vs[r, :] for r in rows_g]
        ab = [_dot_mask(ab_ref[rows_g[d], :], sels[d]) for d in range(2)]
        gb_all = [neg_a[d] * _softplus(ab[d][:, :HD] + dtb[d]) for d in range(2)]
        bt_all = [jax.nn.sigmoid(ab[d][:, HD:]) for d in range(2)]
        chains = [(c, d) for c in range(group) for d in range(2)]
        sl = lambda c: slice(c * CHUNK, (c + 1) * CHUNK)
        kcbs = [k_g[d][sl(c)].astype(BF16) for c, d in chains]
        kk2 = [_dot_nt(kb_, jnp.concatenate([kb_, kb_], axis=0)) for kb_ in kcbs]
        qk = [_dot_nt(q_g[d][sl(c)].astype(BF16), kb_) for (c, d), kb_ in zip(chains, kcbs)]
        gcs = [_mask_dot(incl_b[d], gb_all[d][sl(c)]) for c, d in chains]
        between(0)
        grs = [jnp.concatenate([gc, gc], axis=0).T[:CHUNK, :] for gc in gcs]
        decays = [jnp.where(incl2[d], jnp.exp(jnp.where(incl2[d], gc - gr, 0.0)), 0.0)
                  for (c, d), gc, gr in zip(chains, gcs, grs)]
        zs = [jnp.where(strict2[d] & ~right, -(kk * bt_all[d][sl(c)] * dec), 0.0) + eye_right
              for (c, d), kk, dec in zip(chains, kk2, decays)]
        for level in range(6):
            zs = [_dot3(z[:, :CHUNK], z) + jnp.where(right, z, 0.0) for z in zs]
            if level in (1, 3):
                between((level + 1) // 2)
        zero_rows = jnp.zeros((CHUNK, 2 * HD), BF16)
        egs = [jnp.exp(gc) for gc in gcs]
        g_lasts = [gc[CHUNK - 1:CHUNK, :] if d == 0 else gc[0:1, :] for (c, d), gc in zip(chains, gcs)]
        wus = [_dot(z.astype(BF16), jnp.concatenate(
                   [zero_rows,
                    jnp.concatenate([(k_g[d][sl(c)] * bt_all[d][sl(c)] * eg).astype(BF16),
                                     (v_g[d][sl(c)] * bt_all[d][sl(c)]).astype(BF16)], axis=1)],
                   axis=0)).astype(BF16)
               for (c, d), z, eg in zip(chains, zs, egs)]
        between(3)
        mns = [_dot_tn((k_g[d][sl(c)] * jnp.exp(gl - gc)).astype(BF16), wu)
               for (c, d), gc, gl, wu in zip(chains, gcs, g_lasts, wus)]
        qos = [_dot(jnp.where(incl[d], qk_ * dec[:, :CHUNK], 0.0).astype(BF16), wu)
               for (c, d), qk_, dec, wu in zip(chains, qk, decays, wus)]
        for (c, d), eg, gl, mn, qo in zip(chains, egs, g_lasts, mns, qos):
            ci, pos = base[d] + c, slot_of(g) + c
            mrows = pl.ds(aligned(pos * HD, HD), HD)
            m_s[d, mrows, :] = mn[:, :HD].astype(BF16)
            n_s[d, mrows, :] = mn[:, HD:]
            qp_s[d, pl.ds(aligned(pos * CHUNK, CHUNK), CHUNK), :] = (
                q_g[d][sl(c)] * eg - qo[:, :HD]).astype(BF16)
            o_s[d, pl.ds(aligned(ci * CHUNK, CHUNK), CHUNK), :] = qo[:, HD:]
            gl_s[d, pl.ds(aligned(pos * 8, 8), 8), :] = jnp.broadcast_to(jnp.exp(gl), (8, HD))

    def recurrence_steps(g, state):
        def between(u):
            if u < group:
                state[0] = step(0, g * group + u, slot_of(g) + u, state[0])
                state[1] = step(1, nchunk - 1 - g * group - u, slot_of(g) + group - 1 - u, state[1])
        return between

    return prep, recurrence_steps


def _hg_parts(q_ref, f_refs, i_ref, llb_ref, l1m_ref, oml_ref, o_s, group):
    t = q_ref.shape[0]
    nchunk = t // CHUNK
    nsub = CHUNK // SUB

    def gates(d, rows):
        f = f_refs[d][rows, :]
        e_f = jnp.exp(-jnp.abs(f))
        lsig = jnp.minimum(f, 0.0) - _log1p_unit(e_f)
        a = llb_ref[d:d + 1, :]
        b = l1m_ref[d:d + 1, :] + lsig
        log_f = jnp.maximum(a, b) + _log1p_unit(jnp.exp(-jnp.abs(a - b)))
        k = oml_ref[d:d + 1, :] * (jnp.where(f >= 0.0, e_f, 1.0) * (1.0 / (1.0 + e_f)))
        return _silu(q_ref[rows, :]), k, log_f, i_ref[rows, :]

    incl, _ = _chunk_masks()
    incl_b = tuple(_as_bf16_mask(m) for m in incl)
    sub_i = lax.broadcasted_iota(jnp.int32, (SUB, HD), 0)
    half_i = lax.broadcasted_iota(jnp.int32, (HALF, HD), 0)

    def factored(q_rows, g_rows, k_rows, gk_rows, ref, q_keep=None):
        q_t = q_rows * jnp.exp(jnp.minimum(g_rows - ref, 0.0))
        k_t = k_rows * jnp.exp(jnp.minimum(ref - gk_rows, 0.0))
        if q_keep is not None:
            q_t, k_t = jnp.where(q_keep, q_t, 0.0), jnp.where(q_keep, 0.0, k_t)
        return q_t.astype(BF16), k_t.astype(BF16)

    def trip(g, state):
        base = (g * group, nchunk - (g + 1) * group)
        work = {}

        def stage_gates():
            rows_g = [pl.ds(_aligned(b0 * CHUNK, CHUNK), group * CHUNK) for b0 in base]
            q_g, k_g, lf_g, v_g = zip(*[gates(d, rows_g[d]) for d in range(2)])
            chains = [(d, u if d == 0 else group - 1 - u) for u in range(group) for d in range(2)]
            o_inter, pair_acc, jobs = [], [], []
            for cidx, (d, pos) in enumerate(chains):
                chunk_body(cidx, d, slice(pos * CHUNK, (pos + 1) * CHUNK), q_g[d], k_g[d], lf_g[d], v_g[d],
                           state, o_inter, pair_acc, jobs)
            work.update(chains=chains, o_inter=o_inter, pair_acc=pair_acc, jobs=jobs)

        def stage_scores():
            work["scores"] = [_dot_nt(q_t, k_t).astype(BF16) for _, _, (q_t, k_t), _ in work["jobs"]]

        def stage_values():
            jobs = work["jobs"]
            outs = [_dot(a, v_rows) for a, (_, _, _, v_rows) in zip(work["scores"], jobs)]
            for cidx, (d, pos) in enumerate(work["chains"]):
                blocks = []
                for ib in range(nsub):
                    acc = jnp.concatenate(work["pair_acc"][cidx][2 * ib:2 * ib + 2], axis=0)
                    for o_job, (jc, jb, _, _) in zip(outs, jobs):
                        if (jc, jb) == (cidx, ib):
                            acc = acc + o_job
                    blocks.append(acc)
                rows = pl.ds(_aligned((base[d] + pos) * CHUNK, CHUNK), CHUNK)
                o_s[d, rows, :] = work["o_inter"][cidx] + jnp.concatenate(blocks, axis=0)

        return [stage_gates, stage_scores, stage_values]

    def chunk_body(cidx, d, sl, q_g, k_g, lf_g, v_g, state, o_inter, pair_acc, jobs):
        s_t = state[d]
        qc, kc, vc = q_g[sl], k_g[sl], v_g[sl]
        vb = vc.astype(BF16)
        gc = _mask_dot(incl_b[d], lf_g[sl])
        g_last = gc[CHUNK - 1:CHUNK, :] if d == 0 else gc[0:1, :]
        o_inter.append(_dot_nt((qc * jnp.exp(gc)).astype(BF16), s_t.astype(BF16)))
        k_dec = kc * jnp.exp(g_last - gc)
        state[d] = s_t * jnp.exp(g_last) + _dot_tn(vb, k_dec.astype(BF16))
        for ib in range(nsub):
            r0 = ib * SUB
            blk = slice(r0, r0 + SUB)
            mid = r0 + HALF
            ref = gc[mid - 1:mid, :] if d == 0 else gc[mid:mid + 1, :]
            q_keep = (sub_i >= HALF) if d == 0 else (sub_i < HALF)
            jobs.append((cidx, ib, factored(qc[blk], gc[blk], kc[blk], gc[blk], ref, q_keep), vb[blk]))
            if d == 0 and ib > 0:
                rng, ref = slice(0, r0), gc[r0 - 1:r0, :]
            elif d == 1 and ib < nsub - 1:
                rng, ref = slice(r0 + SUB, CHUNK), gc[r0 + SUB:r0 + SUB + 1, :]
            else:
                continue
            jobs.append((cidx, ib, factored(qc[blk], gc[blk], kc[rng], gc[rng], ref), vb[rng]))
        halves = []
        for h0 in range(0, CHUNK, HALF):
            q_h, g_h = qc[h0:h0 + HALF, :], gc[h0:h0 + HALF, :]
            acc_h = jnp.zeros((HALF, HD), F32)
            for jj in range(HALF):
                j = h0 + jj
                term = q_h * kc[j:j + 1, :] * jnp.exp(g_h - gc[j:j + 1, :])
                keep = (half_i >= jj) if d == 0 else (half_i <= jj)
                acc_h = acc_h + (jnp.sum(jnp.where(keep, term, 0.0), axis=-1, keepdims=True)
                                 * vc[j:j + 1, :])
            halves.append(acc_h)
        pair_acc.append(halves)

    return trip


def _recur_kernel(alog_ref, dtb_ref,
                  dq_ref, dk_ref, dv_ref, dz_ref, dab_ref, cq_ref, ck_ref, cv_ref, dng_ref,
                  ds0f_ref, ds0b_ref,
                  hq_ref, hf0_ref, hf1_ref, hi_ref, hz_ref, llb_ref, l1m_ref, oml_ref, hng_ref,
                  hs0f_ref, hs0b_ref,
                  ydn_ref, dsf_ref, dsb_ref, yhg_ref, hsf_ref, hsb_ref,
                  qs, ks, vs, m_s, n_s, qp_s, gl_s, do_s, ho_s, *, group):
    ngroup = (dq_ref.shape[0] // CHUNK) // group
    dn_prep, dn_steps = _dn_parts(alog_ref, dtb_ref, dq_ref, dk_ref, dv_ref, dab_ref, cq_ref, ck_ref,
                                  cv_ref, qs, ks, vs, m_s, n_s, qp_s, gl_s, do_s, group)
    hg_trip = _hg_parts(hq_ref, (hf0_ref, hf1_ref), hi_ref, llb_ref, l1m_ref, oml_ref, ho_s, group)

    def run_trip(g, dn_state, hg_state):
        recur = dn_steps(g - 1, dn_state) if dn_state is not None else (lambda u: None)
        stages = hg_trip(g, hg_state)

        def between(u):
            recur(u)
            if u < len(stages):
                stages[u]()
        dn_prep(g, between)

    hg_state = [hs0f_ref[0, 0], hs0b_ref[0, 0]]
    run_trip(0, None, hg_state)

    def body(g, carry):
        dn_state, hg_st = list(carry[:2]), list(carry[2:])
        run_trip(g, dn_state, hg_st)
        return tuple(dn_state) + tuple(hg_st)

    carry = lax.fori_loop(1, ngroup, body, (ds0f_ref[0, 0], ds0b_ref[0, 0]) + tuple(hg_state))
    dn_state = list(carry[:2])
    last = dn_steps(ngroup - 1, dn_state)
    for u in range(group):
        last(u)
    dsf_ref[0, 0], dsb_ref[0, 0] = dn_state
    hsf_ref[0, 0], hsb_ref[0, 0] = carry[2], carry[3]
    ydn_ref[...] = _head_rms_gate(do_s[0] + do_s[1], dng_ref, dz_ref).astype(BF16)
    yhg_ref[...] = _head_rms_gate(ho_s[0] + ho_s[1], hng_ref, hz_ref).astype(BF16)


def _recurrent_mixers(p, conv_all, layer, a_log, dt_bias, dn_norm, hg_lbs, hg_norm, dn_s0, hg_s0, t):
    n = p.shape[0]
    bsz = n // t
    nchunk = t // CHUNK
    group = _pick(nchunk, (4, 2, 1))
    col = lambda name, bufs=1: _seq_spec(t, lambda b, h: (b, COL[name] + h), bufs)
    state = pl.BlockSpec((1, 1, HD, HD), lambda b, h: (b, h, 0, 0))
    smem = pl.BlockSpec(memory_space=pltpu.SMEM)
    conv = lambda off: pl.BlockSpec((None, 3, HD), lambda b, h: (layer, 0, off + h))
    lbs = pl.BlockSpec((2, HD), lambda b, h: (0, h))
    gain = pl.BlockSpec((1, HD), lambda b, h: (0, 0))
    y_spec = pl.BlockSpec((t, HD), lambda b, h: (b, h))
    y_shape = jax.ShapeDtypeStruct((n, BRANCH_W), BF16)
    st_shape = jax.ShapeDtypeStruct((bsz, HEADS, HD, HD), F32)
    seq_f32 = pltpu.VMEM((t, HD), F32)
    dir_f32 = pltpu.VMEM((2, t, HD), F32)
    slots = 2 * group
    return pl.pallas_call(
        functools.partial(_recur_kernel, group=group),
        grid=(bsz, HEADS),
        in_specs=[smem, smem,
                  col("dn_q", 2), col("dn_k", 2), col("dn_v", 2), col("dn_z"),
                  _seq_spec(t, lambda b, h: (b, COL["dn_ab"]), 1),
                  conv(0), conv(HEADS), conv(2 * HEADS), gain, state, state,
                  col("hg_q"), col("hg_f0"), col("hg_f1"), col("hg_i"), col("hg_z"),
                  lbs, lbs, lbs, gain, state, state],
        out_specs=[y_spec, state, state, y_spec, state, state],
        out_shape=[y_shape, st_shape, st_shape, y_shape, st_shape, st_shape],
        scratch_shapes=[seq_f32, seq_f32, seq_f32,
                        pltpu.VMEM((2, slots * HD, HD), BF16), pltpu.VMEM((2, slots * HD, HD), F32),
                        pltpu.VMEM((2, slots * CHUNK, HD), BF16), pltpu.VMEM((2, slots * 8, HD), F32),
                        dir_f32, dir_f32],
        compiler_params=_cparams(("parallel", "parallel"), 58),
        name="recurrent_mixers",
    )(a_log, dt_bias, p, p, p, p, p, conv_all, conv_all, conv_all, dn_norm, *dn_s0,
      p, p, p, p, p, *hg_lbs, hg_norm, *hg_s0)


def _rope_tables(t):
    pos = jnp.arange(t)
    row = (pos // GRID_W).astype(F32)
    col = (pos % GRID_W).astype(F32)
    n = DA_DH // 4
    inv = ROPE_THETA ** (-jnp.arange(n, dtype=F32) / n)
    ar, ac = row[:, None] * inv, col[:, None] * inv
    zero = jnp.zeros_like(ar)
    cos = jnp.concatenate([jnp.cos(ar), jnp.cos(ar), jnp.cos(ac), jnp.cos(ac)], axis=-1)
    s_up = jnp.concatenate([zero, jnp.sin(ar), zero, jnp.sin(ac)], axis=-1)
    s_dn = jnp.concatenate([-jnp.sin(ar), zero, -jnp.sin(ac), zero], axis=-1)
    tile = lambda a: jnp.concatenate([a, a], axis=-1)
    return tile(cos), tile(s_up), tile(s_dn)


def _rope(x, cos, s_up, s_dn):
    half = DA_DH // 4
    return x * cos + pltpu.roll(x, half, 1) * s_up + pltpu.roll(x, HD - half, 1) * s_dn


def _da_kernel(lam_ref, q_ref, kl_ref, kc_ref, vl_ref, vc_ref, z_ref, cq_ref, uq_ref, dq_ref,
               ck_ref, uk_ref, dk_ref, ng_ref, y_ref, k_s, vt_s, s_s, *, t_lat, out_scale):
    nkb, kblk, _ = k_s.shape
    nlat = t_lat // kblk

    @pl.when(pl.program_id(2) == 0)
    def _():
        for j in range(nkb):
            if j < nlat:
                rows = slice(j * kblk, (j + 1) * kblk)
                k = _rope(kl_ref[rows, :], ck_ref[rows, :], uk_ref[rows, :], dk_ref[rows, :])
                v = vl_ref[rows, :]
            else:
                rows = slice((j - nlat) * kblk, (j - nlat + 1) * kblk)
                k, v = kc_ref[rows, :], vc_ref[rows, :]
            k_s[j] = k.astype(BF16)
            vt_s[j] = v.T.astype(BF16)

    q = q_ref[...]
    if t_lat:
        q = _rope(q, cq_ref[...], uq_ref[...], dq_ref[...])
    q = q * (DA_DH ** -0.5 * LOG2E)
    lane = lax.broadcasted_iota(jnp.int32, q.shape, 1)
    qms = [jnp.where(lane < DA_DH, q, 0.0).astype(BF16), jnp.where(lane >= DA_DH, q, 0.0).astype(BF16)]
    tq = q.shape[0]
    m = [None, None]
    l = [jnp.zeros((1, tq), F32) for _ in range(2)]
    acc = [jnp.zeros((HD, tq), F32) for _ in range(2)]

    def scores(mp, j):
        s = _dot_nt(k_s[j], qms[mp])
        s_s[mp, j] = s
        bm = jnp.max(s, axis=0, keepdims=True)
        m[mp] = bm if m[mp] is None else jnp.maximum(m[mp], bm)

    def values(mp, j):
        e = jnp.exp2(s_s[mp, j] - m[mp])
        l[mp] = l[mp] + jnp.sum(e, axis=0, keepdims=True)
        acc[mp] = acc[mp] + _dot(vt_s[j], e.astype(BF16))

    for j in range(nkb):
        scores(0, j)
    lead = min(2, nkb)
    for j in range(lead):
        scores(1, j)
    for j in range(nkb):
        if j + lead < nkb:
            scores(1, j + lead)
        values(0, j)
    for j in range(nkb):
        values(1, j)
    o = (acc[0] * (1.0 / l[0]) - acc[1] * (lam_ref[0] * (1.0 / l[1]))).T
    y = o * lax.rsqrt(jnp.mean(o * o, axis=-1, keepdims=True) + EPS) * ng_ref[...]
    y_ref[...] = (y * out_scale * _silu(z_ref[...])).astype(BF16)


def _diff_attention(lam, p_q, p_lat, p_ctx, rope, norm_g, t, t_ctx, out_scale, latent):
    n = p_q.shape[0]
    bsz = n // t
    tq = _pick(t, (512, 256, 128))
    nq = t // tq
    t_lat = t if latent else 0
    t_kl = t if latent else t_ctx
    cos, s_up, s_dn = rope
    kblk = 256 if (t_lat % 256 == 0 and t_ctx % 256 == 0) else 128
    nkb = (t_lat + t_ctx) // kblk
    col = lambda name: (lambda b, h, i: (b, COL[name] + h))
    qcol = lambda name: (lambda b, h, i: (b * nq + i, COL[name] + h))
    tab_q = pl.BlockSpec((tq, HD), lambda b, h, i: (i, 0))
    tab_k = _seq_spec(t_kl, lambda b, h, i: (0, 0), 1)
    return pl.pallas_call(
        functools.partial(_da_kernel, t_lat=t_lat, out_scale=out_scale),
        grid=(bsz, HEADS, nq),
        in_specs=[pl.BlockSpec(memory_space=pltpu.SMEM),
                  pl.BlockSpec((tq, HD), qcol("da_q")),
                  _seq_spec(t_kl, col("da_k")), _seq_spec(t_ctx, col("da_k")),
                  _seq_spec(t_kl, col("da_v")), _seq_spec(t_ctx, col("da_v")),
                  pl.BlockSpec((tq, HD), qcol("da_z")),
                  tab_q, tab_q, tab_q, tab_k, tab_k, tab_k,
                  pl.BlockSpec((1, HD), lambda b, h, i: (0, 0))],
        out_specs=pl.BlockSpec((tq, HD), lambda b, h, i: (b * nq + i, h)),
        out_shape=jax.ShapeDtypeStruct((n, BRANCH_W), BF16),
        scratch_shapes=[pltpu.VMEM((nkb, kblk, HD), BF16), pltpu.VMEM((nkb, HD, kblk), BF16),
                        pltpu.VMEM((2, nkb, kblk, tq), F32)],
        compiler_params=_cparams(("parallel", "parallel", "arbitrary"), 56),
        name="diff_attention",
    )(lam, p_q, p_lat, p_ctx, p_lat, p_ctx, p_q, cos, s_up, s_dn, cos, s_up, s_dn, norm_g)


def _merge_kernel(y0_ref, y1_ref, y2_ref, y3_ref, gl_ref, wb_ref, wo_ref, x_ref, mod_ref, fg_ref,
                  o_ref, *, final):
    d = x_ref.shape[1]
    acc = None
    for k, y_ref in enumerate((y0_ref, y1_ref, y2_ref, y3_ref)):
        c = jax.nn.sigmoid(gl_ref[:, k * d:(k + 1) * d]) * _dot(y_ref[...], wb_ref[k])
        acc = c if acc is None else acc + c
    x = x_ref[...] + mod_ref[0, 2:3, :] * _dot(acc.astype(BF16), wo_ref[...])
    if final:
        x = x * lax.rsqrt(jnp.mean(x * x, axis=-1, keepdims=True) + EPS) * fg_ref[...]
    o_ref[...] = x


def _merge(ys, p, wb_all, wo_all, layer, x2, mod, final_g, rows_per_seg, final):
    n, d = x2.shape
    tm = _pick(rows_per_seg, (256, 128))
    ysp = pl.BlockSpec((tm, BRANCH_W), lambda i: (i, 0))
    return pl.pallas_call(
        functools.partial(_merge_kernel, final=final),
        grid=(n // tm,),
        in_specs=[ysp, ysp, ysp, ysp,
                  pl.BlockSpec((tm, N_BRANCH * d), lambda i: (i, 0)),
                  pl.BlockSpec((None, N_BRANCH, BRANCH_W, d), lambda i: (layer, 0, 0, 0),
                               pipeline_mode=pl.Buffered(1)),
                  pl.BlockSpec((None, d, d), lambda i: (layer, 0, 0), pipeline_mode=pl.Buffered(1)),
                  pl.BlockSpec((tm, d), lambda i: (i, 0)),
                  pl.BlockSpec((1, 3, d), lambda i: (i * tm // rows_per_seg, 0, 0)),
                  pl.BlockSpec((1, d), lambda i: (0, 0))],
        out_specs=pl.BlockSpec((tm, d), lambda i: (i, 0)),
        out_shape=jax.ShapeDtypeStruct((n, d), F32),
        compiler_params=_cparams(("parallel",), 56),
        name="merge",
    )(*ys, p, wb_all, wo_all, x2, mod, final_g)


def kernel(x, c, ctx, c_ctx, norm_g, w_ada, b_ada, w_in, fn_w, fn_b, dn_conv, dn_a_log, dn_dt_bias,
           dn_norm, hg_lb_logits, hg_norm, da_lambda, da_norm, w_branch, w_out, final_g):
    bsz, t, d = x.shape
    t_ctx = ctx.shape[1]
    depth = w_in.shape[0]
    assert d == N_BRANCH * BRANCH_W and t % CHUNK == 0 and t_ctx % CHUNK == 0 and bsz + 1 <= 8

    w_in_r = _transposed_w_in(w_in)
    wb16, wo16, fnw16 = w_branch.astype(BF16), w_out.astype(BF16), fn_w.astype(BF16)

    c8 = jnp.concatenate([c, c_ctx[None, :], jnp.zeros((8 - bsz - 1, d), F32)], axis=0)
    mod = _ada(c8, w_ada, b_ada).reshape(depth, 8, 3, d)

    lb_all = jnp.cumsum(jax.nn.softmax(hg_lb_logits.astype(F32), axis=1), axis=1)
    lb_all = lb_all - lb_all[:, :1]
    log_lb, log_1m_lb, one_m_lb = jnp.log(lb_all), jnp.log1p(-lb_all), 1.0 - lb_all

    c_ch, s_ch = _dft_cos_sin(FN_GW)
    cs_ch = jnp.concatenate([c_ch, s_ch], axis=-1).astype(BF16)
    dft = {}
    for tt in (t, t_ctx):
        ct, st = _dft_cos_sin(tt)
        dft[tt] = (ct.astype(BF16), (-st).astype(BF16))
    rope_l = _rope_tables(t)
    rope_c = tuple(a[:t_ctx] for a in rope_l)

    xl = x.reshape(bsz * t, d)
    xc = ctx.reshape(bsz * t_ctx, d)
    zstate = jnp.zeros((bsz, HEADS, HD, HD), F32)
    for l in range(depth):
        last = l == depth - 1
        g_l = norm_g[l][None, :]
        mod_l, mod_c = mod[l, :bsz], mod[l, bsz:bsz + 1]
        pl_ = _proj(xl, mod_l, g_l, w_in_r, l, t)
        pc_ = _proj(xc, mod_c, g_l, w_in_r, l, bsz * t_ctx)

        fn_b_l = fn_b[l][None, :]
        y_fn_l = _fourier(pl_, cs_ch, *dft[t], fnw16, fn_b_l, l, t)

        dn_n, hg_n, da_n = dn_norm[l][None, :], hg_norm[l][None, :], da_norm[l][None, :]
        lbs = (log_lb[:, l], log_1m_lb[:, l], one_m_lb[:, l])
        rec_args = (dn_conv, l, dn_a_log[l], dn_dt_bias[l], dn_n, lbs, hg_n)
        y_dn_c, s_f, s_b, y_hg_c, h_f, h_b = _recurrent_mixers(
            pc_, *rec_args, (zstate, zstate), (zstate, zstate), t_ctx)
        y_dn_l, _, _, y_hg_l, _, _ = _recurrent_mixers(pl_, *rec_args, (s_f, s_b), (h_f, h_b), t)

        lam_init = 0.8 - 0.6 * math.exp(-0.3 * l)
        lp = da_lambda[l].astype(F32)
        lam = (jnp.exp(jnp.sum(lp[0] * lp[1])) - jnp.exp(jnp.sum(lp[2] * lp[3])) + lam_init).reshape(1)
        y_da_l = _diff_attention(lam, pl_, pl_, pc_, rope_l, da_n, t, t_ctx, 1.0 - lam_init, True)

        fg = final_g[None, :]
        new_xl = _merge((y_fn_l, y_dn_l, y_hg_l, y_da_l), pl_, wb16, wo16, l, xl, mod_l, fg, t, last)
        if not last:
            y_fn_c = _fourier(pc_, cs_ch, *dft[t_ctx], fnw16, fn_b_l, l, t_ctx)
            y_da_c = _diff_attention(lam, pc_, pc_, pc_, rope_c, da_n, t_ctx, t_ctx,
                                     1.0 - lam_init, False)
            xc = _merge((y_fn_c, y_dn_c, y_hg_c, y_da_c), pc_, wb16, wo16, l, xc, mod_c, fg,
                        bsz * t_ctx, False)
        xl = new_xl
    return xl.reshape(bsz, t, d)
```

```python
import functools
import math

import jax
import jax.numpy as jnp
from jax import lax
from jax.experimental import pallas as pl
from jax.experimental.pallas import tpu as pltpu

F32 = jnp.float32
BF16 = jnp.bfloat16

EPS = 1e-6
N_BRANCH = 4
HEADS = 4
HD = 128
BRANCH_W = HEADS * HD
FN_GW = 128
CHUNK = 64
SUB = 16
HALF = SUB // 2
GRID_W = 64
ROPE_THETA = 10000.0
DA_DH = 64
LOG2E = 1.4426950408889634

COL = dict(fn_u=64, fn_z=68, dn_q=72, dn_k=76, dn_v=80, dn_z=84, hg_q=88, hg_f0=92, hg_f1=96,
           hg_i=100, hg_z=104, da_q=108, da_k=112, da_v=116, da_z=120, dn_ab=124)
PROJ_W = 126 * 128
GATE_W_OFF = 7696
AB_OFF = 3072

MIB = 1024 * 1024


def _cparams(sem, vmem_mib):
    return pltpu.CompilerParams(dimension_semantics=sem, vmem_limit_bytes=vmem_mib * MIB)


def _silu(x):
    return x * jax.nn.sigmoid(x)


def _dot(a, b):
    return jnp.dot(a, b, preferred_element_type=F32)


def _dot_nt(a, b):
    return lax.dot_general(a, b, (((1,), (1,)), ((), ())), preferred_element_type=F32)


def _dot_tn(a, b):
    return lax.dot_general(a, b, (((0,), (0,)), ((), ())), preferred_element_type=F32)


def _split2(x):
    hi = x.astype(BF16)
    return hi, (x - hi.astype(F32)).astype(BF16)


def _split3(x):
    hi = x.astype(BF16)
    r = x - hi.astype(F32)
    mid = r.astype(BF16)
    return hi, mid, (r - mid.astype(F32)).astype(BF16)


def _mask_dot(mask, x):
    n = x.shape[1]
    r = _dot(mask, jnp.concatenate(_split3(x), axis=1))
    return r[:, :n] + (r[:, n:2 * n] + r[:, 2 * n:])


def _dot_mask(x, mask):
    hi, mid, lo = _split3(x)
    return _dot(hi, mask) + (_dot(mid, mask) + _dot(lo, mask))


def _pick(n, cands):
    for c in cands:
        if n % c == 0:
            return c
    return n


def _ada_kernel(c_ref, w_ref, b_ref, o_ref):
    sc = _silu(c_ref[...])
    o_ref[0] = _dot(sc.astype(BF16), w_ref[0].astype(BF16)) + b_ref[0]


def _ada(c8, w_ada, b_ada):
    depth, d, d3 = w_ada.shape
    tn = _pick(d3, (1536, 768, 512, 256, 128))
    return pl.pallas_call(
        _ada_kernel,
        grid=(depth, d3 // tn),
        in_specs=[pl.BlockSpec((8, d), lambda l, n: (0, 0)),
                  pl.BlockSpec((1, d, tn), lambda l, n: (l, 0, n)),
                  pl.BlockSpec((1, 1, tn), lambda l, n: (l, 0, n))],
        out_specs=pl.BlockSpec((1, 8, tn), lambda l, n: (l, 0, n)),
        out_shape=jax.ShapeDtypeStruct((depth, 8, d3), F32),
        compiler_params=_cparams(("parallel", "parallel"), 48),
        name="adaln_mod",
    )(c8, w_ada, b_ada.reshape(depth, 1, d3))


def _transposed_w_in(w_in):
    depth, d, _ = w_in.shape
    n_ab = 4 * HEADS
    wt = jnp.swapaxes(w_in, 1, 2)
    parts = [wt[:, GATE_W_OFF:], wt[:, :AB_OFF], wt[:, AB_OFF + n_ab:GATE_W_OFF], wt[:, AB_OFF:AB_OFF + n_ab]]
    return jnp.concatenate([part.astype(BF16) for part in parts]
                           + [jnp.zeros((depth, 2 * HD - n_ab, d), BF16)], axis=1)


def _proj_kernel(x_ref, mod_ref, g_ref, w_ref, o_ref, h_ref):
    @pl.when(pl.program_id(1) == 0)
    def _():
        x = x_ref[...]
        y = x * lax.rsqrt(jnp.mean(x * x, axis=-1, keepdims=True) + EPS) * g_ref[...]
        h = y * (1.0 + mod_ref[0, 1:2, :]) + mod_ref[0, 0:1, :]
        h_ref[...] = h.astype(BF16)

    o_ref[...] = _dot_nt(h_ref[...], w_ref[...])


def _proj(x2, mod, g, wt_all, layer, rows_per_seg):
    n, d = x2.shape
    tm = _pick(rows_per_seg, (1024, 512, 256, 128))
    tn = 1792
    return pl.pallas_call(
        _proj_kernel,
        grid=(n // tm, PROJ_W // tn),
        in_specs=[pl.BlockSpec((tm, d), lambda i, j: (i, 0), pipeline_mode=pl.Buffered(1)),
                  pl.BlockSpec((1, 3, d), lambda i, j: (i * tm // rows_per_seg, 0, 0)),
                  pl.BlockSpec((1, d), lambda i, j: (0, 0)),
                  pl.BlockSpec((None, tn, d), lambda i, j: (layer, j, 0))],
        out_specs=pl.BlockSpec((tm, tn), lambda i, j: (i, j)),
        out_shape=jax.ShapeDtypeStruct((n, PROJ_W), F32),
        scratch_shapes=[pltpu.VMEM((tm, d), BF16)],
        compiler_params=_cparams(("parallel", "arbitrary"), 56),
        name="in_proj",
    )(x2, mod, g, wt_all)


def _dft_cos_sin(n):
    j = jnp.arange(n, dtype=jnp.int32)
    sc = n ** -0.5
    if n <= 1024:
        ang = (2.0 * math.pi / n) * ((j[:, None] * j[None, :]) % n).astype(F32)
        return jnp.cos(ang) * sc, jnp.sin(ang) * sc
    m = n // 64
    k1 = jnp.arange(m, dtype=jnp.int32)
    k2 = jnp.arange(64, dtype=jnp.int32)
    a = (2.0 * math.pi / m) * ((j[:, None] * k1[None, :]) % m).astype(F32)
    b = (2.0 * math.pi / n) * ((j[:, None] * k2[None, :]) % n).astype(F32)
    ca, sa, cb, sb = jnp.cos(a), jnp.sin(a), jnp.cos(b), jnp.sin(b)
    c = ca[:, :, None] * cb[:, None, :] - sa[:, :, None] * sb[:, None, :]
    s = sa[:, :, None] * cb[:, None, :] + ca[:, :, None] * sb[:, None, :]
    return (c.reshape(n, n) * sc).T, (s.reshape(n, n) * sc).T


def _fn_kernel(ct_ref, st_ref, u_ref, cs_ref, z_ref, w_ref, b_ref, y_ref, uc_s, us_s):
    @pl.when(pl.program_id(1) == 0)
    def _():
        for g in range(BRANCH_W // FN_GW):
            cols = slice(g * FN_GW, (g + 1) * FN_GW)
            r = _dot(u_ref[:, cols].astype(BF16), cs_ref[...])
            uc_s[:, cols] = r[:, :FN_GW].astype(BF16)
            us_s[:, cols] = r[:, FN_GW:].astype(BF16)

    f = _dot(ct_ref[...], uc_s[...]) + _dot(st_ref[...], us_s[...])
    y = _dot(f.astype(BF16), w_ref[...]) + b_ref[...]
    y_ref[...] = (y * _silu(z_ref[...])).astype(BF16)


def _fourier(p, cs_ch, ct, nst, fn_w_all, fn_b, layer, t):
    n = p.shape[0]
    bsz = n // t
    tm = _pick(t, (512, 256, 128))
    nt = t // tm
    return pl.pallas_call(
        _fn_kernel,
        grid=(bsz, nt),
        in_specs=[pl.BlockSpec((tm, t), lambda b, i: (i, 0)),
                  pl.BlockSpec((tm, t), lambda b, i: (i, 0)),
                  pl.BlockSpec((t, BRANCH_W), lambda b, i: (b, COL["fn_u"] // 4),
                               pipeline_mode=pl.Buffered(1)),
                  pl.BlockSpec((FN_GW, 2 * FN_GW), lambda b, i: (0, 0)),
                  pl.BlockSpec((tm, BRANCH_W), lambda b, i: (b * nt + i, COL["fn_z"] // 4)),
                  pl.BlockSpec((None, BRANCH_W, BRANCH_W), lambda b, i: (layer, 0, 0)),
                  pl.BlockSpec((1, BRANCH_W), lambda b, i: (0, 0))],
        out_specs=pl.BlockSpec((tm, BRANCH_W), lambda b, i: (b * nt + i, 0)),
        out_shape=jax.ShapeDtypeStruct((n, BRANCH_W), BF16),
        scratch_shapes=[pltpu.VMEM((t, BRANCH_W), BF16), pltpu.VMEM((t, BRANCH_W), BF16)],
        compiler_params=_cparams(("parallel", "arbitrary"), 48),
        name="fourier",
    )(ct, nst, p, cs_ch, p, fn_w_all, fn_b)


def _seq_spec(rows, index_map, buffers=2):
    return pl.BlockSpec((rows, HD), index_map, pipeline_mode=pl.Buffered(buffers))


def _chunk_masks():
    i = lax.broadcasted_iota(jnp.int32, (CHUNK, CHUNK), 0)
    j = lax.broadcasted_iota(jnp.int32, (CHUNK, CHUNK), 1)
    incl = (i >= j, i <= j)
    strict = (i > j, i < j)
    return incl, strict


def _as_bf16_mask(m):
    return jnp.where(m, 1.0, 0.0).astype(BF16)


def _softplus(x):
    return jnp.maximum(x, 0.0) + _log1p_unit(jnp.exp(-jnp.abs(x)))


def _log1p_unit(x):
    return jnp.log(1.0 + x)


def _head_rms_gate(o, g_ref, z_ref):
    y = o * lax.rsqrt(jnp.mean(o * o, axis=-1, keepdims=True) + EPS) * g_ref[...]
    return y * _silu(z_ref[...])


def _aligned(x, m):
    return x if isinstance(x, int) else pl.multiple_of(x, m)


def _dn_parts(alog_ref, dtb_ref, q_ref, k_ref, v_ref, ab_ref, cq_ref, ck_ref, cv_ref,
              qs, ks, vs, m_s, n_s, qp_s, gl_s, o_s, group):
    h = pl.program_id(1)
    t = q_ref.shape[0]
    nchunk = t // CHUNK
    row = lax.broadcasted_iota(jnp.int32, (t, HD), 0)

    def conv(x_ref, c_ref):
        x = x_ref[...]
        xm = jnp.where(row == 0, 0.0, pltpu.roll(x, 1, 0))
        xp = jnp.where(row == t - 1, 0.0, pltpu.roll(x, t - 1, 0))
        y = xm * c_ref[0:1, :] + x * c_ref[1:2, :] + xp * c_ref[2:3, :]
        return _silu(y)

    q = conv(q_ref, cq_ref)
    qs[...] = q * lax.rsqrt(jnp.sum(q * q, axis=-1, keepdims=True) + EPS) * HD ** -0.5
    k = conv(k_ref, ck_ref)
    ks[...] = k * lax.rsqrt(jnp.sum(k * k, axis=-1, keepdims=True) + EPS)
    vs[...] = conv(v_ref, cv_ref)

    incl, _ = _chunk_masks()
    incl_b = tuple(_as_bf16_mask(m) for m in incl)
    ri = lax.broadcasted_iota(jnp.int32, (CHUNK, HD), 0)
    li = lax.broadcasted_iota(jnp.int32, (CHUNK, HD), 1)
    cj = li % CHUNK
    incl2 = (ri >= cj, ri <= cj)
    strict2 = (ri > cj, ri < cj)
    right = li >= CHUNK
    eye_right = jnp.where(li == ri + CHUNK, 1.0, 0.0)
    sel_r = lax.broadcasted_iota(jnp.int32, (HD, 2 * HD), 0)
    sel_c = lax.broadcasted_iota(jnp.int32, (HD, 2 * HD), 1)
    sels = [_as_bf16_mask(sel_r == (sel_c // HD) * 2 * HEADS + d * HEADS + h) for d in range(2)]
    neg_a = [-jnp.exp(jnp.full((1, HD), alog_ref[d, h], F32)) for d in range(2)]
    dtb = [dtb_ref[d, h] for d in range(2)]
    aligned = _aligned
    slot_of = lambda g: (g % 2) * group

    def step(d, ci, pos, s):
        rows = pl.ds(aligned(ci * CHUNK, CHUNK), CHUNK)
        prow = pl.ds(aligned(pos * CHUNK, CHUNK), CHUNK)
        mrows = pl.ds(aligned(pos * HD, HD), HD)
        s_b16 = s.astype(BF16)
        o_s[d, rows, :] = o_s[d, rows, :] + _dot(qp_s[d, prow, :], s_b16)
        g_last = gl_s[d, pl.ds(aligned(pos * 8, 8), 8), :][0:1, :]
        return s * g_last + (n_s[d, mrows, :] - _dot(m_s[d, mrows, :], s_b16))

    def prep(g, between):
        base = (g * group, nchunk - (g + 1) * group)
        rows_g = [pl.ds(aligned(b0 * CHUNK, CHUNK), group * CHUNK) for b0 in base]
        q_g = [qs[r, :] for r in rows_g]
        k_g = [ks[r, :] for r in rows_g]
        v_g = [vs[r, :] for r in rows_g]
        ab = [_dot_mask(ab_ref[rows_g[d], :], sels[d]) for d in range(2)]
        gb_all = [neg_a[d] * _softplus(ab[d][:, :HD] + dtb[d]) for d in range(2)]
        bt_all = [jax.nn.sigmoid(ab[d][:, HD:]) for d in range(2)]
        chains = [(c, d) for c in range(group) for d in range(2)]
        sl = lambda c: slice(c * CHUNK, (c + 1) * CHUNK)
        kcbs = [k_g[d][sl(c)].astype(BF16) for c, d in chains]
        kk2 = [_dot_nt(kb_, jnp.concatenate([kb_, kb_], axis=0)) for kb_ in kcbs]
        qk = [_dot_nt(q_g[d][sl(c)].astype(BF16), kb_) for (c, d), kb_ in zip(chains, kcbs)]
        gcs = [_mask_dot(incl_b[d], gb_all[d][sl(c)]) for c, d in chains]
        between(0)
        grs = [jnp.concatenate([gc, gc], axis=0).T[:CHUNK, :] for gc in gcs]
        decays = [jnp.where(incl2[d], jnp.exp(jnp.where(incl2[d], gc - gr, 0.0)), 0.0)
                  for (c, d), gc, gr in zip(chains, gcs, grs)]
        zs = [jnp.where(strict2[d] & ~right, -(kk * bt_all[d][sl(c)] * dec), 0.0) + eye_right
              for (c, d), kk, dec in zip(chains, kk2, decays)]
        def extend(z):
            z_hi, z_lo = _split2(z)
            x_hi, x_lo = z_hi[:, :CHUNK], z_lo[:, :CHUNK]
            both = _dot(x_hi, jnp.concatenate([z_hi, z_lo], axis=1))
            return both[:, :HD] + (both[:, HD:] + _dot(x_lo, z_hi)) + jnp.where(right, z, 0.0)

        for level in range(6):
            zs = [extend(z) for z in zs]
            if level in (1, 3):
                between((level + 1) // 2)
        zero_rows = jnp.zeros((CHUNK, 2 * HD), BF16)
        egs = [jnp.exp(gc) for gc in gcs]
        g_lasts = [gc[CHUNK - 1:CHUNK, :] if d == 0 else gc[0:1, :] for (c, d), gc in zip(chains, gcs)]
        wus = [_dot(z.astype(BF16), jnp.concatenate(
                   [zero_rows,
                    jnp.concatenate([(k_g[d][sl(c)] * bt_all[d][sl(c)] * eg).astype(BF16),
                                     (v_g[d][sl(c)] * bt_all[d][sl(c)]).astype(BF16)], axis=1)],
                   axis=0)).astype(BF16)
               for (c, d), z, eg in zip(chains, zs, egs)]
        between(3)
        mns = [_dot_tn((k_g[d][sl(c)] * jnp.exp(gl - gc)).astype(BF16), wu)
               for (c, d), gc, gl, wu in zip(chains, gcs, g_lasts, wus)]
        qos = [_dot(jnp.where(incl[d], qk_ * dec[:, :CHUNK], 0.0).astype(BF16), wu)
               for (c, d), qk_, dec, wu in zip(chains, qk, decays, wus)]
        for (c, d), eg, gl, mn, qo in zip(chains, egs, g_lasts, mns, qos):
            ci, pos = base[d] + c, slot_of(g) + c
            mrows = pl.ds(aligned(pos * HD, HD), HD)
            m_s[d, mrows, :] = mn[:, :HD].astype(BF16)
            n_s[d, mrows, :] = mn[:, HD:]
            qp_s[d, pl.ds(aligned(pos * CHUNK, CHUNK), CHUNK), :] = (
                q_g[d][sl(c)] * eg - qo[:, :HD]).astype(BF16)
            o_s[d, pl.ds(aligned(ci * CHUNK, CHUNK), CHUNK), :] = qo[:, HD:]
            gl_s[d, pl.ds(aligned(pos * 8, 8), 8), :] = jnp.broadcast_to(jnp.exp(gl), (8, HD))

    def recurrence_steps(g, state):
        def between(u):
            if u < group:
                state[0] = step(0, g * group + u, slot_of(g) + u, state[0])
                state[1] = step(1, nchunk - 1 - g * group - u, slot_of(g) + group - 1 - u, state[1])
        return between

    return prep, recurrence_steps


def _hg_parts(q_ref, f_refs, i_ref, llb_ref, l1m_ref, oml_ref, o_s, group):
    t = q_ref.shape[0]
    nchunk = t // CHUNK
    nsub = CHUNK // SUB

    def gates(d, rows):
        f = f_refs[d][rows, :]
        e_f = jnp.exp(-jnp.abs(f))
        lsig = jnp.minimum(f, 0.0) - _log1p_unit(e_f)
        a = llb_ref[d:d + 1, :]
        b = l1m_ref[d:d + 1, :] + lsig
        log_f = jnp.maximum(a, b) + _log1p_unit(jnp.exp(-jnp.abs(a - b)))
        k = oml_ref[d:d + 1, :] * (jnp.where(f >= 0.0, e_f, 1.0) * (1.0 / (1.0 + e_f)))
        return _silu(q_ref[rows, :]), k, log_f, i_ref[rows, :]

    incl, _ = _chunk_masks()
    incl_b = tuple(_as_bf16_mask(m) for m in incl)
    sub_i = lax.broadcasted_iota(jnp.int32, (SUB, HD), 0)
    half_i = lax.broadcasted_iota(jnp.int32, (HALF, HD), 0)

    def factored(q_rows, g_rows, k_rows, gk_rows, ref, q_keep=None):
        q_t = q_rows * jnp.exp(jnp.minimum(g_rows - ref, 0.0))
        k_t = k_rows * jnp.exp(jnp.minimum(ref - gk_rows, 0.0))
        if q_keep is not None:
            q_t, k_t = jnp.where(q_keep, q_t, 0.0), jnp.where(q_keep, 0.0, k_t)
        return q_t.astype(BF16), k_t.astype(BF16)

    def trip(g, state):
        base = (g * group, nchunk - (g + 1) * group)
        work = {}

        def stage_gates():
            rows_g = [pl.ds(_aligned(b0 * CHUNK, CHUNK), group * CHUNK) for b0 in base]
            q_g, k_g, lf_g, v_g = zip(*[gates(d, rows_g[d]) for d in range(2)])
            chains = [(d, u if d == 0 else group - 1 - u) for u in range(group) for d in range(2)]
            o_inter, pair_acc, jobs = [], [], []
            for cidx, (d, pos) in enumerate(chains):
                chunk_body(cidx, d, slice(pos * CHUNK, (pos + 1) * CHUNK), q_g[d], k_g[d], lf_g[d], v_g[d],
                           state, o_inter, pair_acc, jobs)
            work.update(chains=chains, o_inter=o_inter, pair_acc=pair_acc, jobs=jobs)

        def stage_scores():
            work["scores"] = [_dot_nt(q_t, k_t).astype(BF16) for _, _, (q_t, k_t), _ in work["jobs"]]

        def stage_values():
            jobs = work["jobs"]
            outs = [_dot(a, v_rows) for a, (_, _, _, v_rows) in zip(work["scores"], jobs)]
            for cidx, (d, pos) in enumerate(work["chains"]):
                blocks = []
                for ib in range(nsub):
                    acc = jnp.concatenate(work["pair_acc"][cidx][2 * ib:2 * ib + 2], axis=0)
                    for o_job, (jc, jb, _, _) in zip(outs, jobs):
                        if (jc, jb) == (cidx, ib):
                            acc = acc + o_job
                    blocks.append(acc)
                rows = pl.ds(_aligned((base[d] + pos) * CHUNK, CHUNK), CHUNK)
                o_s[d, rows, :] = work["o_inter"][cidx] + jnp.concatenate(blocks, axis=0)

        return [stage_gates, stage_scores, stage_values]

    def chunk_body(cidx, d, sl, q_g, k_g, lf_g, v_g, state, o_inter, pair_acc, jobs):
        s_t = state[d]
        qc, kc, vc = q_g[sl], k_g[sl], v_g[sl]
        vb = vc.astype(BF16)
        gc = _mask_dot(incl_b[d], lf_g[sl])
        g_last = gc[CHUNK - 1:CHUNK, :] if d == 0 else gc[0:1, :]
        o_inter.append(_dot_nt((qc * jnp.exp(gc)).astype(BF16), s_t.astype(BF16)))
        k_dec = kc * jnp.exp(g_last - gc)
        state[d] = s_t * jnp.exp(g_last) + _dot_tn(vb, k_dec.astype(BF16))
        for ib in range(nsub):
            r0 = ib * SUB
            blk = slice(r0, r0 + SUB)
            mid = r0 + HALF
            ref = gc[mid - 1:mid, :] if d == 0 else gc[mid:mid + 1, :]
            q_keep = (sub_i >= HALF) if d == 0 else (sub_i < HALF)
            jobs.append((cidx, ib, factored(qc[blk], gc[blk], kc[blk], gc[blk], ref, q_keep), vb[blk]))
            if d == 0 and ib > 0:
                rng, ref = slice(0, r0), gc[r0 - 1:r0, :]
            elif d == 1 and ib < nsub - 1:
                rng, ref = slice(r0 + SUB, CHUNK), gc[r0 + SUB:r0 + SUB + 1, :]
            else:
                continue
            jobs.append((cidx, ib, factored(qc[blk], gc[blk], kc[rng], gc[rng], ref), vb[rng]))
        halves = []
        for h0 in range(0, CHUNK, HALF):
            q_h, g_h = qc[h0:h0 + HALF, :], gc[h0:h0 + HALF, :]
            acc_h = jnp.zeros((HALF, HD), F32)
            for jj in range(HALF):
                j = h0 + jj
                term = q_h * kc[j:j + 1, :] * jnp.exp(g_h - gc[j:j + 1, :])
                keep = (half_i >= jj) if d == 0 else (half_i <= jj)
                acc_h = acc_h + (jnp.sum(jnp.where(keep, term, 0.0), axis=-1, keepdims=True)
                                 * vc[j:j + 1, :])
            halves.append(acc_h)
        pair_acc.append(halves)

    return trip


def _recur_kernel(alog_ref, dtb_ref,
                  dq_ref, dk_ref, dv_ref, dz_ref, dab_ref, cq_ref, ck_ref, cv_ref, dng_ref,
                  ds0f_ref, ds0b_ref,
                  hq_ref, hf0_ref, hf1_ref, hi_ref, hz_ref, llb_ref, l1m_ref, oml_ref, hng_ref,
                  hs0f_ref, hs0b_ref,
                  ydn_ref, dsf_ref, dsb_ref, yhg_ref, hsf_ref, hsb_ref,
                  qs, ks, vs, m_s, n_s, qp_s, gl_s, do_s, ho_s, *, group):
    ngroup = (dq_ref.shape[0] // CHUNK) // group
    dn_prep, dn_steps = _dn_parts(alog_ref, dtb_ref, dq_ref, dk_ref, dv_ref, dab_ref, cq_ref, ck_ref,
                                  cv_ref, qs, ks, vs, m_s, n_s, qp_s, gl_s, do_s, group)
    hg_trip = _hg_parts(hq_ref, (hf0_ref, hf1_ref), hi_ref, llb_ref, l1m_ref, oml_ref, ho_s, group)

    def run_trip(g, dn_state, hg_state):
        recur = dn_steps(g - 1, dn_state) if dn_state is not None else (lambda u: None)
        stages = hg_trip(g, hg_state)

        def between(u):
            recur(u)
            if u < len(stages):
                stages[u]()
        dn_prep(g, between)

    hg_state = [hs0f_ref[0, 0], hs0b_ref[0, 0]]
    run_trip(0, None, hg_state)

    def body(g, carry):
        dn_state, hg_st = list(carry[:2]), list(carry[2:])
        run_trip(g, dn_state, hg_st)
        return tuple(dn_state) + tuple(hg_st)

    carry = lax.fori_loop(1, ngroup, body, (ds0f_ref[0, 0], ds0b_ref[0, 0]) + tuple(hg_state))
    dn_state = list(carry[:2])
    last = dn_steps(ngroup - 1, dn_state)
    for u in range(group):
        last(u)
    dsf_ref[0, 0], dsb_ref[0, 0] = dn_state
    hsf_ref[0, 0], hsb_ref[0, 0] = carry[2], carry[3]
    ydn_ref[...] = _head_rms_gate(do_s[0] + do_s[1], dng_ref, dz_ref).astype(BF16)
    yhg_ref[...] = _head_rms_gate(ho_s[0] + ho_s[1], hng_ref, hz_ref).astype(BF16)


def _recurrent_mixers(p, conv_all, layer, a_log, dt_bias, dn_norm, hg_lbs, hg_norm, dn_s0, hg_s0, t):
    n = p.shape[0]
    bsz = n // t
    nchunk = t // CHUNK
    group = _pick(nchunk, (4, 2, 1))
    col = lambda name, bufs=1: _seq_spec(t, lambda b, h: (b, COL[name] + h), bufs)
    state = pl.BlockSpec((1, 1, HD, HD), lambda b, h: (b, h, 0, 0))
    smem = pl.BlockSpec(memory_space=pltpu.SMEM)
    conv = lambda off: pl.BlockSpec((None, 3, HD), lambda b, h: (layer, 0, off + h))
    lbs = pl.BlockSpec((2, HD), lambda b, h: (0, h))
    gain = pl.BlockSpec((1, HD), lambda b, h: (0, 0))
    y_spec = pl.BlockSpec((t, HD), lambda b, h: (b, h))
    y_shape = jax.ShapeDtypeStruct((n, BRANCH_W), BF16)
    st_shape = jax.ShapeDtypeStruct((bsz, HEADS, HD, HD), F32)
    seq_f32 = pltpu.VMEM((t, HD), F32)
    dir_f32 = pltpu.VMEM((2, t, HD), F32)
    slots = 2 * group
    return pl.pallas_call(
        functools.partial(_recur_kernel, group=group),
        grid=(bsz, HEADS),
        in_specs=[smem, smem,
                  col("dn_q", 2), col("dn_k", 2), col("dn_v", 2), col("dn_z"),
                  _seq_spec(t, lambda b, h: (b, COL["dn_ab"]), 1),
                  conv(0), conv(HEADS), conv(2 * HEADS), gain, state, state,
                  col("hg_q"), col("hg_f0"), col("hg_f1"), col("hg_i"), col("hg_z"),
                  lbs, lbs, lbs, gain, state, state],
        out_specs=[y_spec, state, state, y_spec, state, state],
        out_shape=[y_shape, st_shape, st_shape, y_shape, st_shape, st_shape],
        scratch_shapes=[seq_f32, seq_f32, seq_f32,
                        pltpu.VMEM((2, slots * HD, HD), BF16), pltpu.VMEM((2, slots * HD, HD), F32),
                        pltpu.VMEM((2, slots * CHUNK, HD), BF16), pltpu.VMEM((2, slots * 8, HD), F32),
                        dir_f32, dir_f32],
        compiler_params=_cparams(("parallel", "parallel"), 58),
        name="recurrent_mixers",
    )(a_log, dt_bias, p, p, p, p, p, conv_all, conv_all, conv_all, dn_norm, *dn_s0,
      p, p, p, p, p, *hg_lbs, hg_norm, *hg_s0)


def _rope_tables(t):
    pos = jnp.arange(t)
    row = (pos // GRID_W).astype(F32)
    col = (pos % GRID_W).astype(F32)
    n = DA_DH // 4
    inv = ROPE_THETA ** (-jnp.arange(n, dtype=F32) / n)
    ar, ac = row[:, None] * inv, col[:, None] * inv
    zero = jnp.zeros_like(ar)
    cos = jnp.concatenate([jnp.cos(ar), jnp.cos(ar), jnp.cos(ac), jnp.cos(ac)], axis=-1)
    s_up = jnp.concatenate([zero, jnp.sin(ar), zero, jnp.sin(ac)], axis=-1)
    s_dn = jnp.concatenate([-jnp.sin(ar), zero, -jnp.sin(ac), zero], axis=-1)
    tile = lambda a: jnp.concatenate([a, a], axis=-1)
    return tile(cos), tile(s_up), tile(s_dn)


def _rope(x, cos, s_up, s_dn):
    half = DA_DH // 4
    return x * cos + pltpu.roll(x, half, 1) * s_up + pltpu.roll(x, HD - half, 1) * s_dn


def _da_kernel(lam_ref, q_ref, kl_ref, kc_ref, vl_ref, vc_ref, z_ref, cq_ref, uq_ref, dq_ref,
               ck_ref, uk_ref, dk_ref, ng_ref, y_ref, k_s, vt_s, s_s, *, t_lat, out_scale):
    nkb, kblk, _ = k_s.shape
    nlat = t_lat // kblk

    @pl.when(pl.program_id(2) == 0)
    def _():
        for j in range(nkb):
            if j < nlat:
                rows = slice(j * kblk, (j + 1) * kblk)
                k = _rope(kl_ref[rows, :], ck_ref[rows, :], uk_ref[rows, :], dk_ref[rows, :])
                v = vl_ref[rows, :]
            else:
                rows = slice((j - nlat) * kblk, (j - nlat + 1) * kblk)
                k, v = kc_ref[rows, :], vc_ref[rows, :]
            k_s[j] = k.astype(BF16)
            vt_s[j] = v.T.astype(BF16)

    q = q_ref[...]
    if t_lat:
        q = _rope(q, cq_ref[...], uq_ref[...], dq_ref[...])
    q = q * (DA_DH ** -0.5 * LOG2E)
    lane = lax.broadcasted_iota(jnp.int32, q.shape, 1)
    qms = [jnp.where(lane < DA_DH, q, 0.0).astype(BF16), jnp.where(lane >= DA_DH, q, 0.0).astype(BF16)]
    tq = q.shape[0]
    m = [None, None]
    l = [jnp.zeros((1, tq), F32) for _ in range(2)]
    acc = [jnp.zeros((HD, tq), F32) for _ in range(2)]

    def scores(mp, j):
        s = _dot_nt(k_s[j], qms[mp])
        s_s[mp, j] = s
        bm = jnp.max(s, axis=0, keepdims=True)
        m[mp] = bm if m[mp] is None else jnp.maximum(m[mp], bm)

    def values(mp, j):
        e = jnp.exp2(s_s[mp, j] - m[mp])
        l[mp] = l[mp] + jnp.sum(e, axis=0, keepdims=True)
        acc[mp] = acc[mp] + _dot(vt_s[j], e.astype(BF16))

    for j in range(nkb):
        scores(0, j)
    lead = min(2, nkb)
    for j in range(lead):
        scores(1, j)
    for j in range(nkb):
        if j + lead < nkb:
            scores(1, j + lead)
        values(0, j)
    for j in range(nkb):
        values(1, j)
    o = (acc[0] * (1.0 / l[0]) - acc[1] * (lam_ref[0] * (1.0 / l[1]))).T
    y = o * lax.rsqrt(jnp.mean(o * o, axis=-1, keepdims=True) + EPS) * ng_ref[...]
    y_ref[...] = (y * out_scale * _silu(z_ref[...])).astype(BF16)


def _diff_attention(lam, p_q, p_lat, p_ctx, rope, norm_g, t, t_ctx, out_scale, latent):
    n = p_q.shape[0]
    bsz = n // t
    tq = _pick(t, (512, 256, 128))
    nq = t // tq
    t_lat = t if latent else 0
    t_kl = t if latent else t_ctx
    cos, s_up, s_dn = rope
    kblk = 256 if (t_lat % 256 == 0 and t_ctx % 256 == 0) else 128
    nkb = (t_lat + t_ctx) // kblk
    col = lambda name: (lambda b, h, i: (b, COL[name] + h))
    qcol = lambda name: (lambda b, h, i: (b * nq + i, COL[name] + h))
    tab_q = pl.BlockSpec((tq, HD), lambda b, h, i: (i, 0))
    tab_k = _seq_spec(t_kl, lambda b, h, i: (0, 0), 1)
    return pl.pallas_call(
        functools.partial(_da_kernel, t_lat=t_lat, out_scale=out_scale),
        grid=(bsz, HEADS, nq),
        in_specs=[pl.BlockSpec(memory_space=pltpu.SMEM),
                  pl.BlockSpec((tq, HD), qcol("da_q")),
                  _seq_spec(t_kl, col("da_k")), _seq_spec(t_ctx, col("da_k")),
                  _seq_spec(t_kl, col("da_v")), _seq_spec(t_ctx, col("da_v")),
                  pl.BlockSpec((tq, HD), qcol("da_z")),
                  tab_q, tab_q, tab_q, tab_k, tab_k, tab_k,
                  pl.BlockSpec((1, HD), lambda b, h, i: (0, 0))],
        out_specs=pl.BlockSpec((tq, HD), lambda b, h, i: (b * nq + i, h)),
        out_shape=jax.ShapeDtypeStruct((n, BRANCH_W), BF16),
        scratch_shapes=[pltpu.VMEM((nkb, kblk, HD), BF16), pltpu.VMEM((nkb, HD, kblk), BF16),
                        pltpu.VMEM((2, nkb, kblk, tq), F32)],
        compiler_params=_cparams(("parallel", "parallel", "arbitrary"), 56),
        name="diff_attention",
    )(lam, p_q, p_lat, p_ctx, p_lat, p_ctx, p_q, cos, s_up, s_dn, cos, s_up, s_dn, norm_g)


def _merge_kernel(y0_ref, y1_ref, y2_ref, y3_ref, gl_ref, wb_ref, wo_ref, x_ref, mod_ref, fg_ref,
                  o_ref, *, final):
    d = x_ref.shape[1]
    acc = None
    for k, y_ref in enumerate((y0_ref, y1_ref, y2_ref, y3_ref)):
        c = jax.nn.sigmoid(gl_ref[:, k * d:(k + 1) * d]) * _dot(y_ref[...], wb_ref[k])
        acc = c if acc is None else acc + c
    x = x_ref[...] + mod_ref[0, 2:3, :] * _dot(acc.astype(BF16), wo_ref[...])
    if final:
        x = x * lax.rsqrt(jnp.mean(x * x, axis=-1, keepdims=True) + EPS) * fg_ref[...]
    o_ref[...] = x


def _merge(ys, p, wb_all, wo_all, layer, x2, mod, final_g, rows_per_seg, final):
    n, d = x2.shape
    tm = _pick(rows_per_seg, (256, 128))
    ysp = pl.BlockSpec((tm, BRANCH_W), lambda i: (i, 0))
    return pl.pallas_call(
        functools.partial(_merge_kernel, final=final),
        grid=(n // tm,),
        in_specs=[ysp, ysp, ysp, ysp,
                  pl.BlockSpec((tm, N_BRANCH * d), lambda i: (i, 0)),
                  pl.BlockSpec((None, N_BRANCH, BRANCH_W, d), lambda i: (layer, 0, 0, 0),
                               pipeline_mode=pl.Buffered(1)),
                  pl.BlockSpec((None, d, d), lambda i: (layer, 0, 0), pipeline_mode=pl.Buffered(1)),
                  pl.BlockSpec((tm, d), lambda i: (i, 0)),
                  pl.BlockSpec((1, 3, d), lambda i: (i * tm // rows_per_seg, 0, 0)),
                  pl.BlockSpec((1, d), lambda i: (0, 0))],
        out_specs=pl.BlockSpec((tm, d), lambda i: (i, 0)),
        out_shape=jax.ShapeDtypeStruct((n, d), F32),
        compiler_params=_cparams(("parallel",), 56),
        name="merge",
    )(*ys, p, wb_all, wo_all, x2, mod, final_g)


def kernel(x, c, ctx, c_ctx, norm_g, w_ada, b_ada, w_in, fn_w, fn_b, dn_conv, dn_a_log, dn_dt_bias,
           dn_norm, hg_lb_logits, hg_norm, da_lambda, da_norm, w_branch, w_out, final_g):
    bsz, t, d = x.shape
    t_ctx = ctx.shape[1]
    depth = w_in.shape[0]
    assert d == N_BRANCH * BRANCH_W and t % CHUNK == 0 and t_ctx % CHUNK == 0 and bsz + 1 <= 8

    w_in_r = _transposed_w_in(w_in)
    wb16, wo16, fnw16 = w_branch.astype(BF16), w_out.astype(BF16), fn_w.astype(BF16)

    c8 = jnp.concatenate([c, c_ctx[None, :], jnp.zeros((8 - bsz - 1, d), F32)], axis=0)
    mod = _ada(c8, w_ada, b_ada).reshape(depth, 8, 3, d)

    lb_all = jnp.cumsum(jax.nn.softmax(hg_lb_logits.astype(F32), axis=1), axis=1)
    lb_all = lb_all - lb_all[:, :1]
    log_lb, log_1m_lb, one_m_lb = jnp.log(lb_all), jnp.log1p(-lb_all), 1.0 - lb_all

    c_ch, s_ch = _dft_cos_sin(FN_GW)
    cs_ch = jnp.concatenate([c_ch, s_ch], axis=-1).astype(BF16)
    dft = {}
    for tt in (t, t_ctx):
        ct, st = _dft_cos_sin(tt)
        dft[tt] = (ct.astype(BF16), (-st).astype(BF16))
    rope_l = _rope_tables(t)
    rope_c = tuple(a[:t_ctx] for a in rope_l)

    xl = x.reshape(bsz * t, d)
    xc = ctx.reshape(bsz * t_ctx, d)
    zstate = jnp.zeros((bsz, HEADS, HD, HD), F32)
    for l in range(depth):
        last = l == depth - 1
        g_l = norm_g[l][None, :]
        mod_l, mod_c = mod[l, :bsz], mod[l, bsz:bsz + 1]
        pl_ = _proj(xl, mod_l, g_l, w_in_r, l, t)
        pc_ = _proj(xc, mod_c, g_l, w_in_r, l, bsz * t_ctx)

        fn_b_l = fn_b[l][None, :]
        y_fn_l = _fourier(pl_, cs_ch, *dft[t], fnw16, fn_b_l, l, t)

        dn_n, hg_n, da_n = dn_norm[l][None, :], hg_norm[l][None, :], da_norm[l][None, :]
        lbs = (log_lb[:, l], log_1m_lb[:, l], one_m_lb[:, l])
        rec_args = (dn_conv, l, dn_a_log[l], dn_dt_bias[l], dn_n, lbs, hg_n)
        y_dn_c, s_f, s_b, y_hg_c, h_f, h_b = _recurrent_mixers(
            pc_, *rec_args, (zstate, zstate), (zstate, zstate), t_ctx)
        y_dn_l, _, _, y_hg_l, _, _ = _recurrent_mixers(pl_, *rec_args, (s_f, s_b), (h_f, h_b), t)

        lam_init = 0.8 - 0.6 * math.exp(-0.3 * l)
        lp = da_lambda[l].astype(F32)
        lam = (jnp.exp(jnp.sum(lp[0] * lp[1])) - jnp.exp(jnp.sum(lp[2] * lp[3])) + lam_init).reshape(1)
        y_da_l = _diff_attention(lam, pl_, pl_, pc_, rope_l, da_n, t, t_ctx, 1.0 - lam_init, True)

        fg = final_g[None, :]
        new_xl = _merge((y_fn_l, y_dn_l, y_hg_l, y_da_l), pl_, wb16, wo16, l, xl, mod_l, fg, t, last)
        if not last:
            y_fn_c = _fourier(pc_, cs_ch, *dft[t_ctx], fnw16, fn_b_l, l, t_ctx)
            y_da_c = _diff_attention(lam, pc_, pc_, pc_, rope_c, da_n, t_ctx, t_ctx,
                                     1.0 - lam_init, False)
            xc = _merge((y_fn_c, y_dn_c, y_hg_c, y_da_c), pc_, wb16, wo16, l, xc, mod_c, fg,
                        bsz * t_ctx, False)
        xl = new_xl
    return xl.reshape(bsz, t, d)
```

```python
import functools
import math

import jax
import jax.numpy as jnp
from jax import lax
from jax.experimental import pallas as pl
from jax.experimental.pallas import tpu as pltpu

F32 = jnp.float32
BF16 = jnp.bfloat16

EPS = 1e-6
N_BRANCH = 4
HEADS = 4
HD = 128
BRANCH_W = HEADS * HD
FN_GW = 128
CHUNK = 64
SUB = 16
HALF = SUB // 2
GRID_W = 64
ROPE_THETA = 10000.0
DA_DH = 64
LOG2E = 1.4426950408889634

COL = dict(fn_u=64, fn_z=68, dn_q=72, dn_k=76, dn_v=80, dn_z=84, hg_q=88, hg_f0=92, hg_f1=96,
           hg_i=100, hg_z=104, da_q=108, da_k=112, da_v=116, da_z=120, dn_ab=124)
PROJ_W = 126 * 128
PROJ_TN = 7 * 256
GATE_W_OFF = 7696
AB_OFF = 3072

MIB = 1024 * 1024
VMEM_MIB = dict(adaln_mod=48, in_proj=56, fourier=48, recurrent_mixers=58, diff_attention=56, merge=56)


def _cparams(name, sem):
    return pltpu.CompilerParams(dimension_semantics=sem, vmem_limit_bytes=VMEM_MIB[name] * MIB)


def _silu(x):
    return x * jax.nn.sigmoid(x)


def _dot(a, b):
    return jnp.dot(a, b, preferred_element_type=F32)


def _dot_nt(a, b):
    return lax.dot_general(a, b, (((1,), (1,)), ((), ())), preferred_element_type=F32)


def _dot_tn(a, b):
    return lax.dot_general(a, b, (((0,), (0,)), ((), ())), preferred_element_type=F32)


def _split2(x):
    hi = x.astype(BF16)
    return hi, (x - hi.astype(F32)).astype(BF16)


def _split3(x):
    hi = x.astype(BF16)
    r = x - hi.astype(F32)
    mid = r.astype(BF16)
    return hi, mid, (r - mid.astype(F32)).astype(BF16)


def _mask_dot(mask, x):
    n = x.shape[1]
    r = _dot(mask, jnp.concatenate(_split3(x), axis=1))
    return r[:, :n] + (r[:, n:2 * n] + r[:, 2 * n:])


def _dot_mask(x, mask):
    hi, mid, lo = _split3(x)
    return _dot(hi, mask) + (_dot(mid, mask) + _dot(lo, mask))


def _pick(n, cands):
    for c in cands:
        if n % c == 0:
            return c
    return n


def _ada_kernel(c_ref, w_ref, b_ref, o_ref):
    sc = _silu(c_ref[...])
    o_ref[0] = _dot(sc.astype(BF16), w_ref[0].astype(BF16)) + b_ref[0]


def _ada(c8, w_ada, b_ada):
    depth, d, d3 = w_ada.shape
    tn = _pick(d3, (1536, 768, 512, 256, 128))
    return pl.pallas_call(
        _ada_kernel,
        grid=(depth, d3 // tn),
        in_specs=[pl.BlockSpec((8, d), lambda l, n: (0, 0)),
                  pl.BlockSpec((1, d, tn), lambda l, n: (l, 0, n)),
                  pl.BlockSpec((1, 1, tn), lambda l, n: (l, 0, n))],
        out_specs=pl.BlockSpec((1, 8, tn), lambda l, n: (l, 0, n)),
        out_shape=jax.ShapeDtypeStruct((depth, 8, d3), F32),
        compiler_params=_cparams("adaln_mod", ("parallel", "parallel")),
        name="adaln_mod",
    )(c8, w_ada, b_ada.reshape(depth, 1, d3))


def _transposed_w_in(w_in):
    depth, d, _ = w_in.shape
    n_ab = 4 * HEADS
    wt = jnp.swapaxes(w_in, 1, 2)
    parts = [wt[:, GATE_W_OFF:], wt[:, :AB_OFF], wt[:, AB_OFF + n_ab:GATE_W_OFF], wt[:, AB_OFF:AB_OFF + n_ab]]
    return jnp.concatenate([part.astype(BF16) for part in parts]
                           + [jnp.zeros((depth, 2 * HD - n_ab, d), BF16)], axis=1)


def _proj_kernel(x_ref, mod_ref, g_ref, w_ref, o_ref, h_ref):
    @pl.when(pl.program_id(1) == 0)
    def _():
        x = x_ref[...]
        y = x * lax.rsqrt(jnp.mean(x * x, axis=-1, keepdims=True) + EPS) * g_ref[...]
        h = (y * (1.0 + mod_ref[0, 1:2, :]) + mod_ref[0, 0:1, :]).astype(BF16)
        h_ref[...] = h
        o_ref[...] = _dot_nt(h, w_ref[...])

    @pl.when(pl.program_id(1) != 0)
    def _():
        o_ref[...] = _dot_nt(h_ref[...], w_ref[...])


def _proj(x2, mod, g, wt_all, layer, rows_per_seg):
    n, d = x2.shape
    tm = _pick(rows_per_seg, (1024, 512, 256, 128))
    tn = PROJ_TN
    return pl.pallas_call(
        _proj_kernel,
        grid=(n // tm, PROJ_W // tn),
        in_specs=[pl.BlockSpec((tm, d), lambda i, j: (i, 0), pipeline_mode=pl.Buffered(1)),
                  pl.BlockSpec((1, 3, d), lambda i, j: (i * tm // rows_per_seg, 0, 0)),
                  pl.BlockSpec((1, d), lambda i, j: (0, 0)),
                  pl.BlockSpec((None, tn, d), lambda i, j: (layer, j, 0))],
        out_specs=pl.BlockSpec((tm, tn), lambda i, j: (i, j)),
        out_shape=jax.ShapeDtypeStruct((n, PROJ_W), F32),
        scratch_shapes=[pltpu.VMEM((tm, d), BF16)],
        compiler_params=_cparams("in_proj", ("parallel", "arbitrary")),
        name="in_proj",
    )(x2, mod, g, wt_all)


def _dft_cos_sin(n):
    j = jnp.arange(n, dtype=jnp.int32)
    sc = n ** -0.5
    if n <= 1024:
        ang = (2.0 * math.pi / n) * ((j[:, None] * j[None, :]) % n).astype(F32)
        return jnp.cos(ang) * sc, jnp.sin(ang) * sc
    m = n // 64
    k1 = jnp.arange(m, dtype=jnp.int32)
    k2 = jnp.arange(64, dtype=jnp.int32)
    a = (2.0 * math.pi / m) * ((j[:, None] * k1[None, :]) % m).astype(F32)
    b = (2.0 * math.pi / n) * ((j[:, None] * k2[None, :]) % n).astype(F32)
    ca, sa, cb, sb = jnp.cos(a), jnp.sin(a), jnp.cos(b), jnp.sin(b)
    c = ca[:, :, None] * cb[:, None, :] - sa[:, :, None] * sb[:, None, :]
    s = sa[:, :, None] * cb[:, None, :] + ca[:, :, None] * sb[:, None, :]
    return (c.reshape(n, n) * sc).T, (s.reshape(n, n) * sc).T


def _fn_kernel(ct_ref, st_ref, u_ref, cs_ref, z_ref, w_ref, b_ref, y_ref, uc_s, us_s):
    @pl.when(pl.program_id(1) == 0)
    def _():
        for g in range(BRANCH_W // FN_GW):
            cols = slice(g * FN_GW, (g + 1) * FN_GW)
            r = _dot(u_ref[:, cols].astype(BF16), cs_ref[...])
            uc_s[:, cols] = r[:, :FN_GW].astype(BF16)
            us_s[:, cols] = r[:, FN_GW:].astype(BF16)

    f = _dot(ct_ref[...], uc_s[...]) + _dot(st_ref[...], us_s[...])
    y = _dot(f.astype(BF16), w_ref[...]) + b_ref[...]
    y_ref[...] = (y * _silu(z_ref[...])).astype(BF16)


def _fourier(p, cs_ch, ct, nst, fn_w_all, fn_b, layer, t):
    n = p.shape[0]
    bsz = n // t
    tm = _pick(t, (512, 256, 128))
    nt = t // tm
    return pl.pallas_call(
        _fn_kernel,
        grid=(bsz, nt),
        in_specs=[pl.BlockSpec((tm, t), lambda b, i: (i, 0)),
                  pl.BlockSpec((tm, t), lambda b, i: (i, 0)),
                  pl.BlockSpec((t, BRANCH_W), lambda b, i: (b, COL["fn_u"] // 4),
                               pipeline_mode=pl.Buffered(1)),
                  pl.BlockSpec((FN_GW, 2 * FN_GW), lambda b, i: (0, 0)),
                  pl.BlockSpec((tm, BRANCH_W), lambda b, i: (b * nt + i, COL["fn_z"] // 4)),
                  pl.BlockSpec((None, BRANCH_W, BRANCH_W), lambda b, i: (layer, 0, 0)),
                  pl.BlockSpec((1, BRANCH_W), lambda b, i: (0, 0))],
        out_specs=pl.BlockSpec((tm, BRANCH_W), lambda b, i: (b * nt + i, 0)),
        out_shape=jax.ShapeDtypeStruct((n, BRANCH_W), BF16),
        scratch_shapes=[pltpu.VMEM((t, BRANCH_W), BF16), pltpu.VMEM((t, BRANCH_W), BF16)],
        compiler_params=_cparams("fourier", ("parallel", "arbitrary")),
        name="fourier",
    )(ct, nst, p, cs_ch, p, fn_w_all, fn_b)


def _seq_spec(rows, index_map, buffers=2):
    return pl.BlockSpec((rows, HD), index_map, pipeline_mode=pl.Buffered(buffers))


def _chunk_masks():
    i = lax.broadcasted_iota(jnp.int32, (CHUNK, CHUNK), 0)
    j = lax.broadcasted_iota(jnp.int32, (CHUNK, CHUNK), 1)
    incl = (i >= j, i <= j)
    strict = (i > j, i < j)
    return incl, strict


def _as_bf16_mask(m):
    return jnp.where(m, 1.0, 0.0).astype(BF16)


def _softplus(x):
    return jnp.maximum(x, 0.0) + _log1p_unit(jnp.exp(-jnp.abs(x)))


def _log1p_unit(x):
    return jnp.log(1.0 + x)


def _head_rms_gate(o, g_ref, z_ref):
    y = o * lax.rsqrt(jnp.mean(o * o, axis=-1, keepdims=True) + EPS) * g_ref[...]
    return y * _silu(z_ref[...])


def _aligned(x, m):
    return x if isinstance(x, int) else pl.multiple_of(x, m)


def _dn_parts(alog_ref, dtb_ref, q_ref, k_ref, v_ref, ab_ref, cq_ref, ck_ref, cv_ref,
              qs, ks, vs, m_s, n_s, qp_s, gl_s, o_s, group):
    h = pl.program_id(1)
    t = q_ref.shape[0]
    nchunk = t // CHUNK
    row = lax.broadcasted_iota(jnp.int32, (t, HD), 0)

    def conv(x_ref, c_ref):
        x = x_ref[...]
        xm = jnp.where(row == 0, 0.0, pltpu.roll(x, 1, 0))
        xp = jnp.where(row == t - 1, 0.0, pltpu.roll(x, t - 1, 0))
        y = xm * c_ref[0:1, :] + x * c_ref[1:2, :] + xp * c_ref[2:3, :]
        return _silu(y)

    q = conv(q_ref, cq_ref)
    qs[...] = q * lax.rsqrt(jnp.sum(q * q, axis=-1, keepdims=True) + EPS) * HD ** -0.5
    k = conv(k_ref, ck_ref)
    ks[...] = k * lax.rsqrt(jnp.sum(k * k, axis=-1, keepdims=True) + EPS)
    vs[...] = conv(v_ref, cv_ref)

    incl, _ = _chunk_masks()
    incl_b = tuple(_as_bf16_mask(m) for m in incl)
    ri = lax.broadcasted_iota(jnp.int32, (CHUNK, HD), 0)
    li = lax.broadcasted_iota(jnp.int32, (CHUNK, HD), 1)
    cj = li % CHUNK
    incl2 = (ri >= cj, ri <= cj)
    strict2 = (ri > cj, ri < cj)
    right = li >= CHUNK
    eye_right = jnp.where(li == ri + CHUNK, 1.0, 0.0)
    sel_r = lax.broadcasted_iota(jnp.int32, (HD, 2 * HD), 0)
    sel_c = lax.broadcasted_iota(jnp.int32, (HD, 2 * HD), 1)
    sels = [_as_bf16_mask(sel_r == (sel_c // HD) * 2 * HEADS + d * HEADS + h) for d in range(2)]
    neg_a = [-jnp.exp(jnp.full((1, HD), alog_ref[d, h], F32)) for d in range(2)]
    dtb = [dtb_ref[d, h] for d in range(2)]
    aligned = _aligned
    slot_of = lambda g: (g % 2) * group

    def step(d, ci, pos, s):
        rows = pl.ds(aligned(ci * CHUNK, CHUNK), CHUNK)
        prow = pl.ds(aligned(pos * CHUNK, CHUNK), CHUNK)
        mrows = pl.ds(aligned(pos * HD, HD), HD)
        s_b16 = s.astype(BF16)
        o_s[d, rows, :] = o_s[d, rows, :] + _dot(qp_s[d, prow, :], s_b16)
        g_last = gl_s[d, pl.ds(aligned(pos * 8, 8), 8), :][0:1, :]
        return s * g_last + (n_s[d, mrows, :] - _dot(m_s[d, mrows, :], s_b16))

    def prep(g, between):
        base = (g * group, nchunk - (g + 1) * group)
        rows_g = [pl.ds(aligned(b0 * CHUNK, CHUNK), group * CHUNK) for b0 in base]
        q_g = [qs[r, :] for r in rows_g]
        k_g = [ks[r, :] for r in rows_g]
        v_g = [vs[r, :] for r in rows_g]
        ab = [_dot_mask(ab_ref[rows_g[d], :], sels[d]) for d in range(2)]
        gb_all = [neg_a[d] * _softplus(ab[d][:, :HD] + dtb[d]) for d in range(2)]
        bt_all = [jax.nn.sigmoid(ab[d][:, HD:]) for d in range(2)]
        chains = [(c, d) for c in range(group) for d in range(2)]
        sl = lambda c: slice(c * CHUNK, (c + 1) * CHUNK)
        kcbs = [k_g[d][sl(c)].astype(BF16) for c, d in chains]
        kk2 = [_dot_nt(kb_, jnp.concatenate([kb_, kb_], axis=0)) for kb_ in kcbs]
        qk = [_dot_nt(q_g[d][sl(c)].astype(BF16), kb_) for (c, d), kb_ in zip(chains, kcbs)]
        gcs = [_mask_dot(incl_b[d], gb_all[d][sl(c)]) for c, d in chains]
        between(0)
        grs = [jnp.concatenate([gc, gc], axis=0).T[:CHUNK, :] for gc in gcs]
        decays = [jnp.where(incl2[d], jnp.exp(gc - gr), 0.0)
                  for (c, d), gc, gr in zip(chains, gcs, grs)]
        zs = [jnp.where(strict2[d] & ~right, -(kk * bt_all[d][sl(c)] * dec), 0.0) + eye_right
              for (c, d), kk, dec in zip(chains, kk2, decays)]
        def extend(z):
            z_hi, z_lo = _split2(z)
            x_hi, x_lo = z_hi[:, :CHUNK], z_lo[:, :CHUNK]
            both = _dot(x_hi, jnp.concatenate([z_hi, z_lo], axis=1))
            return both[:, :HD] + (both[:, HD:] + _dot(x_lo, z_hi)) + jnp.where(right, z, 0.0)

        for level in range(6):
            zs = [extend(z) for z in zs]
            if level in (1, 3):
                between((level + 1) // 2)
        zero_rows = jnp.zeros((CHUNK, 2 * HD), BF16)
        egs = [jnp.exp(gc) for gc in gcs]
        g_lasts = [gc[CHUNK - 1:CHUNK, :] if d == 0 else gc[0:1, :] for (c, d), gc in zip(chains, gcs)]
        wus = [_dot(z.astype(BF16), jnp.concatenate(
                   [zero_rows,
                    jnp.concatenate([(k_g[d][sl(c)] * bt_all[d][sl(c)] * eg).astype(BF16),
                                     (v_g[d][sl(c)] * bt_all[d][sl(c)]).astype(BF16)], axis=1)],
                   axis=0)).astype(BF16)
               for (c, d), z, eg in zip(chains, zs, egs)]
        between(3)
        mns = [_dot_tn((k_g[d][sl(c)] * jnp.exp(gl - gc)).astype(BF16), wu)
               for (c, d), gc, gl, wu in zip(chains, gcs, g_lasts, wus)]
        qos = [_dot(jnp.where(incl[d], qk_ * dec[:, :CHUNK], 0.0).astype(BF16), wu)
               for (c, d), qk_, dec, wu in zip(chains, qk, decays, wus)]
        for (c, d), eg, gl, mn, qo in zip(chains, egs, g_lasts, mns, qos):
            ci, pos = base[d] + c, slot_of(g) + c
            mrows = pl.ds(aligned(pos * HD, HD), HD)
            m_s[d, mrows, :] = mn[:, :HD].astype(BF16)
            n_s[d, mrows, :] = mn[:, HD:]
            qp_s[d, pl.ds(aligned(pos * CHUNK, CHUNK), CHUNK), :] = (
                q_g[d][sl(c)] * eg - qo[:, :HD]).astype(BF16)
            o_s[d, pl.ds(aligned(ci * CHUNK, CHUNK), CHUNK), :] = qo[:, HD:]
            gl_s[d, pl.ds(aligned(pos * 8, 8), 8), :] = jnp.broadcast_to(jnp.exp(gl), (8, HD))

    def recurrence_steps(g, state):
        def between(u):
            if u < group:
                state[0] = step(0, g * group + u, slot_of(g) + u, state[0])
                state[1] = step(1, nchunk - 1 - g * group - u, slot_of(g) + group - 1 - u, state[1])
        return between

    return prep, recurrence_steps


def _hg_parts(q_ref, f_refs, i_ref, llb_ref, l1m_ref, oml_ref, o_s, group):
    t = q_ref.shape[0]
    nchunk = t // CHUNK
    nsub = CHUNK // SUB

    def gates(d, rows):
        f = f_refs[d][rows, :]
        e_f = jnp.exp(-jnp.abs(f))
        lsig = jnp.minimum(f, 0.0) - _log1p_unit(e_f)
        a = llb_ref[d:d + 1, :]
        b = l1m_ref[d:d + 1, :] + lsig
        log_f = jnp.maximum(a, b) + _log1p_unit(jnp.exp(-jnp.abs(a - b)))
        k = oml_ref[d:d + 1, :] * (jnp.where(f >= 0.0, e_f, 1.0) * (1.0 / (1.0 + e_f)))
        return _silu(q_ref[rows, :]), k, log_f, i_ref[rows, :]

    incl, _ = _chunk_masks()
    incl_b = tuple(_as_bf16_mask(m) for m in incl)
    sub_i = lax.broadcasted_iota(jnp.int32, (SUB, HD), 0)
    half_i = lax.broadcasted_iota(jnp.int32, (HALF, HD), 0)

    def factored(q_rows, g_rows, k_rows, gk_rows, ref, q_keep=None):
        q_t = q_rows * jnp.exp(g_rows - ref)
        k_t = k_rows * jnp.exp(ref - gk_rows)
        if q_keep is not None:
            q_t, k_t = jnp.where(q_keep, q_t, 0.0), jnp.where(q_keep, 0.0, k_t)
        return q_t.astype(BF16), k_t.astype(BF16)

    def trip(g, state):
        base = (g * group, nchunk - (g + 1) * group)
        work = {}

        def stage_gates():
            rows_g = [pl.ds(_aligned(b0 * CHUNK, CHUNK), group * CHUNK) for b0 in base]
            q_g, k_g, lf_g, v_g = zip(*[gates(d, rows_g[d]) for d in range(2)])
            chains = [(d, u if d == 0 else group - 1 - u) for u in range(group) for d in range(2)]
            o_inter, pair_acc, jobs = [], [], []
            for cidx, (d, pos) in enumerate(chains):
                chunk_body(cidx, d, slice(pos * CHUNK, (pos + 1) * CHUNK), q_g[d], k_g[d], lf_g[d], v_g[d],
                           state, o_inter, pair_acc, jobs)
            work.update(chains=chains, o_inter=o_inter, pair_acc=pair_acc, jobs=jobs)

        def stage_scores():
            work["scores"] = [_dot_nt(q_t, k_t).astype(BF16) for _, _, (q_t, k_t), _ in work["jobs"]]

        def stage_values():
            jobs = work["jobs"]
            outs = [_dot(a, v_rows) for a, (_, _, _, v_rows) in zip(work["scores"], jobs)]
            for cidx, (d, pos) in enumerate(work["chains"]):
                blocks = []
                for ib in range(nsub):
                    acc = jnp.concatenate(work["pair_acc"][cidx][2 * ib:2 * ib + 2], axis=0)
                    for o_job, (jc, jb, _, _) in zip(outs, jobs):
                        if (jc, jb) == (cidx, ib):
                            acc = acc + o_job
                    blocks.append(acc)
                rows = pl.ds(_aligned((base[d] + pos) * CHUNK, CHUNK), CHUNK)
                o_s[d, rows, :] = work["o_inter"][cidx] + jnp.concatenate(blocks, axis=0)

        return [stage_gates, stage_scores, stage_values]

    def chunk_body(cidx, d, sl, q_g, k_g, lf_g, v_g, state, o_inter, pair_acc, jobs):
        s_t = state[d]
        qc, kc, vc = q_g[sl], k_g[sl], v_g[sl]
        vb = vc.astype(BF16)
        gc = _mask_dot(incl_b[d], lf_g[sl])
        g_last = gc[CHUNK - 1:CHUNK, :] if d == 0 else gc[0:1, :]
        o_inter.append(_dot_nt((qc * jnp.exp(gc)).astype(BF16), s_t.astype(BF16)))
        k_dec = kc * jnp.exp(g_last - gc)
        state[d] = s_t * jnp.exp(g_last) + _dot_tn(vb, k_dec.astype(BF16))
        for ib in range(nsub):
            r0 = ib * SUB
            blk = slice(r0, r0 + SUB)
            mid = r0 + HALF
            ref = gc[mid - 1:mid, :] if d == 0 else gc[mid:mid + 1, :]
            q_keep = (sub_i >= HALF) if d == 0 else (sub_i < HALF)
            jobs.append((cidx, ib, factored(qc[blk], gc[blk], kc[blk], gc[blk], ref, q_keep), vb[blk]))
            if d == 0 and ib > 0:
                rng, ref = slice(0, r0), gc[r0 - 1:r0, :]
            elif d == 1 and ib < nsub - 1:
                rng, ref = slice(r0 + SUB, CHUNK), gc[r0 + SUB:r0 + SUB + 1, :]
            else:
                continue
            jobs.append((cidx, ib, factored(qc[blk], gc[blk], kc[rng], gc[rng], ref), vb[rng]))
        halves = []
        for h0 in range(0, CHUNK, HALF):
            q_h, g_h = qc[h0:h0 + HALF, :], gc[h0:h0 + HALF, :]
            acc_h = jnp.zeros((HALF, HD), F32)
            for jj in range(HALF):
                j = h0 + jj
                term = q_h * kc[j:j + 1, :] * jnp.exp(g_h - gc[j:j + 1, :])
                keep = (half_i >= jj) if d == 0 else (half_i <= jj)
                acc_h = acc_h + (jnp.sum(jnp.where(keep, term, 0.0), axis=-1, keepdims=True)
                                 * vc[j:j + 1, :])
            halves.append(acc_h)
        pair_acc.append(halves)

    return trip


def _recur_kernel(alog_ref, dtb_ref,
                  dq_ref, dk_ref, dv_ref, dz_ref, dab_ref, cq_ref, ck_ref, cv_ref, dng_ref,
                  ds0f_ref, ds0b_ref,
                  hq_ref, hf0_ref, hf1_ref, hi_ref, hz_ref, llb_ref, l1m_ref, oml_ref, hng_ref,
                  hs0f_ref, hs0b_ref,
                  ydn_ref, dsf_ref, dsb_ref, yhg_ref, hsf_ref, hsb_ref,
                  qs, ks, vs, m_s, n_s, qp_s, gl_s, do_s, ho_s, *, group):
    ngroup = (dq_ref.shape[0] // CHUNK) // group
    dn_prep, dn_steps = _dn_parts(alog_ref, dtb_ref, dq_ref, dk_ref, dv_ref, dab_ref, cq_ref, ck_ref,
                                  cv_ref, qs, ks, vs, m_s, n_s, qp_s, gl_s, do_s, group)
    hg_trip = _hg_parts(hq_ref, (hf0_ref, hf1_ref), hi_ref, llb_ref, l1m_ref, oml_ref, ho_s, group)

    def run_trip(g, dn_state, hg_state):
        recur = dn_steps(g - 1, dn_state) if dn_state is not None else (lambda u: None)
        stages = hg_trip(g, hg_state)

        def between(u):
            recur(u)
            if u < len(stages):
                stages[u]()
        dn_prep(g, between)

    hg_state = [hs0f_ref[0, 0], hs0b_ref[0, 0]]
    run_trip(0, None, hg_state)

    def body(g, carry):
        dn_state, hg_st = list(carry[:2]), list(carry[2:])
        run_trip(g, dn_state, hg_st)
        return tuple(dn_state) + tuple(hg_st)

    carry = lax.fori_loop(1, ngroup, body, (ds0f_ref[0, 0], ds0b_ref[0, 0]) + tuple(hg_state))
    dn_state = list(carry[:2])
    last = dn_steps(ngroup - 1, dn_state)
    for u in range(group):
        last(u)
    dsf_ref[0, 0], dsb_ref[0, 0] = dn_state
    hsf_ref[0, 0], hsb_ref[0, 0] = carry[2], carry[3]
    ydn_ref[...] = _head_rms_gate(do_s[0] + do_s[1], dng_ref, dz_ref).astype(BF16)
    yhg_ref[...] = _head_rms_gate(ho_s[0] + ho_s[1], hng_ref, hz_ref).astype(BF16)


def _recurrent_mixers(p, conv_all, layer, a_log, dt_bias, dn_norm, hg_lbs, hg_norm, dn_s0, hg_s0, t):
    n = p.shape[0]
    bsz = n // t
    nchunk = t // CHUNK
    group = _pick(nchunk, (4, 2, 1))
    col = lambda name, bufs=1: _seq_spec(t, lambda b, h: (b, COL[name] + h), bufs)
    state = pl.BlockSpec((1, 1, HD, HD), lambda b, h: (b, h, 0, 0))
    smem = pl.BlockSpec(memory_space=pltpu.SMEM)
    conv = lambda off: pl.BlockSpec((None, 3, HD), lambda b, h: (layer, 0, off + h))
    lbs = pl.BlockSpec((2, HD), lambda b, h: (0, h))
    gain = pl.BlockSpec((1, HD), lambda b, h: (0, 0))
    y_spec = pl.BlockSpec((t, HD), lambda b, h: (b, h))
    y_shape = jax.ShapeDtypeStruct((n, BRANCH_W), BF16)
    st_shape = jax.ShapeDtypeStruct((bsz, HEADS, HD, HD), F32)
    seq_f32 = pltpu.VMEM((t, HD), F32)
    dir_f32 = pltpu.VMEM((2, t, HD), F32)
    slots = 2 * group
    return pl.pallas_call(
        functools.partial(_recur_kernel, group=group),
        grid=(bsz, HEADS),
        in_specs=[smem, smem,
                  col("dn_q", 2), col("dn_k", 2), col("dn_v", 2), col("dn_z"),
                  _seq_spec(t, lambda b, h: (b, COL["dn_ab"]), 1),
                  conv(0), conv(HEADS), conv(2 * HEADS), gain, state, state,
                  col("hg_q"), col("hg_f0"), col("hg_f1"), col("hg_i"), col("hg_z"),
                  lbs, lbs, lbs, gain, state, state],
        out_specs=[y_spec, state, state, y_spec, state, state],
        out_shape=[y_shape, st_shape, st_shape, y_shape, st_shape, st_shape],
        scratch_shapes=[seq_f32, seq_f32, seq_f32,
                        pltpu.VMEM((2, slots * HD, HD), BF16), pltpu.VMEM((2, slots * HD, HD), F32),
                        pltpu.VMEM((2, slots * CHUNK, HD), BF16), pltpu.VMEM((2, slots * 8, HD), F32),
                        dir_f32, dir_f32],
        compiler_params=_cparams("recurrent_mixers", ("parallel", "parallel")),
        name="recurrent_mixers",
    )(a_log, dt_bias, p, p, p, p, p, conv_all, conv_all, conv_all, dn_norm, *dn_s0,
      p, p, p, p, p, *hg_lbs, hg_norm, *hg_s0)


def _rope_tables(t):
    pos = jnp.arange(t)
    row = (pos // GRID_W).astype(F32)
    col = (pos % GRID_W).astype(F32)
    n = DA_DH // 4
    inv = ROPE_THETA ** (-jnp.arange(n, dtype=F32) / n)
    ar, ac = row[:, None] * inv, col[:, None] * inv
    zero = jnp.zeros_like(ar)
    cos = jnp.concatenate([jnp.cos(ar), jnp.cos(ar), jnp.cos(ac), jnp.cos(ac)], axis=-1)
    s_up = jnp.concatenate([zero, jnp.sin(ar), zero, jnp.sin(ac)], axis=-1)
    s_dn = jnp.concatenate([-jnp.sin(ar), zero, -jnp.sin(ac), zero], axis=-1)
    tile = lambda a: jnp.concatenate([a, a], axis=-1)
    return tile(cos), tile(s_up), tile(s_dn)


def _rope(x, cos, s_up, s_dn):
    half = DA_DH // 4
    return x * cos + pltpu.roll(x, half, 1) * s_up + pltpu.roll(x, HD - half, 1) * s_dn


def _da_kernel(lam_ref, q_ref, kl_ref, kc_ref, vl_ref, vc_ref, z_ref, cq_ref, uq_ref, dq_ref,
               ck_ref, uk_ref, dk_ref, ng_ref, y_ref, k_s, vt_s, s_s, *, t_lat, out_scale):
    nkb, kblk, _ = k_s.shape
    nlat = t_lat // kblk

    @pl.when(pl.program_id(2) == 0)
    def _():
        for j in range(nkb):
            if j < nlat:
                rows = slice(j * kblk, (j + 1) * kblk)
                k = _rope(kl_ref[rows, :], ck_ref[rows, :], uk_ref[rows, :], dk_ref[rows, :])
                v = vl_ref[rows, :]
            else:
                rows = slice((j - nlat) * kblk, (j - nlat + 1) * kblk)
                k, v = kc_ref[rows, :], vc_ref[rows, :]
            k_s[j] = k.astype(BF16)
            vt_s[j] = v.T.astype(BF16)

    q = q_ref[...]
    if t_lat:
        q = _rope(q, cq_ref[...], uq_ref[...], dq_ref[...])
    q = q * (DA_DH ** -0.5 * LOG2E)
    lane = lax.broadcasted_iota(jnp.int32, q.shape, 1)
    qms = [jnp.where(lane < DA_DH, q, 0.0).astype(BF16), jnp.where(lane >= DA_DH, q, 0.0).astype(BF16)]
    tq = q.shape[0]
    m = [None, None]
    l = [jnp.zeros((1, tq), F32) for _ in range(2)]
    acc = [jnp.zeros((HD, tq), F32) for _ in range(2)]

    def scores(mp, j):
        s = _dot_nt(k_s[j], qms[mp])
        s_s[mp, j] = s
        bm = jnp.max(s, axis=0, keepdims=True)
        m[mp] = bm if m[mp] is None else jnp.maximum(m[mp], bm)

    def values(mp, j):
        e = jnp.exp2(s_s[mp, j] - m[mp])
        l[mp] = l[mp] + jnp.sum(e, axis=0, keepdims=True)
        acc[mp] = acc[mp] + _dot(vt_s[j], e.astype(BF16))

    for j in range(nkb):
        scores(0, j)
    lead = min(4, nkb)
    for j in range(lead):
        scores(1, j)
    for j in range(nkb):
        if j + lead < nkb:
            scores(1, j + lead)
        values(0, j)
    for j in range(nkb):
        values(1, j)
    o = (acc[0] * (1.0 / l[0]) - acc[1] * (lam_ref[0] * (1.0 / l[1]))).T
    y = o * lax.rsqrt(jnp.mean(o * o, axis=-1, keepdims=True) + EPS) * ng_ref[...]
    y_ref[...] = (y * out_scale * _silu(z_ref[...])).astype(BF16)


def _diff_attention(lam, p_q, p_lat, p_ctx, rope, norm_g, t, t_ctx, out_scale, latent):
    n = p_q.shape[0]
    bsz = n // t
    tq = _pick(t, (512, 256, 128))
    nq = t // tq
    t_lat = t if latent else 0
    t_kl = t if latent else t_ctx
    cos, s_up, s_dn = rope
    kblk = 256 if (t_lat % 256 == 0 and t_ctx % 256 == 0) else 128
    nkb = (t_lat + t_ctx) // kblk
    col = lambda name: (lambda b, h, i: (b, COL[name] + h))
    qcol = lambda name: (lambda b, h, i: (b * nq + i, COL[name] + h))
    tab_q = pl.BlockSpec((tq, HD), lambda b, h, i: (i, 0))
    tab_k = _seq_spec(t_kl, lambda b, h, i: (0, 0), 1)
    return pl.pallas_call(
        functools.partial(_da_kernel, t_lat=t_lat, out_scale=out_scale),
        grid=(bsz, HEADS, nq),
        in_specs=[pl.BlockSpec(memory_space=pltpu.SMEM),
                  pl.BlockSpec((tq, HD), qcol("da_q")),
                  _seq_spec(t_kl, col("da_k")), _seq_spec(t_ctx, col("da_k")),
                  _seq_spec(t_kl, col("da_v")), _seq_spec(t_ctx, col("da_v")),
                  pl.BlockSpec((tq, HD), qcol("da_z")),
                  tab_q, tab_q, tab_q, tab_k, tab_k, tab_k,
                  pl.BlockSpec((1, HD), lambda b, h, i: (0, 0))],
        out_specs=pl.BlockSpec((tq, HD), lambda b, h, i: (b * nq + i, h)),
        out_shape=jax.ShapeDtypeStruct((n, BRANCH_W), BF16),
        scratch_shapes=[pltpu.VMEM((nkb, kblk, HD), BF16), pltpu.VMEM((nkb, HD, kblk), BF16),
                        pltpu.VMEM((2, nkb, kblk, tq), F32)],
        compiler_params=_cparams("diff_attention", ("parallel", "parallel", "arbitrary")),
        name="diff_attention",
    )(lam, p_q, p_lat, p_ctx, p_lat, p_ctx, p_q, cos, s_up, s_dn, cos, s_up, s_dn, norm_g)


def _merge_kernel(y0_ref, y1_ref, y2_ref, y3_ref, gl_ref, wb_ref, wo_ref, x_ref, mod_ref, fg_ref,
                  o_ref, *, final):
    d = x_ref.shape[1]
    acc = None
    for k, y_ref in enumerate((y0_ref, y1_ref, y2_ref, y3_ref)):
        c = jax.nn.sigmoid(gl_ref[:, k * d:(k + 1) * d]) * _dot(y_ref[...], wb_ref[k])
        acc = c if acc is None else acc + c
    x = x_ref[...] + mod_ref[0, 2:3, :] * _dot(acc.astype(BF16), wo_ref[...])
    if final:
        x = x * lax.rsqrt(jnp.mean(x * x, axis=-1, keepdims=True) + EPS) * fg_ref[...]
    o_ref[...] = x


def _merge(ys, p, wb_all, wo_all, layer, x2, mod, final_g, rows_per_seg, final):
    n, d = x2.shape
    tm = _pick(rows_per_seg, (256, 128))
    ysp = pl.BlockSpec((tm, BRANCH_W), lambda i: (i, 0))
    return pl.pallas_call(
        functools.partial(_merge_kernel, final=final),
        grid=(n // tm,),
        in_specs=[ysp, ysp, ysp, ysp,
                  pl.BlockSpec((tm, N_BRANCH * d), lambda i: (i, 0)),
                  pl.BlockSpec((None, N_BRANCH, BRANCH_W, d), lambda i: (layer, 0, 0, 0),
                               pipeline_mode=pl.Buffered(1)),
                  pl.BlockSpec((None, d, d), lambda i: (layer, 0, 0), pipeline_mode=pl.Buffered(1)),
                  pl.BlockSpec((tm, d), lambda i: (i, 0)),
                  pl.BlockSpec((1, 3, d), lambda i: (i * tm // rows_per_seg, 0, 0)),
                  pl.BlockSpec((1, d), lambda i: (0, 0))],
        out_specs=pl.BlockSpec((tm, d), lambda i: (i, 0)),
        out_shape=jax.ShapeDtypeStruct((n, d), F32),
        compiler_params=_cparams("merge", ("parallel",)),
        name="merge",
    )(*ys, p, wb_all, wo_all, x2, mod, final_g)


def kernel(x, c, ctx, c_ctx, norm_g, w_ada, b_ada, w_in, fn_w, fn_b, dn_conv, dn_a_log, dn_dt_bias,
           dn_norm, hg_lb_logits, hg_norm, da_lambda, da_norm, w_branch, w_out, final_g):
    bsz, t, d = x.shape
    t_ctx = ctx.shape[1]
    depth = w_in.shape[0]
    assert d == N_BRANCH * BRANCH_W and t % CHUNK == 0 and t_ctx % CHUNK == 0 and bsz + 1 <= 8

    w_in_r = _transposed_w_in(w_in)
    wb16, wo16, fnw16 = w_branch.astype(BF16), w_out.astype(BF16), fn_w.astype(BF16)

    c8 = jnp.concatenate([c, c_ctx[None, :], jnp.zeros((8 - bsz - 1, d), F32)], axis=0)
    mod = _ada(c8, w_ada, b_ada).reshape(depth, 8, 3, d)

    lb_all = jnp.cumsum(jax.nn.softmax(hg_lb_logits.astype(F32), axis=1), axis=1)
    lb_all = lb_all - lb_all[:, :1]
    log_lb, log_1m_lb, one_m_lb = jnp.log(lb_all), jnp.log1p(-lb_all), 1.0 - lb_all

    c_ch, s_ch = _dft_cos_sin(FN_GW)
    cs_ch = jnp.concatenate([c_ch, s_ch], axis=-1).astype(BF16)
    dft = {}
    for tt in (t, t_ctx):
        ct, st = _dft_cos_sin(tt)
        dft[tt] = (ct.astype(BF16), (-st).astype(BF16))
    rope_l = _rope_tables(t)
    rope_c = tuple(a[:t_ctx] for a in rope_l)

    xl = x.reshape(bsz * t, d)
    xc = ctx.reshape(bsz * t_ctx, d)
    zstate = jnp.zeros((bsz, HEADS, HD, HD), F32)
    for l in range(depth):
        last = l == depth - 1
        g_l = norm_g[l][None, :]
        mod_l, mod_c = mod[l, :bsz], mod[l, bsz:bsz + 1]
        pl_ = _proj(xl, mod_l, g_l, w_in_r, l, t)
        pc_ = _proj(xc, mod_c, g_l, w_in_r, l, bsz * t_ctx)

        fn_b_l = fn_b[l][None, :]
        y_fn_l = _fourier(pl_, cs_ch, *dft[t], fnw16, fn_b_l, l, t)

        dn_n, hg_n, da_n = dn_norm[l][None, :], hg_norm[l][None, :], da_norm[l][None, :]
        lbs = (log_lb[:, l], log_1m_lb[:, l], one_m_lb[:, l])
        rec_args = (dn_conv, l, dn_a_log[l], dn_dt_bias[l], dn_n, lbs, hg_n)
        y_dn_c, s_f, s_b, y_hg_c, h_f, h_b = _recurrent_mixers(
            pc_, *rec_args, (zstate, zstate), (zstate, zstate), t_ctx)
        y_dn_l, _, _, y_hg_l, _, _ = _recurrent_mixers(pl_, *rec_args, (s_f, s_b), (h_f, h_b), t)

        lam_init = 0.8 - 0.6 * math.exp(-0.3 * l)
        lp = da_lambda[l].astype(F32)
        lam = (jnp.exp(jnp.sum(lp[0] * lp[1])) - jnp.exp(jnp.sum(lp[2] * lp[3])) + lam_init).reshape(1)
        y_da_l = _diff_attention(lam, pl_, pl_, pc_, rope_l, da_n, t, t_ctx, 1.0 - lam_init, True)

        fg = final_g[None, :]
        new_xl = _merge((y_fn_l, y_dn_l, y_hg_l, y_da_l), pl_, wb16, wo16, l, xl, mod_l, fg, t, last)
        if not last:
            y_fn_c = _fourier(pc_, cs_ch, *dft[t_ctx], fnw16, fn_b_l, l, t_ctx)
            y_da_c = _diff_attention(lam, pc_, pc_, pc_, rope_c, da_n, t_ctx, t_ctx,
                                     1.0 - lam_init, False)
            xc = _merge((y_fn_c, y_dn_c, y_hg_c, y_da_c), pc_, wb16, wo16, l, xc, mod_c, fg,
                        bsz * t_ctx, False)
        xl = new_xl
    return xl.reshape(bsz, t, d)
```

```python
import functools
import math

import jax
import jax.numpy as jnp
from jax import lax
from jax.experimental import pallas as pl
from jax.experimental.pallas import tpu as pltpu

F32 = jnp.float32
BF16 = jnp.bfloat16

EPS = 1e-6
N_BRANCH = 4
HEADS = 4
HD = 128
BRANCH_W = HEADS * HD
FN_GW = 128
CHUNK = 64
SUB = 16
HALF = SUB // 2
GRID_W = 64
ROPE_THETA = 10000.0
DA_DH = 64
LOG2E = 1.4426950408889634

COL = dict(fn_u=64, fn_z=68, dn_q=72, dn_k=76, dn_v=80, dn_z=84, hg_q=88, hg_f0=92, hg_f1=96,
           hg_i=100, hg_z=104, da_q=108, da_k=112, da_v=116, da_z=120, dn_ab=124)
PROJ_W = 126 * 128
PROJ_TN = 7 * 256
GATE_W_OFF = 7696
AB_OFF = 3072

MIB = 1024 * 1024
VMEM_MIB = dict(adaln_mod=48, in_proj=61, fourier=48, recurrent_mixers=58, diff_attention=56, merge=56)


def _cparams(name, sem):
    return pltpu.CompilerParams(dimension_semantics=sem, vmem_limit_bytes=VMEM_MIB[name] * MIB)


def _silu(x):
    return x * jax.nn.sigmoid(x)


def _dot(a, b):
    return jnp.dot(a, b, preferred_element_type=F32)


def _dot_nt(a, b):
    return lax.dot_general(a, b, (((1,), (1,)), ((), ())), preferred_element_type=F32)


def _dot_tn(a, b):
    return lax.dot_general(a, b, (((0,), (0,)), ((), ())), preferred_element_type=F32)


def _split2(x):
    hi = x.astype(BF16)
    return hi, (x - hi.astype(F32)).astype(BF16)


def _split3(x):
    hi = x.astype(BF16)
    r = x - hi.astype(F32)
    mid = r.astype(BF16)
    return hi, mid, (r - mid.astype(F32)).astype(BF16)


def _mask_dot(mask, x):
    n = x.shape[1]
    r = _dot(mask, jnp.concatenate(_split3(x), axis=1))
    return r[:, :n] + (r[:, n:2 * n] + r[:, 2 * n:])


def _dot_mask(x, mask):
    hi, mid, lo = _split3(x)
    return _dot(hi, mask) + (_dot(mid, mask) + _dot(lo, mask))


def _pick(n, cands):
    for c in cands:
        if n % c == 0:
            return c
    return n


def _ada_kernel(c_ref, w_ref, b_ref, o_ref):
    sc = _silu(c_ref[...])
    o_ref[0] = _dot(sc.astype(BF16), w_ref[0].astype(BF16)) + b_ref[0]


def _ada(c8, w_ada, b_ada):
    depth, d, d3 = w_ada.shape
    tn = _pick(d3, (1536, 768, 512, 256, 128))
    return pl.pallas_call(
        _ada_kernel,
        grid=(depth, d3 // tn),
        in_specs=[pl.BlockSpec((8, d), lambda l, n: (0, 0)),
                  pl.BlockSpec((1, d, tn), lambda l, n: (l, 0, n)),
                  pl.BlockSpec((1, 1, tn), lambda l, n: (l, 0, n))],
        out_specs=pl.BlockSpec((1, 8, tn), lambda l, n: (l, 0, n)),
        out_shape=jax.ShapeDtypeStruct((depth, 8, d3), F32),
        compiler_params=_cparams("adaln_mod", ("parallel", "parallel")),
        name="adaln_mod",
    )(c8, w_ada, b_ada.reshape(depth, 1, d3))


def _transposed_w_in(w_in):
    depth, d, _ = w_in.shape
    n_ab = 4 * HEADS
    wt = jnp.swapaxes(w_in, 1, 2)
    parts = [wt[:, GATE_W_OFF:], wt[:, :AB_OFF], wt[:, AB_OFF + n_ab:GATE_W_OFF], wt[:, AB_OFF:AB_OFF + n_ab]]
    return jnp.concatenate([part.astype(BF16) for part in parts]
                           + [jnp.zeros((depth, 2 * HD - n_ab, d), BF16)], axis=1)


def _proj_kernel(x_ref, mod_ref, g_ref, w_ref, o_ref, h_ref):
    @pl.when(pl.program_id(1) == 0)
    def _():
        x = x_ref[...]
        y = x * lax.rsqrt(jnp.mean(x * x, axis=-1, keepdims=True) + EPS) * g_ref[...]
        h = (y * (1.0 + mod_ref[0, 1:2, :]) + mod_ref[0, 0:1, :]).astype(BF16)
        h_ref[...] = h
        o_ref[...] = _dot_nt(h, w_ref[...])

    @pl.when(pl.program_id(1) != 0)
    def _():
        o_ref[...] = _dot_nt(h_ref[...], w_ref[...])


def _proj(x2, mod, g, wt_all, layer, rows_per_seg):
    n, d = x2.shape
    tm = _pick(rows_per_seg, (1024, 512, 256, 128))
    tn = PROJ_TN
    return pl.pallas_call(
        _proj_kernel,
        grid=(n // tm, PROJ_W // tn),
        in_specs=[pl.BlockSpec((tm, d), lambda i, j: (i, 0)),
                  pl.BlockSpec((1, 3, d), lambda i, j: (i * tm // rows_per_seg, 0, 0)),
                  pl.BlockSpec((1, d), lambda i, j: (0, 0)),
                  pl.BlockSpec((None, tn, d), lambda i, j: (layer, j, 0))],
        out_specs=pl.BlockSpec((tm, tn), lambda i, j: (i, j)),
        out_shape=jax.ShapeDtypeStruct((n, PROJ_W), F32),
        scratch_shapes=[pltpu.VMEM((tm, d), BF16)],
        compiler_params=_cparams("in_proj", ("parallel", "arbitrary")),
        name="in_proj",
    )(x2, mod, g, wt_all)


def _dft_cos_sin(n):
    j = jnp.arange(n, dtype=jnp.int32)
    sc = n ** -0.5
    if n <= 1024:
        ang = (2.0 * math.pi / n) * ((j[:, None] * j[None, :]) % n).astype(F32)
        return jnp.cos(ang) * sc, jnp.sin(ang) * sc
    m = n // 64
    k1 = jnp.arange(m, dtype=jnp.int32)
    k2 = jnp.arange(64, dtype=jnp.int32)
    a = (2.0 * math.pi / m) * ((j[:, None] * k1[None, :]) % m).astype(F32)
    b = (2.0 * math.pi / n) * ((j[:, None] * k2[None, :]) % n).astype(F32)
    ca, sa, cb, sb = jnp.cos(a), jnp.sin(a), jnp.cos(b), jnp.sin(b)
    c = ca[:, :, None] * cb[:, None, :] - sa[:, :, None] * sb[:, None, :]
    s = sa[:, :, None] * cb[:, None, :] + ca[:, :, None] * sb[:, None, :]
    return (c.reshape(n, n) * sc).T, (s.reshape(n, n) * sc).T


def _fn_kernel(ct_ref, st_ref, u_ref, cs_ref, z_ref, w_ref, b_ref, y_ref, uc_s, us_s):
    @pl.when(pl.program_id(1) == 0)
    def _():
        for g in range(BRANCH_W // FN_GW):
            cols = slice(g * FN_GW, (g + 1) * FN_GW)
            r = _dot(u_ref[:, cols].astype(BF16), cs_ref[...])
            uc_s[:, cols] = r[:, :FN_GW].astype(BF16)
            us_s[:, cols] = r[:, FN_GW:].astype(BF16)

    f = _dot(ct_ref[...], uc_s[...]) + _dot(st_ref[...], us_s[...])
    y = _dot(f.astype(BF16), w_ref[...]) + b_ref[...]
    y_ref[...] = (y * _silu(z_ref[...])).astype(BF16)


def _fourier(p, cs_ch, ct, nst, fn_w_all, fn_b, layer, t):
    n = p.shape[0]
    bsz = n // t
    tm = _pick(t, (512, 256, 128))
    nt = t // tm
    return pl.pallas_call(
        _fn_kernel,
        grid=(bsz, nt),
        in_specs=[pl.BlockSpec((tm, t), lambda b, i: (i, 0)),
                  pl.BlockSpec((tm, t), lambda b, i: (i, 0)),
                  pl.BlockSpec((t, BRANCH_W), lambda b, i: (b, COL["fn_u"] // 4),
                               pipeline_mode=pl.Buffered(1)),
                  pl.BlockSpec((FN_GW, 2 * FN_GW), lambda b, i: (0, 0)),
                  pl.BlockSpec((tm, BRANCH_W), lambda b, i: (b * nt + i, COL["fn_z"] // 4)),
                  pl.BlockSpec((None, BRANCH_W, BRANCH_W), lambda b, i: (layer, 0, 0)),
                  pl.BlockSpec((1, BRANCH_W), lambda b, i: (0, 0))],
        out_specs=pl.BlockSpec((tm, BRANCH_W), lambda b, i: (b * nt + i, 0)),
        out_shape=jax.ShapeDtypeStruct((n, BRANCH_W), BF16),
        scratch_shapes=[pltpu.VMEM((t, BRANCH_W), BF16), pltpu.VMEM((t, BRANCH_W), BF16)],
        compiler_params=_cparams("fourier", ("parallel", "arbitrary")),
        name="fourier",
    )(ct, nst, p, cs_ch, p, fn_w_all, fn_b)


def _seq_spec(rows, index_map, buffers=2):
    return pl.BlockSpec((rows, HD), index_map, pipeline_mode=pl.Buffered(buffers))


def _chunk_masks():
    i = lax.broadcasted_iota(jnp.int32, (CHUNK, CHUNK), 0)
    j = lax.broadcasted_iota(jnp.int32, (CHUNK, CHUNK), 1)
    incl = (i >= j, i <= j)
    strict = (i > j, i < j)
    return incl, strict


def _as_bf16_mask(m):
    return jnp.where(m, 1.0, 0.0).astype(BF16)


def _softplus(x):
    return jnp.maximum(x, 0.0) + _log1p_unit(jnp.exp(-jnp.abs(x)))


def _log1p_unit(x):
    return jnp.log(1.0 + x)


def _head_rms_gate(o, g_ref, z_ref):
    y = o * lax.rsqrt(jnp.mean(o * o, axis=-1, keepdims=True) + EPS) * g_ref[...]
    return y * _silu(z_ref[...])


def _aligned(x, m):
    return x if isinstance(x, int) else pl.multiple_of(x, m)


def _dn_parts(alog_ref, dtb_ref, q_ref, k_ref, v_ref, ab_ref, cq_ref, ck_ref, cv_ref,
              qs, ks, vs, m_s, n_s, qp_s, gl_s, o_s, group):
    h = pl.program_id(1)
    t = q_ref.shape[0]
    nchunk = t // CHUNK
    row = lax.broadcasted_iota(jnp.int32, (t, HD), 0)

    def conv(x_ref, c_ref):
        x = x_ref[...]
        xm = jnp.where(row == 0, 0.0, pltpu.roll(x, 1, 0))
        xp = jnp.where(row == t - 1, 0.0, pltpu.roll(x, t - 1, 0))
        y = xm * c_ref[0:1, :] + x * c_ref[1:2, :] + xp * c_ref[2:3, :]
        return _silu(y)

    q = conv(q_ref, cq_ref)
    qs[...] = q * lax.rsqrt(jnp.sum(q * q, axis=-1, keepdims=True) + EPS) * HD ** -0.5
    k = conv(k_ref, ck_ref)
    ks[...] = k * lax.rsqrt(jnp.sum(k * k, axis=-1, keepdims=True) + EPS)
    vs[...] = conv(v_ref, cv_ref)

    incl, _ = _chunk_masks()
    incl_b = tuple(_as_bf16_mask(m) for m in incl)
    ri = lax.broadcasted_iota(jnp.int32, (CHUNK, HD), 0)
    li = lax.broadcasted_iota(jnp.int32, (CHUNK, HD), 1)
    cj = li % CHUNK
    incl2 = (ri >= cj, ri <= cj)
    strict2 = (ri > cj, ri < cj)
    right = li >= CHUNK
    eye_right = jnp.where(li == ri + CHUNK, 1.0, 0.0)
    sel_r = lax.broadcasted_iota(jnp.int32, (HD, 2 * HD), 0)
    sel_c = lax.broadcasted_iota(jnp.int32, (HD, 2 * HD), 1)
    sels = [_as_bf16_mask(sel_r == (sel_c // HD) * 2 * HEADS + d * HEADS + h) for d in range(2)]
    neg_a = [-jnp.exp(jnp.full((1, HD), alog_ref[d, h], F32)) for d in range(2)]
    dtb = [dtb_ref[d, h] for d in range(2)]
    aligned = _aligned
    slot_of = lambda g: (g % 2) * group

    def step(d, ci, pos, s):
        rows = pl.ds(aligned(ci * CHUNK, CHUNK), CHUNK)
        prow = pl.ds(aligned(pos * CHUNK, CHUNK), CHUNK)
        mrows = pl.ds(aligned(pos * HD, HD), HD)
        s_b16 = s.astype(BF16)
        o_s[d, rows, :] = o_s[d, rows, :] + _dot(qp_s[d, prow, :], s_b16)
        g_last = gl_s[d, pl.ds(aligned(pos * 8, 8), 8), :][0:1, :]
        return s * g_last + (n_s[d, mrows, :] - _dot(m_s[d, mrows, :], s_b16))

    def prep(g, between):
        base = (g * group, nchunk - (g + 1) * group)
        rows_g = [pl.ds(aligned(b0 * CHUNK, CHUNK), group * CHUNK) for b0 in base]
        q_g = [qs[r, :] for r in rows_g]
        k_g = [ks[r, :] for r in rows_g]
        v_g = [vs[r, :] for r in rows_g]
        ab = [_dot_mask(ab_ref[rows_g[d], :], sels[d]) for d in range(2)]
        gb_all = [neg_a[d] * _softplus(ab[d][:, :HD] + dtb[d]) for d in range(2)]
        bt_all = [jax.nn.sigmoid(ab[d][:, HD:]) for d in range(2)]
        chains = [(c, d) for c in range(group) for d in range(2)]
        sl = lambda c: slice(c * CHUNK, (c + 1) * CHUNK)
        kcbs = [k_g[d][sl(c)].astype(BF16) for c, d in chains]
        kk2 = [_dot_nt(kb_, jnp.concatenate([kb_, kb_], axis=0)) for kb_ in kcbs]
        qk = [_dot_nt(q_g[d][sl(c)].astype(BF16), kb_) for (c, d), kb_ in zip(chains, kcbs)]
        gcs = [_mask_dot(incl_b[d], gb_all[d][sl(c)]) for c, d in chains]
        between(0)
        grs = [jnp.concatenate([gc, gc], axis=0).T[:CHUNK, :] for gc in gcs]
        decays = [jnp.where(incl2[d], jnp.exp(gc - gr), 0.0)
                  for (c, d), gc, gr in zip(chains, gcs, grs)]
        zs = [jnp.where(strict2[d] & ~right, -(kk * bt_all[d][sl(c)] * dec), 0.0) + eye_right
              for (c, d), kk, dec in zip(chains, kk2, decays)]
        def extend(z):
            z_hi, z_lo = _split2(z)
            x_hi, x_lo = z_hi[:, :CHUNK], z_lo[:, :CHUNK]
            both = _dot(x_hi, jnp.concatenate([z_hi, z_lo], axis=1))
            return both[:, :HD] + (both[:, HD:] + _dot(x_lo, z_hi)) + jnp.where(right, z, 0.0)

        for level in range(6):
            zs = [extend(z) for z in zs]
            if level in (1, 3):
                between((level + 1) // 2)
        zero_rows = jnp.zeros((CHUNK, 2 * HD), BF16)
        egs = [jnp.exp(gc) for gc in gcs]
        g_lasts = [gc[CHUNK - 1:CHUNK, :] if d == 0 else gc[0:1, :] for (c, d), gc in zip(chains, gcs)]
        wus = [_dot(z.astype(BF16), jnp.concatenate(
                   [zero_rows,
                    jnp.concatenate([(k_g[d][sl(c)] * bt_all[d][sl(c)] * eg).astype(BF16),
                                     (v_g[d][sl(c)] * bt_all[d][sl(c)]).astype(BF16)], axis=1)],
                   axis=0)).astype(BF16)
               for (c, d), z, eg in zip(chains, zs, egs)]
        between(3)
        mns = [_dot_tn((k_g[d][sl(c)] * jnp.exp(gl - gc)).astype(BF16), wu)
               for (c, d), gc, gl, wu in zip(chains, gcs, g_lasts, wus)]
        qos = [_dot(jnp.where(incl[d], qk_ * dec[:, :CHUNK], 0.0).astype(BF16), wu)
               for (c, d), qk_, dec, wu in zip(chains, qk, decays, wus)]
        for (c, d), eg, gl, mn, qo in zip(chains, egs, g_lasts, mns, qos):
            ci, pos = base[d] + c, slot_of(g) + c
            mrows = pl.ds(aligned(pos * HD, HD), HD)
            m_s[d, mrows, :] = mn[:, :HD].astype(BF16)
            n_s[d, mrows, :] = mn[:, HD:]
            qp_s[d, pl.ds(aligned(pos * CHUNK, CHUNK), CHUNK), :] = (
                q_g[d][sl(c)] * eg - qo[:, :HD]).astype(BF16)
            o_s[d, pl.ds(aligned(ci * CHUNK, CHUNK), CHUNK), :] = qo[:, HD:]
            gl_s[d, pl.ds(aligned(pos * 8, 8), 8), :] = jnp.broadcast_to(jnp.exp(gl), (8, HD))

    def recurrence_steps(g, state):
        def between(u):
            if u < group:
                state[0] = step(0, g * group + u, slot_of(g) + u, state[0])
                state[1] = step(1, nchunk - 1 - g * group - u, slot_of(g) + group - 1 - u, state[1])
        return between

    return prep, recurrence_steps


def _hg_parts(q_ref, f_refs, i_ref, llb_ref, l1m_ref, oml_ref, o_s, group):
    t = q_ref.shape[0]
    nchunk = t // CHUNK
    nsub = CHUNK // SUB

    def gates(d, rows):
        f = f_refs[d][rows, :]
        e_f = jnp.exp(-jnp.abs(f))
        lsig = jnp.minimum(f, 0.0) - _log1p_unit(e_f)
        a = llb_ref[d:d + 1, :]
        b = l1m_ref[d:d + 1, :] + lsig
        log_f = jnp.maximum(a, b) + _log1p_unit(jnp.exp(-jnp.abs(a - b)))
        k = oml_ref[d:d + 1, :] * (jnp.where(f >= 0.0, e_f, 1.0) * (1.0 / (1.0 + e_f)))
        return _silu(q_ref[rows, :]), k, log_f, i_ref[rows, :]

    incl, _ = _chunk_masks()
    incl_b = tuple(_as_bf16_mask(m) for m in incl)
    sub_i = lax.broadcasted_iota(jnp.int32, (SUB, HD), 0)
    half_i = lax.broadcasted_iota(jnp.int32, (HALF, HD), 0)

    def factored(q_rows, g_rows, k_rows, gk_rows, ref, q_keep=None):
        q_t = q_rows * jnp.exp(g_rows - ref)
        k_t = k_rows * jnp.exp(ref - gk_rows)
        if q_keep is not None:
            q_t, k_t = jnp.where(q_keep, q_t, 0.0), jnp.where(q_keep, 0.0, k_t)
        return q_t.astype(BF16), k_t.astype(BF16)

    def trip(g, state):
        base = (g * group, nchunk - (g + 1) * group)
        work = {}

        def stage_gates():
            rows_g = [pl.ds(_aligned(b0 * CHUNK, CHUNK), group * CHUNK) for b0 in base]
            q_g, k_g, lf_g, v_g = zip(*[gates(d, rows_g[d]) for d in range(2)])
            chains = [(d, u if d == 0 else group - 1 - u) for u in range(group) for d in range(2)]
            o_inter, pair_acc, jobs = [], [], []
            for cidx, (d, pos) in enumerate(chains):
                chunk_body(cidx, d, slice(pos * CHUNK, (pos + 1) * CHUNK), q_g[d], k_g[d], lf_g[d], v_g[d],
                           state, o_inter, pair_acc, jobs)
            work.update(chains=chains, o_inter=o_inter, pair_acc=pair_acc, jobs=jobs)

        def stage_scores():
            work["scores"] = [_dot_nt(q_t, k_t).astype(BF16) for _, _, (q_t, k_t), _ in work["jobs"]]

        def stage_values():
            jobs = work["jobs"]
            outs = [_dot(a, v_rows) for a, (_, _, _, v_rows) in zip(work["scores"], jobs)]
            for cidx, (d, pos) in enumerate(work["chains"]):
                blocks = []
                for ib in range(nsub):
                    acc = jnp.concatenate(work["pair_acc"][cidx][2 * ib:2 * ib + 2], axis=0)
                    for o_job, (jc, jb, _, _) in zip(outs, jobs):
                        if (jc, jb) == (cidx, ib):
                            acc = acc + o_job
                    blocks.append(acc)
                rows = pl.ds(_aligned((base[d] + pos) * CHUNK, CHUNK), CHUNK)
                o_s[d, rows, :] = work["o_inter"][cidx] + jnp.concatenate(blocks, axis=0)

        return [stage_gates, stage_scores, stage_values]

    def chunk_body(cidx, d, sl, q_g, k_g, lf_g, v_g, state, o_inter, pair_acc, jobs):
        s_t = state[d]
        qc, kc, vc = q_g[sl], k_g[sl], v_g[sl]
        vb = vc.astype(BF16)
        gc = _mask_dot(incl_b[d], lf_g[sl])
        g_last = gc[CHUNK - 1:CHUNK, :] if d == 0 else gc[0:1, :]
        o_inter.append(_dot_nt((qc * jnp.exp(gc)).astype(BF16), s_t.astype(BF16)))
        k_dec = kc * jnp.exp(g_last - gc)
        state[d] = s_t * jnp.exp(g_last) + _dot_tn(vb, k_dec.astype(BF16))
        for ib in range(nsub):
            r0 = ib * SUB
            blk = slice(r0, r0 + SUB)
            mid = r0 + HALF
            ref = gc[mid - 1:mid, :] if d == 0 else gc[mid:mid + 1, :]
            q_keep = (sub_i >= HALF) if d == 0 else (sub_i < HALF)
            jobs.append((cidx, ib, factored(qc[blk], gc[blk], kc[blk], gc[blk], ref, q_keep), vb[blk]))
            if d == 0 and ib > 0:
                rng, ref = slice(0, r0), gc[r0 - 1:r0, :]
            elif d == 1 and ib < nsub - 1:
                rng, ref = slice(r0 + SUB, CHUNK), gc[r0 + SUB:r0 + SUB + 1, :]
            else:
                continue
            jobs.append((cidx, ib, factored(qc[blk], gc[blk], kc[rng], gc[rng], ref), vb[rng]))
        halves = []
        for h0 in range(0, CHUNK, HALF):
            q_h, g_h = qc[h0:h0 + HALF, :], gc[h0:h0 + HALF, :]
            acc_h = jnp.zeros((HALF, HD), F32)
            for jj in range(HALF):
                j = h0 + jj
                term = q_h * kc[j:j + 1, :] * jnp.exp(g_h - gc[j:j + 1, :])
                keep = (half_i >= jj) if d == 0 else (half_i <= jj)
                acc_h = acc_h + (jnp.sum(jnp.where(keep, term, 0.0), axis=-1, keepdims=True)
                                 * vc[j:j + 1, :])
            halves.append(acc_h)
        pair_acc.append(halves)

    return trip


def _recur_kernel(alog_ref, dtb_ref,
                  dq_ref, dk_ref, dv_ref, dz_ref, dab_ref, cq_ref, ck_ref, cv_ref, dng_ref,
                  ds0f_ref, ds0b_ref,
                  hq_ref, hf0_ref, hf1_ref, hi_ref, hz_ref, llb_ref, l1m_ref, oml_ref, hng_ref,
                  hs0f_ref, hs0b_ref,
                  ydn_ref, dsf_ref, dsb_ref, yhg_ref, hsf_ref, hsb_ref,
                  qs, ks, vs, m_s, n_s, qp_s, gl_s, do_s, ho_s, *, group):
    ngroup = (dq_ref.shape[0] // CHUNK) // group
    dn_prep, dn_steps = _dn_parts(alog_ref, dtb_ref, dq_ref, dk_ref, dv_ref, dab_ref, cq_ref, ck_ref,
                                  cv_ref, qs, ks, vs, m_s, n_s, qp_s, gl_s, do_s, group)
    hg_trip = _hg_parts(hq_ref, (hf0_ref, hf1_ref), hi_ref, llb_ref, l1m_ref, oml_ref, ho_s, group)

    def run_trip(g, dn_state, hg_state):
        recur = dn_steps(g - 1, dn_state) if dn_state is not None else (lambda u: None)
        stages = hg_trip(g, hg_state)

        def between(u):
            recur(u)
            if u < len(stages):
                stages[u]()
        dn_prep(g, between)

    hg_state = [hs0f_ref[0, 0], hs0b_ref[0, 0]]
    run_trip(0, None, hg_state)

    def body(g, carry):
        dn_state, hg_st = list(carry[:2]), list(carry[2:])
        run_trip(g, dn_state, hg_st)
        return tuple(dn_state) + tuple(hg_st)

    carry = lax.fori_loop(1, ngroup, body, (ds0f_ref[0, 0], ds0b_ref[0, 0]) + tuple(hg_state))
    dn_state = list(carry[:2])
    last = dn_steps(ngroup - 1, dn_state)
    for u in range(group):
        last(u)
    dsf_ref[0, 0], dsb_ref[0, 0] = dn_state
    hsf_ref[0, 0], hsb_ref[0, 0] = carry[2], carry[3]
    ydn_ref[...] = _head_rms_gate(do_s[0] + do_s[1], dng_ref, dz_ref).astype(BF16)
    yhg_ref[...] = _head_rms_gate(ho_s[0] + ho_s[1], hng_ref, hz_ref).astype(BF16)


def _recurrent_mixers(p, conv_all, layer, a_log, dt_bias, dn_norm, hg_lbs, hg_norm, dn_s0, hg_s0, t):
    n = p.shape[0]
    bsz = n // t
    nchunk = t // CHUNK
    group = _pick(nchunk, (4, 2, 1))
    col = lambda name, bufs=1: _seq_spec(t, lambda b, h: (b, COL[name] + h), bufs)
    state = pl.BlockSpec((1, 1, HD, HD), lambda b, h: (b, h, 0, 0))
    smem = pl.BlockSpec(memory_space=pltpu.SMEM)
    conv = lambda off: pl.BlockSpec((None, 3, HD), lambda b, h: (layer, 0, off + h))
    lbs = pl.BlockSpec((2, HD), lambda b, h: (0, h))
    gain = pl.BlockSpec((1, HD), lambda b, h: (0, 0))
    y_spec = pl.BlockSpec((t, HD), lambda b, h: (b, h))
    y_shape = jax.ShapeDtypeStruct((n, BRANCH_W), BF16)
    st_shape = jax.ShapeDtypeStruct((bsz, HEADS, HD, HD), F32)
    seq_f32 = pltpu.VMEM((t, HD), F32)
    dir_f32 = pltpu.VMEM((2, t, HD), F32)
    slots = 2 * group
    return pl.pallas_call(
        functools.partial(_recur_kernel, group=group),
        grid=(bsz, HEADS),
        in_specs=[smem, smem,
                  col("dn_q", 2), col("dn_k", 2), col("dn_v", 2), col("dn_z"),
                  _seq_spec(t, lambda b, h: (b, COL["dn_ab"]), 1),
                  conv(0), conv(HEADS), conv(2 * HEADS), gain, state, state,
                  col("hg_q"), col("hg_f0"), col("hg_f1"), col("hg_i"), col("hg_z"),
                  lbs, lbs, lbs, gain, state, state],
        out_specs=[y_spec, state, state, y_spec, state, state],
        out_shape=[y_shape, st_shape, st_shape, y_shape, st_shape, st_shape],
        scratch_shapes=[seq_f32, seq_f32, seq_f32,
                        pltpu.VMEM((2, slots * HD, HD), BF16), pltpu.VMEM((2, slots * HD, HD), F32),
                        pltpu.VMEM((2, slots * CHUNK, HD), BF16), pltpu.VMEM((2, slots * 8, HD), F32),
                        dir_f32, dir_f32],
        compiler_params=_cparams("recurrent_mixers", ("parallel", "parallel")),
        name="recurrent_mixers",
    )(a_log, dt_bias, p, p, p, p, p, conv_all, conv_all, conv_all, dn_norm, *dn_s0,
      p, p, p, p, p, *hg_lbs, hg_norm, *hg_s0)


def _rope_tables(t):
    pos = jnp.arange(t)
    row = (pos // GRID_W).astype(F32)
    col = (pos % GRID_W).astype(F32)
    n = DA_DH // 4
    inv = ROPE_THETA ** (-jnp.arange(n, dtype=F32) / n)
    ar, ac = row[:, None] * inv, col[:, None] * inv
    zero = jnp.zeros_like(ar)
    cos = jnp.concatenate([jnp.cos(ar), jnp.cos(ar), jnp.cos(ac), jnp.cos(ac)], axis=-1)
    s_up = jnp.concatenate([zero, jnp.sin(ar), zero, jnp.sin(ac)], axis=-1)
    s_dn = jnp.concatenate([-jnp.sin(ar), zero, -jnp.sin(ac), zero], axis=-1)
    tile = lambda a: jnp.concatenate([a, a], axis=-1)
    return tile(cos), tile(s_up), tile(s_dn)


def _rope(x, cos, s_up, s_dn):
    half = DA_DH // 4
    return x * cos + pltpu.roll(x, half, 1) * s_up + pltpu.roll(x, HD - half, 1) * s_dn


def _da_kernel(lam_ref, q_ref, kl_ref, kc_ref, vl_ref, vc_ref, z_ref, cq_ref, uq_ref, dq_ref,
               ck_ref, uk_ref, dk_ref, ng_ref, y_ref, k_s, vt_s, s_s, *, t_lat, out_scale):
    nkb, kblk, _ = k_s.shape
    nlat = t_lat // kblk

    @pl.when(pl.program_id(2) == 0)
    def _():
        for j in range(nkb):
            if j < nlat:
                rows = slice(j * kblk, (j + 1) * kblk)
                k = _rope(kl_ref[rows, :], ck_ref[rows, :], uk_ref[rows, :], dk_ref[rows, :])
                v = vl_ref[rows, :]
            else:
                rows = slice((j - nlat) * kblk, (j - nlat + 1) * kblk)
                k, v = kc_ref[rows, :], vc_ref[rows, :]
            k_s[j] = k.astype(BF16)
            vt_s[j] = v.T.astype(BF16)

    q = q_ref[...]
    if t_lat:
        q = _rope(q, cq_ref[...], uq_ref[...], dq_ref[...])
    q = q * (DA_DH ** -0.5 * LOG2E)
    lane = lax.broadcasted_iota(jnp.int32, q.shape, 1)
    qms = [jnp.where(lane < DA_DH, q, 0.0).astype(BF16), jnp.where(lane >= DA_DH, q, 0.0).astype(BF16)]
    tq = q.shape[0]
    m = [None, None]
    l = [jnp.zeros((1, tq), F32) for _ in range(2)]
    acc = [jnp.zeros((HD, tq), F32) for _ in range(2)]

    def scores(mp, j):
        s = _dot_nt(k_s[j], qms[mp])
        s_s[mp, j] = s
        bm = jnp.max(s, axis=0, keepdims=True)
        m[mp] = bm if m[mp] is None else jnp.maximum(m[mp], bm)

    def values(mp, j):
        e = jnp.exp2(s_s[mp, j] - m[mp])
        l[mp] = l[mp] + jnp.sum(e, axis=0, keepdims=True)
        acc[mp] = acc[mp] + _dot(vt_s[j], e.astype(BF16))

    for j in range(nkb):
        scores(0, j)
    lead = min(4, nkb)
    for j in range(lead):
        scores(1, j)
    for j in range(nkb):
        if j + lead < nkb:
            scores(1, j + lead)
        values(0, j)
    for j in range(nkb):
        values(1, j)
    o = (acc[0] * (1.0 / l[0]) - acc[1] * (lam_ref[0] * (1.0 / l[1]))).T
    y = o * lax.rsqrt(jnp.mean(o * o, axis=-1, keepdims=True) + EPS) * ng_ref[...]
    y_ref[...] = (y * out_scale * _silu(z_ref[...])).astype(BF16)


def _diff_attention(lam, p_q, p_lat, p_ctx, rope, norm_g, t, t_ctx, out_scale, latent):
    n = p_q.shape[0]
    bsz = n // t
    tq = _pick(t, (512, 256, 128))
    nq = t // tq
    t_lat = t if latent else 0
    t_kl = t if latent else t_ctx
    cos, s_up, s_dn = rope
    kblk = 256 if (t_lat % 256 == 0 and t_ctx % 256 == 0) else 128
    nkb = (t_lat + t_ctx) // kblk
    col = lambda name: (lambda b, h, i: (b, COL[name] + h))
    qcol = lambda name: (lambda b, h, i: (b * nq + i, COL[name] + h))
    tab_q = pl.BlockSpec((tq, HD), lambda b, h, i: (i, 0))
    tab_k = _seq_spec(t_kl, lambda b, h, i: (0, 0), 1)
    return pl.pallas_call(
        functools.partial(_da_kernel, t_lat=t_lat, out_scale=out_scale),
        grid=(bsz, HEADS, nq),
        in_specs=[pl.BlockSpec(memory_space=pltpu.SMEM),
                  pl.BlockSpec((tq, HD), qcol("da_q")),
                  _seq_spec(t_kl, col("da_k")), _seq_spec(t_ctx, col("da_k")),
                  _seq_spec(t_kl, col("da_v")), _seq_spec(t_ctx, col("da_v")),
                  pl.BlockSpec((tq, HD), qcol("da_z")),
                  tab_q, tab_q, tab_q, tab_k, tab_k, tab_k,
                  pl.BlockSpec((1, HD), lambda b, h, i: (0, 0))],
        out_specs=pl.BlockSpec((tq, HD), lambda b, h, i: (b * nq + i, h)),
        out_shape=jax.ShapeDtypeStruct((n, BRANCH_W), BF16),
        scratch_shapes=[pltpu.VMEM((nkb, kblk, HD), BF16), pltpu.VMEM((nkb, HD, kblk), BF16),
                        pltpu.VMEM((2, nkb, kblk, tq), F32)],
        compiler_params=_cparams("diff_attention", ("parallel", "parallel", "arbitrary")),
        name="diff_attention",
    )(lam, p_q, p_lat, p_ctx, p_lat, p_ctx, p_q, cos, s_up, s_dn, cos, s_up, s_dn, norm_g)


def _merge_kernel(y0_ref, y1_ref, y2_ref, y3_ref, gl_ref, wb_ref, wo_ref, x_ref, mod_ref, fg_ref,
                  o_ref, *, final):
    d = x_ref.shape[1]
    acc = None
    for k, y_ref in enumerate((y0_ref, y1_ref, y2_ref, y3_ref)):
        c = jax.nn.sigmoid(gl_ref[:, k * d:(k + 1) * d]) * _dot(y_ref[...], wb_ref[k])
        acc = c if acc is None else acc + c
    x = x_ref[...] + mod_ref[0, 2:3, :] * _dot(acc.astype(BF16), wo_ref[...])
    if final:
        x = x * lax.rsqrt(jnp.mean(x * x, axis=-1, keepdims=True) + EPS) * fg_ref[...]
    o_ref[...] = x


def _merge(ys, p, wb_all, wo_all, layer, x2, mod, final_g, rows_per_seg, final):
    n, d = x2.shape
    tm = _pick(rows_per_seg, (256, 128))
    ysp = pl.BlockSpec((tm, BRANCH_W), lambda i: (i, 0))
    return pl.pallas_call(
        functools.partial(_merge_kernel, final=final),
        grid=(n // tm,),
        in_specs=[ysp, ysp, ysp, ysp,
                  pl.BlockSpec((tm, N_BRANCH * d), lambda i: (i, 0)),
                  pl.BlockSpec((None, N_BRANCH, BRANCH_W, d), lambda i: (layer, 0, 0, 0),
                               pipeline_mode=pl.Buffered(1)),
                  pl.BlockSpec((None, d, d), lambda i: (layer, 0, 0), pipeline_mode=pl.Buffered(1)),
                  pl.BlockSpec((tm, d), lambda i: (i, 0)),
                  pl.BlockSpec((1, 3, d), lambda i: (i * tm // rows_per_seg, 0, 0)),
                  pl.BlockSpec((1, d), lambda i: (0, 0))],
        out_specs=pl.BlockSpec((tm, d), lambda i: (i, 0)),
        out_shape=jax.ShapeDtypeStruct((n, d), F32),
        compiler_params=_cparams("merge", ("parallel",)),
        name="merge",
    )(*ys, p, wb_all, wo_all, x2, mod, final_g)


def kernel(x, c, ctx, c_ctx, norm_g, w_ada, b_ada, w_in, fn_w, fn_b, dn_conv, dn_a_log, dn_dt_bias,
           dn_norm, hg_lb_logits, hg_norm, da_lambda, da_norm, w_branch, w_out, final_g):
    bsz, t, d = x.shape
    t_ctx = ctx.shape[1]
    depth = w_in.shape[0]
    assert d == N_BRANCH * BRANCH_W and t % CHUNK == 0 and t_ctx % CHUNK == 0 and bsz + 1 <= 8

    w_in_r = _transposed_w_in(w_in)
    wb16, wo16, fnw16 = w_branch.astype(BF16), w_out.astype(BF16), fn_w.astype(BF16)

    c8 = jnp.concatenate([c, c_ctx[None, :], jnp.zeros((8 - bsz - 1, d), F32)], axis=0)
    mod = _ada(c8, w_ada, b_ada).reshape(depth, 8, 3, d)

    lb_all = jnp.cumsum(jax.nn.softmax(hg_lb_logits.astype(F32), axis=1), axis=1)
    lb_all = lb_all - lb_all[:, :1]
    log_lb, log_1m_lb, one_m_lb = jnp.log(lb_all), jnp.log1p(-lb_all), 1.0 - lb_all

    c_ch, s_ch = _dft_cos_sin(FN_GW)
    cs_ch = jnp.concatenate([c_ch, s_ch], axis=-1).astype(BF16)
    dft = {}
    for tt in (t, t_ctx):
        ct, st = _dft_cos_sin(tt)
        dft[tt] = (ct.astype(BF16), (-st).astype(BF16))
    rope_l = _rope_tables(t)
    rope_c = tuple(a[:t_ctx] for a in rope_l)

    xl = x.reshape(bsz * t, d)
    xc = ctx.reshape(bsz * t_ctx, d)
    zstate = jnp.zeros((bsz, HEADS, HD, HD), F32)
    for l in range(depth):
        last = l == depth - 1
        g_l = norm_g[l][None, :]
        mod_l, mod_c = mod[l, :bsz], mod[l, bsz:bsz + 1]
        pl_ = _proj(xl, mod_l, g_l, w_in_r, l, t)
        pc_ = _proj(xc, mod_c, g_l, w_in_r, l, bsz * t_ctx)

        fn_b_l = fn_b[l][None, :]
        y_fn_l = _fourier(pl_, cs_ch, *dft[t], fnw16, fn_b_l, l, t)

        dn_n, hg_n, da_n = dn_norm[l][None, :], hg_norm[l][None, :], da_norm[l][None, :]
        lbs = (log_lb[:, l], log_1m_lb[:, l], one_m_lb[:, l])
        rec_args = (dn_conv, l, dn_a_log[l], dn_dt_bias[l], dn_n, lbs, hg_n)
        y_dn_c, s_f, s_b, y_hg_c, h_f, h_b = _recurrent_mixers(
            pc_, *rec_args, (zstate, zstate), (zstate, zstate), t_ctx)
        y_dn_l, _, _, y_hg_l, _, _ = _recurrent_mixers(pl_, *rec_args, (s_f, s_b), (h_f, h_b), t)

        lam_init = 0.8 - 0.6 * math.exp(-0.3 * l)
        lp = da_lambda[l].astype(F32)
        lam = (jnp.exp(jnp.sum(lp[0] * lp[1])) - jnp.exp(jnp.sum(lp[2] * lp[3])) + lam_init).reshape(1)
        y_da_l = _diff_attention(lam, pl_, pl_, pc_, rope_l, da_n, t, t_ctx, 1.0 - lam_init, True)

        fg = final_g[None, :]
        new_xl = _merge((y_fn_l, y_dn_l, y_hg_l, y_da_l), pl_, wb16, wo16, l, xl, mod_l, fg, t, last)
        if not last:
            y_fn_c = _fourier(pc_, cs_ch, *dft[t_ctx], fnw16, fn_b_l, l, t_ctx)
            y_da_c = _diff_attention(lam, pc_, pc_, pc_, rope_c, da_n, t_ctx, t_ctx,
                                     1.0 - lam_init, False)
            xc = _merge((y_fn_c, y_dn_c, y_hg_c, y_da_c), pc_, wb16, wo16, l, xc, mod_c, fg,
                        bsz * t_ctx, False)
        xl = new_xl
    return xl.reshape(bsz, t, d)
```

```python
import functools
import math

import jax
import jax.numpy as jnp
from jax import lax
from jax.experimental import pallas as pl
from jax.experimental.pallas import tpu as pltpu

F32 = jnp.float32
BF16 = jnp.bfloat16

EPS = 1e-6
N_BRANCH = 4
HEADS = 4
HD = 128
BRANCH_W = HEADS * HD
FN_GW = 128
CHUNK = 64
SUB = 16
HALF = SUB // 2
GRID_W = 64
ROPE_THETA = 10000.0
DA_DH = 64
LOG2E = 1.4426950408889634

COL = dict(fn_u=64, fn_z=68, dn_q=72, dn_k=76, dn_v=80, dn_z=84, hg_q=88, hg_f0=92, hg_f1=96,
           hg_i=100, hg_z=104, da_q=108, da_k=112, da_v=116, da_z=120, dn_ab=124)
PROJ_W = 126 * 128
PROJ_TN = 7 * 256
GATE_W_OFF = 7696
AB_OFF = 3072

MIB = 1024 * 1024
VMEM_MIB = dict(adaln_mod=48, in_proj=61, fourier=48, recurrent_mixers=61, diff_attention=56, merge=56)


def _cparams(name, sem):
    return pltpu.CompilerParams(dimension_semantics=sem, vmem_limit_bytes=VMEM_MIB[name] * MIB)


def _silu(x):
    return x * jax.nn.sigmoid(x)


def _dot(a, b):
    return jnp.dot(a, b, preferred_element_type=F32)


def _dot_nt(a, b):
    return lax.dot_general(a, b, (((1,), (1,)), ((), ())), preferred_element_type=F32)


def _dot_tn(a, b):
    return lax.dot_general(a, b, (((0,), (0,)), ((), ())), preferred_element_type=F32)


def _split2(x):
    hi = x.astype(BF16)
    return hi, (x - hi.astype(F32)).astype(BF16)


def _split3(x):
    hi = x.astype(BF16)
    r = x - hi.astype(F32)
    mid = r.astype(BF16)
    return hi, mid, (r - mid.astype(F32)).astype(BF16)


def _mask_dot(mask, x):
    n = x.shape[1]
    r = _dot(mask, jnp.concatenate(_split3(x), axis=1))
    return r[:, :n] + (r[:, n:2 * n] + r[:, 2 * n:])


def _dot_mask(x, mask):
    hi, mid, lo = _split3(x)
    return _dot(hi, mask) + (_dot(mid, mask) + _dot(lo, mask))


def _pick(n, cands):
    for c in cands:
        if n % c == 0:
            return c
    return n


def _ada_kernel(c_ref, w_ref, b_ref, o_ref):
    sc = _silu(c_ref[...])
    o_ref[0] = _dot(sc.astype(BF16), w_ref[0].astype(BF16)) + b_ref[0]


def _ada(c8, w_ada, b_ada):
    depth, d, d3 = w_ada.shape
    tn = _pick(d3, (1536, 768, 512, 256, 128))
    return pl.pallas_call(
        _ada_kernel,
        grid=(depth, d3 // tn),
        in_specs=[pl.BlockSpec((8, d), lambda l, n: (0, 0)),
                  pl.BlockSpec((1, d, tn), lambda l, n: (l, 0, n)),
                  pl.BlockSpec((1, 1, tn), lambda l, n: (l, 0, n))],
        out_specs=pl.BlockSpec((1, 8, tn), lambda l, n: (l, 0, n)),
        out_shape=jax.ShapeDtypeStruct((depth, 8, d3), F32),
        compiler_params=_cparams("adaln_mod", ("parallel", "parallel")),
        name="adaln_mod",
    )(c8, w_ada, b_ada.reshape(depth, 1, d3))


def _transposed_w_in(w_in):
    depth, d, _ = w_in.shape
    n_ab = 4 * HEADS
    wt = jnp.swapaxes(w_in, 1, 2)
    parts = [wt[:, GATE_W_OFF:], wt[:, :AB_OFF], wt[:, AB_OFF + n_ab:GATE_W_OFF], wt[:, AB_OFF:AB_OFF + n_ab]]
    return jnp.concatenate([part.astype(BF16) for part in parts]
                           + [jnp.zeros((depth, 2 * HD - n_ab, d), BF16)], axis=1)


def _proj_kernel(x_ref, mod_ref, g_ref, w_ref, o_ref, h_ref):
    @pl.when(pl.program_id(1) == 0)
    def _():
        x = x_ref[...]
        y = x * lax.rsqrt(jnp.mean(x * x, axis=-1, keepdims=True) + EPS) * g_ref[...]
        h = (y * (1.0 + mod_ref[0, 1:2, :]) + mod_ref[0, 0:1, :]).astype(BF16)
        h_ref[...] = h
        o_ref[...] = _dot_nt(h, w_ref[...])

    @pl.when(pl.program_id(1) != 0)
    def _():
        o_ref[...] = _dot_nt(h_ref[...], w_ref[...])


def _proj(x2, mod, g, wt_all, layer, rows_per_seg):
    n, d = x2.shape
    tm = _pick(rows_per_seg, (1024, 512, 256, 128))
    tn = PROJ_TN
    return pl.pallas_call(
        _proj_kernel,
        grid=(n // tm, PROJ_W // tn),
        in_specs=[pl.BlockSpec((tm, d), lambda i, j: (i, 0)),
                  pl.BlockSpec((1, 3, d), lambda i, j: (i * tm // rows_per_seg, 0, 0)),
                  pl.BlockSpec((1, d), lambda i, j: (0, 0)),
                  pl.BlockSpec((None, tn, d), lambda i, j: (layer, j, 0))],
        out_specs=pl.BlockSpec((tm, tn), lambda i, j: (i, j)),
        out_shape=jax.ShapeDtypeStruct((n, PROJ_W), F32),
        scratch_shapes=[pltpu.VMEM((tm, d), BF16)],
        compiler_params=_cparams("in_proj", ("parallel", "arbitrary")),
        name="in_proj",
    )(x2, mod, g, wt_all)


def _dft_cos_sin(n):
    j = jnp.arange(n, dtype=jnp.int32)
    sc = n ** -0.5
    if n <= 1024:
        ang = (2.0 * math.pi / n) * ((j[:, None] * j[None, :]) % n).astype(F32)
        return jnp.cos(ang) * sc, jnp.sin(ang) * sc
    m = n // 64
    k1 = jnp.arange(m, dtype=jnp.int32)
    k2 = jnp.arange(64, dtype=jnp.int32)
    a = (2.0 * math.pi / m) * ((j[:, None] * k1[None, :]) % m).astype(F32)
    b = (2.0 * math.pi / n) * ((j[:, None] * k2[None, :]) % n).astype(F32)
    ca, sa, cb, sb = jnp.cos(a), jnp.sin(a), jnp.cos(b), jnp.sin(b)
    c = ca[:, :, None] * cb[:, None, :] - sa[:, :, None] * sb[:, None, :]
    s = sa[:, :, None] * cb[:, None, :] + ca[:, :, None] * sb[:, None, :]
    return (c.reshape(n, n) * sc).T, (s.reshape(n, n) * sc).T


def _fn_kernel(ct_ref, st_ref, u_ref, cs_ref, z_ref, w_ref, b_ref, y_ref, uc_s, us_s):
    @pl.when(pl.program_id(1) == 0)
    def _():
        for g in range(BRANCH_W // FN_GW):
            cols = slice(g * FN_GW, (g + 1) * FN_GW)
            r = _dot(u_ref[:, cols].astype(BF16), cs_ref[...])
            uc_s[:, cols] = r[:, :FN_GW].astype(BF16)
            us_s[:, cols] = r[:, FN_GW:].astype(BF16)

    f = _dot(ct_ref[...], uc_s[...]) + _dot(st_ref[...], us_s[...])
    y = _dot(f.astype(BF16), w_ref[...]) + b_ref[...]
    y_ref[...] = (y * _silu(z_ref[...])).astype(BF16)


def _fourier(p, cs_ch, ct, nst, fn_w_all, fn_b, layer, t):
    n = p.shape[0]
    bsz = n // t
    tm = _pick(t, (512, 256, 128))
    nt = t // tm
    return pl.pallas_call(
        _fn_kernel,
        grid=(bsz, nt),
        in_specs=[pl.BlockSpec((tm, t), lambda b, i: (i, 0)),
                  pl.BlockSpec((tm, t), lambda b, i: (i, 0)),
                  pl.BlockSpec((t, BRANCH_W), lambda b, i: (b, COL["fn_u"] // 4),
                               pipeline_mode=pl.Buffered(1)),
                  pl.BlockSpec((FN_GW, 2 * FN_GW), lambda b, i: (0, 0)),
                  pl.BlockSpec((tm, BRANCH_W), lambda b, i: (b * nt + i, COL["fn_z"] // 4)),
                  pl.BlockSpec((None, BRANCH_W, BRANCH_W), lambda b, i: (layer, 0, 0)),
                  pl.BlockSpec((1, BRANCH_W), lambda b, i: (0, 0))],
        out_specs=pl.BlockSpec((tm, BRANCH_W), lambda b, i: (b * nt + i, 0)),
        out_shape=jax.ShapeDtypeStruct((n, BRANCH_W), BF16),
        scratch_shapes=[pltpu.VMEM((t, BRANCH_W), BF16), pltpu.VMEM((t, BRANCH_W), BF16)],
        compiler_params=_cparams("fourier", ("parallel", "arbitrary")),
        name="fourier",
    )(ct, nst, p, cs_ch, p, fn_w_all, fn_b)


def _seq_spec(rows, index_map, buffers=2):
    return pl.BlockSpec((rows, HD), index_map, pipeline_mode=pl.Buffered(buffers))


def _chunk_masks():
    i = lax.broadcasted_iota(jnp.int32, (CHUNK, CHUNK), 0)
    j = lax.broadcasted_iota(jnp.int32, (CHUNK, CHUNK), 1)
    incl = (i >= j, i <= j)
    strict = (i > j, i < j)
    return incl, strict


def _as_bf16_mask(m):
    return jnp.where(m, 1.0, 0.0).astype(BF16)


def _softplus(x):
    return jnp.maximum(x, 0.0) + _log1p_unit(jnp.exp(-jnp.abs(x)))


def _log1p_unit(x):
    return jnp.log(1.0 + x)


def _head_rms_gate(o, g_ref, z_ref):
    y = o * lax.rsqrt(jnp.mean(o * o, axis=-1, keepdims=True) + EPS) * g_ref[...]
    return y * _silu(z_ref[...])


def _aligned(x, m):
    return x if isinstance(x, int) else pl.multiple_of(x, m)


def _dn_parts(alog_ref, dtb_ref, q_ref, k_ref, v_ref, ab_ref, cq_ref, ck_ref, cv_ref,
              qs, ks, vs, m_s, n_s, qp_s, gl_s, o_s, group):
    h = pl.program_id(1)
    t = q_ref.shape[0]
    nchunk = t // CHUNK
    row = lax.broadcasted_iota(jnp.int32, (t, HD), 0)

    def conv(x_ref, c_ref):
        x = x_ref[...]
        xm = jnp.where(row == 0, 0.0, pltpu.roll(x, 1, 0))
        xp = jnp.where(row == t - 1, 0.0, pltpu.roll(x, t - 1, 0))
        y = xm * c_ref[0:1, :] + x * c_ref[1:2, :] + xp * c_ref[2:3, :]
        return _silu(y)

    q = conv(q_ref, cq_ref)
    qs[...] = q * lax.rsqrt(jnp.sum(q * q, axis=-1, keepdims=True) + EPS) * HD ** -0.5
    k = conv(k_ref, ck_ref)
    ks[...] = k * lax.rsqrt(jnp.sum(k * k, axis=-1, keepdims=True) + EPS)
    vs[...] = conv(v_ref, cv_ref)

    incl, _ = _chunk_masks()
    incl_b = tuple(_as_bf16_mask(m) for m in incl)
    ri = lax.broadcasted_iota(jnp.int32, (CHUNK, HD), 0)
    li = lax.broadcasted_iota(jnp.int32, (CHUNK, HD), 1)
    cj = li % CHUNK
    incl2 = (ri >= cj, ri <= cj)
    strict2 = (ri > cj, ri < cj)
    right = li >= CHUNK
    eye_right = jnp.where(li == ri + CHUNK, 1.0, 0.0)
    sel_r = lax.broadcasted_iota(jnp.int32, (HD, 2 * HD), 0)
    sel_c = lax.broadcasted_iota(jnp.int32, (HD, 2 * HD), 1)
    sels = [_as_bf16_mask(sel_r == (sel_c // HD) * 2 * HEADS + d * HEADS + h) for d in range(2)]
    neg_a = [-jnp.exp(jnp.full((1, HD), alog_ref[d, h], F32)) for d in range(2)]
    dtb = [dtb_ref[d, h] for d in range(2)]
    aligned = _aligned
    slot_of = lambda g: (g % 2) * group

    def step(d, ci, pos, s):
        rows = pl.ds(aligned(ci * CHUNK, CHUNK), CHUNK)
        prow = pl.ds(aligned(pos * CHUNK, CHUNK), CHUNK)
        mrows = pl.ds(aligned(pos * HD, HD), HD)
        s_b16 = s.astype(BF16)
        o_s[d, rows, :] = o_s[d, rows, :] + _dot(qp_s[d, prow, :], s_b16)
        g_last = gl_s[d, pl.ds(aligned(pos * 8, 8), 8), :][0:1, :]
        return s * g_last + (n_s[d, mrows, :] - _dot(m_s[d, mrows, :], s_b16))

    def prep(g, between):
        base = (g * group, nchunk - (g + 1) * group)
        rows_g = [pl.ds(aligned(b0 * CHUNK, CHUNK), group * CHUNK) for b0 in base]
        q_g = [qs[r, :] for r in rows_g]
        k_g = [ks[r, :] for r in rows_g]
        v_g = [vs[r, :] for r in rows_g]
        ab = [_dot_mask(ab_ref[rows_g[d], :], sels[d]) for d in range(2)]
        gb_all = [neg_a[d] * _softplus(ab[d][:, :HD] + dtb[d]) for d in range(2)]
        bt_all = [jax.nn.sigmoid(ab[d][:, HD:]) for d in range(2)]
        chains = [(c, d) for c in range(group) for d in range(2)]
        sl = lambda c: slice(c * CHUNK, (c + 1) * CHUNK)
        kcbs = [k_g[d][sl(c)].astype(BF16) for c, d in chains]
        kk2 = [_dot_nt(kb_, jnp.concatenate([kb_, kb_], axis=0)) for kb_ in kcbs]
        qk = [_dot_nt(q_g[d][sl(c)].astype(BF16), kb_) for (c, d), kb_ in zip(chains, kcbs)]
        gcs = [_mask_dot(incl_b[d], gb_all[d][sl(c)]) for c, d in chains]
        between(0)
        grs = [jnp.concatenate([gc, gc], axis=0).T[:CHUNK, :] for gc in gcs]
        decays = [jnp.where(incl2[d], jnp.exp(gc - gr), 0.0)
                  for (c, d), gc, gr in zip(chains, gcs, grs)]
        zs = [jnp.where(strict2[d] & ~right, -(kk * bt_all[d][sl(c)] * dec), 0.0) + eye_right
              for (c, d), kk, dec in zip(chains, kk2, decays)]
        def extend(z):
            z_hi, z_lo = _split2(z)
            x_hi, x_lo = z_hi[:, :CHUNK], z_lo[:, :CHUNK]
            both = _dot(x_hi, jnp.concatenate([z_hi, z_lo], axis=1))
            return both[:, :HD] + (both[:, HD:] + _dot(x_lo, z_hi)) + jnp.where(right, z, 0.0)

        for level in range(6):
            zs = [extend(z) for z in zs]
            if level in (1, 3):
                between((level + 1) // 2)
        zero_rows = jnp.zeros((CHUNK, 2 * HD), BF16)
        egs = [jnp.exp(gc) for gc in gcs]
        g_lasts = [gc[CHUNK - 1:CHUNK, :] if d == 0 else gc[0:1, :] for (c, d), gc in zip(chains, gcs)]
        wus = [_dot(z.astype(BF16), jnp.concatenate(
                   [zero_rows,
                    jnp.concatenate([(k_g[d][sl(c)] * bt_all[d][sl(c)] * eg).astype(BF16),
                                     (v_g[d][sl(c)] * bt_all[d][sl(c)]).astype(BF16)], axis=1)],
                   axis=0)).astype(BF16)
               for (c, d), z, eg in zip(chains, zs, egs)]
        between(3)
        mns = [_dot_tn((k_g[d][sl(c)] * jnp.exp(gl - gc)).astype(BF16), wu)
               for (c, d), gc, gl, wu in zip(chains, gcs, g_lasts, wus)]
        qos = [_dot(jnp.where(incl[d], qk_ * dec[:, :CHUNK], 0.0).astype(BF16), wu)
               for (c, d), qk_, dec, wu in zip(chains, qk, decays, wus)]
        for (c, d), eg, gl, mn, qo in zip(chains, egs, g_lasts, mns, qos):
            ci, pos = base[d] + c, slot_of(g) + c
            mrows = pl.ds(aligned(pos * HD, HD), HD)
            m_s[d, mrows, :] = mn[:, :HD].astype(BF16)
            n_s[d, mrows, :] = mn[:, HD:]
            qp_s[d, pl.ds(aligned(pos * CHUNK, CHUNK), CHUNK), :] = (
                q_g[d][sl(c)] * eg - qo[:, :HD]).astype(BF16)
            o_s[d, pl.ds(aligned(ci * CHUNK, CHUNK), CHUNK), :] = qo[:, HD:]
            gl_s[d, pl.ds(aligned(pos * 8, 8), 8), :] = jnp.broadcast_to(jnp.exp(gl), (8, HD))

    def recurrence_steps(g, state):
        def between(u):
            if u < group:
                state[0] = step(0, g * group + u, slot_of(g) + u, state[0])
                state[1] = step(1, nchunk - 1 - g * group - u, slot_of(g) + group - 1 - u, state[1])
        return between

    return prep, recurrence_steps


def _hg_parts(q_ref, f_refs, i_ref, llb_ref, l1m_ref, oml_ref, o_s, group):
    t = q_ref.shape[0]
    nchunk = t // CHUNK
    nsub = CHUNK // SUB

    def gates(d, rows):
        f = f_refs[d][rows, :]
        e_f = jnp.exp(-jnp.abs(f))
        lsig = jnp.minimum(f, 0.0) - _log1p_unit(e_f)
        a = llb_ref[d:d + 1, :]
        b = l1m_ref[d:d + 1, :] + lsig
        log_f = jnp.maximum(a, b) + _log1p_unit(jnp.exp(-jnp.abs(a - b)))
        k = oml_ref[d:d + 1, :] * (jnp.where(f >= 0.0, e_f, 1.0) * (1.0 / (1.0 + e_f)))
        return _silu(q_ref[rows, :]), k, log_f, i_ref[rows, :]

    incl, _ = _chunk_masks()
    incl_b = tuple(_as_bf16_mask(m) for m in incl)
    sub_i = lax.broadcasted_iota(jnp.int32, (SUB, HD), 0)
    half_i = lax.broadcasted_iota(jnp.int32, (HALF, HD), 0)

    def factored(q_rows, g_rows, k_rows, gk_rows, ref, q_keep=None):
        q_t = q_rows * jnp.exp(g_rows - ref)
        k_t = k_rows * jnp.exp(ref - gk_rows)
        if q_keep is not None:
            q_t, k_t = jnp.where(q_keep, q_t, 0.0), jnp.where(q_keep, 0.0, k_t)
        return q_t.astype(BF16), k_t.astype(BF16)

    def trip(g, state):
        base = (g * group, nchunk - (g + 1) * group)
        work = {}

        def stage_gates():
            rows_g = [pl.ds(_aligned(b0 * CHUNK, CHUNK), group * CHUNK) for b0 in base]
            q_g, k_g, lf_g, v_g = zip(*[gates(d, rows_g[d]) for d in range(2)])
            chains = [(d, u if d == 0 else group - 1 - u) for u in range(group) for d in range(2)]
            o_inter, pair_acc, jobs = [], [], []
            for cidx, (d, pos) in enumerate(chains):
                chunk_body(cidx, d, slice(pos * CHUNK, (pos + 1) * CHUNK), q_g[d], k_g[d], lf_g[d], v_g[d],
                           state, o_inter, pair_acc, jobs)
            work.update(chains=chains, o_inter=o_inter, pair_acc=pair_acc, jobs=jobs)

        def stage_scores():
            work["scores"] = [_dot_nt(q_t, k_t).astype(BF16) for _, _, (q_t, k_t), _ in work["jobs"]]

        def stage_values():
            jobs = work["jobs"]
            outs = [_dot(a, v_rows) for a, (_, _, _, v_rows) in zip(work["scores"], jobs)]
            for cidx, (d, pos) in enumerate(work["chains"]):
                blocks = []
                for ib in range(nsub):
                    acc = jnp.concatenate(work["pair_acc"][cidx][2 * ib:2 * ib + 2], axis=0)
                    for o_job, (jc, jb, _, _) in zip(outs, jobs):
                        if (jc, jb) == (cidx, ib):
                            acc = acc + o_job
                    blocks.append(acc)
                rows = pl.ds(_aligned((base[d] + pos) * CHUNK, CHUNK), CHUNK)
                o_s[d, rows, :] = work["o_inter"][cidx] + jnp.concatenate(blocks, axis=0)

        return [stage_gates, stage_scores, stage_values]

    def chunk_body(cidx, d, sl, q_g, k_g, lf_g, v_g, state, o_inter, pair_acc, jobs):
        s_t = state[d]
        qc, kc, vc = q_g[sl], k_g[sl], v_g[sl]
        vb = vc.astype(BF16)
        gc = _mask_dot(incl_b[d], lf_g[sl])
        g_last = gc[CHUNK - 1:CHUNK, :] if d == 0 else gc[0:1, :]
        o_inter.append(_dot_nt((qc * jnp.exp(gc)).astype(BF16), s_t.astype(BF16)))
        k_dec = kc * jnp.exp(g_last - gc)
        state[d] = s_t * jnp.exp(g_last) + _dot_tn(vb, k_dec.astype(BF16))
        for ib in range(nsub):
            r0 = ib * SUB
            blk = slice(r0, r0 + SUB)
            mid = r0 + HALF
            ref = gc[mid - 1:mid, :] if d == 0 else gc[mid:mid + 1, :]
            q_keep = (sub_i >= HALF) if d == 0 else (sub_i < HALF)
            jobs.append((cidx, ib, factored(qc[blk], gc[blk], kc[blk], gc[blk], ref, q_keep), vb[blk]))
            if d == 0 and ib > 0:
                rng, ref = slice(0, r0), gc[r0 - 1:r0, :]
            elif d == 1 and ib < nsub - 1:
                rng, ref = slice(r0 + SUB, CHUNK), gc[r0 + SUB:r0 + SUB + 1, :]
            else:
                continue
            jobs.append((cidx, ib, factored(qc[blk], gc[blk], kc[rng], gc[rng], ref), vb[rng]))
        halves = []
        for h0 in range(0, CHUNK, HALF):
            q_h, g_h = qc[h0:h0 + HALF, :], gc[h0:h0 + HALF, :]
            acc_h = jnp.zeros((HALF, HD), F32)
            for jj in range(HALF):
                j = h0 + jj
                term = q_h * kc[j:j + 1, :] * jnp.exp(g_h - gc[j:j + 1, :])
                keep = (half_i >= jj) if d == 0 else (half_i <= jj)
                acc_h = acc_h + (jnp.sum(jnp.where(keep, term, 0.0), axis=-1, keepdims=True)
                                 * vc[j:j + 1, :])
            halves.append(acc_h)
        pair_acc.append(halves)

    return trip


def _recur_kernel(alog_ref, dtb_ref,
                  dq_ref, dk_ref, dv_ref, dz_ref, dab_ref, cq_ref, ck_ref, cv_ref, dng_ref,
                  ds0f_ref, ds0b_ref,
                  hq_ref, hf0_ref, hf1_ref, hi_ref, hz_ref, llb_ref, l1m_ref, oml_ref, hng_ref,
                  hs0f_ref, hs0b_ref,
                  ydn_ref, dsf_ref, dsb_ref, yhg_ref, hsf_ref, hsb_ref,
                  qs, ks, vs, m_s, n_s, qp_s, gl_s, do_s, ho_s, *, group):
    ngroup = (dq_ref.shape[0] // CHUNK) // group
    dn_prep, dn_steps = _dn_parts(alog_ref, dtb_ref, dq_ref, dk_ref, dv_ref, dab_ref, cq_ref, ck_ref,
                                  cv_ref, qs, ks, vs, m_s, n_s, qp_s, gl_s, do_s, group)
    hg_trip = _hg_parts(hq_ref, (hf0_ref, hf1_ref), hi_ref, llb_ref, l1m_ref, oml_ref, ho_s, group)

    def run_trip(g, dn_state, hg_state):
        recur = dn_steps(g - 1, dn_state) if dn_state is not None else (lambda u: None)
        stages = hg_trip(g, hg_state)

        def between(u):
            recur(u)
            if u < len(stages):
                stages[u]()
        dn_prep(g, between)

    hg_state = [hs0f_ref[0, 0], hs0b_ref[0, 0]]
    run_trip(0, None, hg_state)

    def body(g, carry):
        dn_state, hg_st = list(carry[:2]), list(carry[2:])
        run_trip(g, dn_state, hg_st)
        return tuple(dn_state) + tuple(hg_st)

    carry = lax.fori_loop(1, ngroup, body, (ds0f_ref[0, 0], ds0b_ref[0, 0]) + tuple(hg_state))
    dn_state = list(carry[:2])
    last = dn_steps(ngroup - 1, dn_state)
    for u in range(group):
        last(u)
    dsf_ref[0, 0], dsb_ref[0, 0] = dn_state
    hsf_ref[0, 0], hsb_ref[0, 0] = carry[2], carry[3]
    ydn_ref[...] = _head_rms_gate(do_s[0] + do_s[1], dng_ref, dz_ref).astype(BF16)
    yhg_ref[...] = _head_rms_gate(ho_s[0] + ho_s[1], hng_ref, hz_ref).astype(BF16)


def _recurrent_mixers(p, conv_all, layer, a_log, dt_bias, dn_norm, hg_lbs, hg_norm, dn_s0, hg_s0, t):
    n = p.shape[0]
    bsz = n // t
    nchunk = t // CHUNK
    group = _pick(nchunk, (4, 2, 1))
    col = lambda name, bufs=1: _seq_spec(t, lambda b, h: (b, COL[name] + h), bufs)
    state = pl.BlockSpec((1, 1, HD, HD), lambda b, h: (b, h, 0, 0))
    smem = pl.BlockSpec(memory_space=pltpu.SMEM)
    conv = lambda off: pl.BlockSpec((None, 3, HD), lambda b, h: (layer, 0, off + h))
    lbs = pl.BlockSpec((2, HD), lambda b, h: (0, h))
    gain = pl.BlockSpec((1, HD), lambda b, h: (0, 0))
    y_spec = pl.BlockSpec((t, HD), lambda b, h: (b, h))
    y_shape = jax.ShapeDtypeStruct((n, BRANCH_W), BF16)
    st_shape = jax.ShapeDtypeStruct((bsz, HEADS, HD, HD), F32)
    seq_f32 = pltpu.VMEM((t, HD), F32)
    dir_f32 = pltpu.VMEM((2, t, HD), F32)
    slots = 2 * group
    return pl.pallas_call(
        functools.partial(_recur_kernel, group=group),
        grid=(bsz, HEADS),
        in_specs=[smem, smem,
                  col("dn_q", 2), col("dn_k", 2), col("dn_v", 2), col("dn_z"),
                  _seq_spec(t, lambda b, h: (b, COL["dn_ab"]), 1),
                  conv(0), conv(HEADS), conv(2 * HEADS), gain, state, state,
                  col("hg_q"), col("hg_f0", 2), col("hg_f1", 2), col("hg_i"), col("hg_z"),
                  lbs, lbs, lbs, gain, state, state],
        out_specs=[y_spec, state, state, y_spec, state, state],
        out_shape=[y_shape, st_shape, st_shape, y_shape, st_shape, st_shape],
        scratch_shapes=[seq_f32, seq_f32, seq_f32,
                        pltpu.VMEM((2, slots * HD, HD), BF16), pltpu.VMEM((2, slots * HD, HD), F32),
                        pltpu.VMEM((2, slots * CHUNK, HD), BF16), pltpu.VMEM((2, slots * 8, HD), F32),
                        dir_f32, dir_f32],
        compiler_params=_cparams("recurrent_mixers", ("parallel", "parallel")),
        name="recurrent_mixers",
    )(a_log, dt_bias, p, p, p, p, p, conv_all, conv_all, conv_all, dn_norm, *dn_s0,
      p, p, p, p, p, *hg_lbs, hg_norm, *hg_s0)


def _rope_tables(t):
    pos = jnp.arange(t)
    row = (pos // GRID_W).astype(F32)
    col = (pos % GRID_W).astype(F32)
    n = DA_DH // 4
    inv = ROPE_THETA ** (-jnp.arange(n, dtype=F32) / n)
    ar, ac = row[:, None] * inv, col[:, None] * inv
    zero = jnp.zeros_like(ar)
    cos = jnp.concatenate([jnp.cos(ar), jnp.cos(ar), jnp.cos(ac), jnp.cos(ac)], axis=-1)
    s_up = jnp.concatenate([zero, jnp.sin(ar), zero, jnp.sin(ac)], axis=-1)
    s_dn = jnp.concatenate([-jnp.sin(ar), zero, -jnp.sin(ac), zero], axis=-1)
    tile = lambda a: jnp.concatenate([a, a], axis=-1)
    return tile(cos), tile(s_up), tile(s_dn)


def _rope(x, cos, s_up, s_dn):
    half = DA_DH // 4
    return x * cos + pltpu.roll(x, half, 1) * s_up + pltpu.roll(x, HD - half, 1) * s_dn


def _da_kernel(lam_ref, q_ref, kl_ref, kc_ref, vl_ref, vc_ref, z_ref, cq_ref, uq_ref, dq_ref,
               ck_ref, uk_ref, dk_ref, ng_ref, y_ref, k_s, vt_s, s_s, *, t_lat, out_scale):
    nkb, kblk, _ = k_s.shape
    nlat = t_lat // kblk

    @pl.when(pl.program_id(2) == 0)
    def _():
        for j in range(nkb):
            if j < nlat:
                rows = slice(j * kblk, (j + 1) * kblk)
                k = _rope(kl_ref[rows, :], ck_ref[rows, :], uk_ref[rows, :], dk_ref[rows, :])
                v = vl_ref[rows, :]
            else:
                rows = slice((j - nlat) * kblk, (j - nlat + 1) * kblk)
                k, v = kc_ref[rows, :], vc_ref[rows, :]
            k_s[j] = k.astype(BF16)
            vt_s[j] = v.T.astype(BF16)

    q = q_ref[...]
    if t_lat:
        q = _rope(q, cq_ref[...], uq_ref[...], dq_ref[...])
    q = q * (DA_DH ** -0.5 * LOG2E)
    lane = lax.broadcasted_iota(jnp.int32, q.shape, 1)
    qms = [jnp.where(lane < DA_DH, q, 0.0).astype(BF16), jnp.where(lane >= DA_DH, q, 0.0).astype(BF16)]
    tq = q.shape[0]
    m = [None, None]
    l = [jnp.zeros((1, tq), F32) for _ in range(2)]
    acc = [jnp.zeros((HD, tq), F32) for _ in range(2)]

    def scores(mp, j):
        s = _dot_nt(k_s[j], qms[mp])
        s_s[mp, j] = s
        bm = jnp.max(s, axis=0, keepdims=True)
        m[mp] = bm if m[mp] is None else jnp.maximum(m[mp], bm)

    def values(mp, j):
        e = jnp.exp2(s_s[mp, j] - m[mp])
        l[mp] = l[mp] + jnp.sum(e, axis=0, keepdims=True)
        acc[mp] = acc[mp] + _dot(vt_s[j], e.astype(BF16))

    for j in range(nkb):
        scores(0, j)
    lead = min(4, nkb)
    for j in range(lead):
        scores(1, j)
    for j in range(nkb):
        if j + lead < nkb:
            scores(1, j + lead)
        values(0, j)
    for j in range(nkb):
        values(1, j)
    o = (acc[0] * (1.0 / l[0]) - acc[1] * (lam_ref[0] * (1.0 / l[1]))).T
    y = o * lax.rsqrt(jnp.mean(o * o, axis=-1, keepdims=True) + EPS) * ng_ref[...]
    y_ref[...] = (y * out_scale * _silu(z_ref[...])).astype(BF16)


def _diff_attention(lam, p_q, p_lat, p_ctx, rope, norm_g, t, t_ctx, out_scale, latent):
    n = p_q.shape[0]
    bsz = n // t
    tq = _pick(t, (512, 256, 128))
    nq = t // tq
    t_lat = t if latent else 0
    t_kl = t if latent else t_ctx
    cos, s_up, s_dn = rope
    kblk = 256 if (t_lat % 256 == 0 and t_ctx % 256 == 0) else 128
    nkb = (t_lat + t_ctx) // kblk
    col = lambda name: (lambda b, h, i: (b, COL[name] + h))
    qcol = lambda name: (lambda b, h, i: (b * nq + i, COL[name] + h))
    tab_q = pl.BlockSpec((tq, HD), lambda b, h, i: (i, 0))
    tab_k = _seq_spec(t_kl, lambda b, h, i: (0, 0), 1)
    return pl.pallas_call(
        functools.partial(_da_kernel, t_lat=t_lat, out_scale=out_scale),
        grid=(bsz, HEADS, nq),
        in_specs=[pl.BlockSpec(memory_space=pltpu.SMEM),
                  pl.BlockSpec((tq, HD), qcol("da_q")),
                  _seq_spec(t_kl, col("da_k")), _seq_spec(t_ctx, col("da_k")),
                  _seq_spec(t_kl, col("da_v")), _seq_spec(t_ctx, col("da_v")),
                  pl.BlockSpec((tq, HD), qcol("da_z")),
                  tab_q, tab_q, tab_q, tab_k, tab_k, tab_k,
                  pl.BlockSpec((1, HD), lambda b, h, i: (0, 0))],
        out_specs=pl.BlockSpec((tq, HD), lambda b, h, i: (b * nq + i, h)),
        out_shape=jax.ShapeDtypeStruct((n, BRANCH_W), BF16),
        scratch_shapes=[pltpu.VMEM((nkb, kblk, HD), BF16), pltpu.VMEM((nkb, HD, kblk), BF16),
                        pltpu.VMEM((2, nkb, kblk, tq), F32)],
        compiler_params=_cparams("diff_attention", ("parallel", "parallel", "arbitrary")),
        name="diff_attention",
    )(lam, p_q, p_lat, p_ctx, p_lat, p_ctx, p_q, cos, s_up, s_dn, cos, s_up, s_dn, norm_g)


def _merge_kernel(y0_ref, y1_ref, y2_ref, y3_ref, gl_ref, wb_ref, wo_ref, x_ref, mod_ref, fg_ref,
                  o_ref, *, final):
    d = x_ref.shape[1]
    acc = None
    for k, y_ref in enumerate((y0_ref, y1_ref, y2_ref, y3_ref)):
        c = jax.nn.sigmoid(gl_ref[:, k * d:(k + 1) * d]) * _dot(y_ref[...], wb_ref[k])
        acc = c if acc is None else acc + c
    x = x_ref[...] + mod_ref[0, 2:3, :] * _dot(acc.astype(BF16), wo_ref[...])
    if final:
        x = x * lax.rsqrt(jnp.mean(x * x, axis=-1, keepdims=True) + EPS) * fg_ref[...]
    o_ref[...] = x


def _merge(ys, p, wb_all, wo_all, layer, x2, mod, final_g, rows_per_seg, final):
    n, d = x2.shape
    tm = _pick(rows_per_seg, (256, 128))
    ysp = pl.BlockSpec((tm, BRANCH_W), lambda i: (i, 0))
    return pl.pallas_call(
        functools.partial(_merge_kernel, final=final),
        grid=(n // tm,),
        in_specs=[ysp, ysp, ysp, ysp,
                  pl.BlockSpec((tm, N_BRANCH * d), lambda i: (i, 0)),
                  pl.BlockSpec((None, N_BRANCH, BRANCH_W, d), lambda i: (layer, 0, 0, 0),
                               pipeline_mode=pl.Buffered(1)),
                  pl.BlockSpec((None, d, d), lambda i: (layer, 0, 0), pipeline_mode=pl.Buffered(1)),
                  pl.BlockSpec((tm, d), lambda i: (i, 0)),
                  pl.BlockSpec((1, 3, d), lambda i: (i * tm // rows_per_seg, 0, 0)),
                  pl.BlockSpec((1, d), lambda i: (0, 0))],
        out_specs=pl.BlockSpec((tm, d), lambda i: (i, 0)),
        out_shape=jax.ShapeDtypeStruct((n, d), F32),
        compiler_params=_cparams("merge", ("parallel",)),
        name="merge",
    )(*ys, p, wb_all, wo_all, x2, mod, final_g)


def kernel(x, c, ctx, c_ctx, norm_g, w_ada, b_ada, w_in, fn_w, fn_b, dn_conv, dn_a_log, dn_dt_bias,
           dn_norm, hg_lb_logits, hg_norm, da_lambda, da_norm, w_branch, w_out, final_g):
    bsz, t, d = x.shape
    t_ctx = ctx.shape[1]
    depth = w_in.shape[0]
    assert d == N_BRANCH * BRANCH_W and t % CHUNK == 0 and t_ctx % CHUNK == 0 and bsz + 1 <= 8

    w_in_r = _transposed_w_in(w_in)
    wb16, wo16, fnw16 = w_branch.astype(BF16), w_out.astype(BF16), fn_w.astype(BF16)

    c8 = jnp.concatenate([c, c_ctx[None, :], jnp.zeros((8 - bsz - 1, d), F32)], axis=0)
    mod = _ada(c8, w_ada, b_ada).reshape(depth, 8, 3, d)

    lb_all = jnp.cumsum(jax.nn.softmax(hg_lb_logits.astype(F32), axis=1), axis=1)
    lb_all = lb_all - lb_all[:, :1]
    log_lb, log_1m_lb, one_m_lb = jnp.log(lb_all), jnp.log1p(-lb_all), 1.0 - lb_all

    c_ch, s_ch = _dft_cos_sin(FN_GW)
    cs_ch = jnp.concatenate([c_ch, s_ch], axis=-1).astype(BF16)
    dft = {}
    for tt in (t, t_ctx):
        ct, st = _dft_cos_sin(tt)
        dft[tt] = (ct.astype(BF16), (-st).astype(BF16))
    rope_l = _rope_tables(t)
    rope_c = tuple(a[:t_ctx] for a in rope_l)

    xl = x.reshape(bsz * t, d)
    xc = ctx.reshape(bsz * t_ctx, d)
    zstate = jnp.zeros((bsz, HEADS, HD, HD), F32)
    for l in range(depth):
        last = l == depth - 1
        g_l = norm_g[l][None, :]
        mod_l, mod_c = mod[l, :bsz], mod[l, bsz:bsz + 1]
        pl_ = _proj(xl, mod_l, g_l, w_in_r, l, t)
        pc_ = _proj(xc, mod_c, g_l, w_in_r, l, bsz * t_ctx)

        fn_b_l = fn_b[l][None, :]
        y_fn_l = _fourier(pl_, cs_ch, *dft[t], fnw16, fn_b_l, l, t)

        dn_n, hg_n, da_n = dn_norm[l][None, :], hg_norm[l][None, :], da_norm[l][None, :]
        lbs = (log_lb[:, l], log_1m_lb[:, l], one_m_lb[:, l])
        rec_args = (dn_conv, l, dn_a_log[l], dn_dt_bias[l], dn_n, lbs, hg_n)
        y_dn_c, s_f, s_b, y_hg_c, h_f, h_b = _recurrent_mixers(
            pc_, *rec_args, (zstate, zstate), (zstate, zstate), t_ctx)
        y_dn_l, _, _, y_hg_l, _, _ = _recurrent_mixers(pl_, *rec_args, (s_f, s_b), (h_f, h_b), t)

        lam_init = 0.8 - 0.6 * math.exp(-0.3 * l)
        lp = da_lambda[l].astype(F32)
        lam = (jnp.exp(jnp.sum(lp[0] * lp[1])) - jnp.exp(jnp.sum(lp[2] * lp[3])) + lam_init).reshape(1)
        y_da_l = _diff_attention(lam, pl_, pl_, pc_, rope_l, da_n, t, t_ctx, 1.0 - lam_init, True)

        fg = final_g[None, :]
        new_xl = _merge((y_fn_l, y_dn_l, y_hg_l, y_da_l), pl_, wb16, wo16, l, xl, mod_l, fg, t, last)
        if not last:
            y_fn_c = _fourier(pc_, cs_ch, *dft[t_ctx], fnw16, fn_b_l, l, t_ctx)
            y_da_c = _diff_attention(lam, pc_, pc_, pc_, rope_c, da_n, t_ctx, t_ctx,
                                     1.0 - lam_init, False)
            xc = _merge((y_fn_c, y_dn_c, y_hg_c, y_da_c), pc_, wb16, wo16, l, xc, mod_c, fg,
                        bsz * t_ctx, False)
        xl = new_xl
    return xl.reshape(bsz, t, d)
```
